```python
import jax, jax.numpy as jnp
from jax import lax
import numpy as np

D_MODEL = 1024
BATCH = 4
SEQ = 4096
DEPTH = 1

PLE_DIM = 256
D_FF = 4 * D_MODEL
EPS = 1e-6
RET_HEADS = 4
RET_DK = 128
RET_DV = 256
RET_CHUNK = 128
RET_QK = RET_HEADS * RET_DK
RET_V = RET_HEADS * RET_DV
ATT_PATTERNS = ((128, 1), (512, 4), (2048, 16))
N_GROUPS = len(ATT_PATTERNS)
ATT_HEADS = 4
ATT_HD = 128
ATT_W = ATT_HEADS * ATT_HD
ROPE_THETA = 10000.0
N_BRANCH = 2
IN_SIZES = (RET_QK, RET_QK, RET_V, RET_V,
            N_GROUPS * ATT_W, N_GROUPS * ATT_W, N_GROUPS * ATT_W,
            N_BRANCH * D_MODEL)
D_IN = int(sum(IN_SIZES))
IN_OFFSETS = tuple(int(o) for o in np.cumsum(IN_SIZES)[:-1])

kernel_name = "hybrid_gated_retention_dilated_attn_block"


def rmsnorm(x, g):
    xf = x.astype(jnp.float32)
    y = xf * lax.rsqrt(jnp.mean(xf * xf, axis=-1, keepdims=True) + EPS)
    return (y * g.astype(jnp.float32)).astype(x.dtype)


def rotate(x, inv_freq):
    S = x.shape[1]
    pos = jnp.arange(S, dtype=jnp.float32)
    ang = pos[:, None] * inv_freq[None, :]
    cos = jnp.cos(ang)[:, None, :].astype(x.dtype)
    sin = jnp.sin(ang)[:, None, :].astype(x.dtype)
    x1, x2 = jnp.split(x, 2, axis=-1)
    return jnp.concatenate([x1 * cos - x2 * sin, x2 * cos + x1 * sin], axis=-1)


def retention_chunkwise(q, k, v):
    B, S, H, dk = q.shape
    dv = v.shape[-1]
    C = RET_CHUNK
    N = S // C
    dt = v.dtype
    log_g = jnp.log1p(-jnp.exp2(-5.0 - jnp.arange(H, dtype=jnp.float32)))
    idx = jnp.arange(C, dtype=jnp.float32)
    diff = idx[:, None] - idx[None, :]
    inner_decay = jnp.where(diff >= 0, jnp.exp(log_g[:, None, None] * jnp.maximum(diff, 0.0)), 0.0)
    q_decay = jnp.exp(log_g[None, :] * (idx[:, None] + 1.0))
    k_decay = jnp.exp(log_g[None, :] * (C - 1.0 - idx[:, None]))
    chunk_decay = jnp.exp(log_g * C)
    qc = q.reshape(B, N, C, H, dk)
    kc = k.reshape(B, N, C, H, dk)
    vc = v.reshape(B, N, C, H, dv)
    scores = jnp.einsum('bnihd,bnjhd->bnhij', qc, kc) * inner_decay.astype(dt)
    inner = jnp.einsum('bnhij,bnjhe->bnihe', scores, vc)
    kv = jnp.einsum('bnjhd,bnjhe->nbhde', kc * k_decay[:, :, None].astype(dt), vc).astype(jnp.float32)

    def step(state, kv_n):
        return state * chunk_decay[:, None, None] + kv_n, state

    _, prev = lax.scan(step, jnp.zeros((B, H, dk, dv), jnp.float32), kv)
    cross = jnp.einsum('bnihd,nbhde->bnihe', qc * q_decay[:, :, None].astype(dt), prev.astype(dt))
    return (inner + cross).reshape(B, S, H, dv)


def dilated_window_attention(q, k, v, window, dilation):
    B, S, H, hd = q.shape
    w_sub = window // dilation
    L = S // dilation
    nb = -(-L // w_sub)
    Lp = nb * w_sub

    def to_sub(t):
        t = t.reshape(B, L, dilation, H, hd).transpose(0, 2, 1, 3, 4)
        t = jnp.pad(t, ((0, 0), (0, 0), (0, Lp - L), (0, 0), (0, 0)))
        return t.reshape(B, dilation, nb, w_sub, H, hd)

    def with_prev(t):
        prev = jnp.pad(t, ((0, 0), (0, 0), (1, 0), (0, 0), (0, 0), (0, 0)))[:, :, :-1]
        return jnp.concatenate([prev, t], axis=3)

    qs = to_sub(q)
    kb = with_prev(to_sub(k))
    vb = with_prev(to_sub(v))
    s = jnp.einsum('brnqhd,brnkhd->brnhqk', qs, kb).astype(jnp.float32) * (hd ** -0.5)
    blk = jnp.arange(nb)[:, None]
    qpos = blk * w_sub + jnp.arange(w_sub)[None, :]
    kpos = (blk - 1) * w_sub + jnp.arange(2 * w_sub)[None, :]
    dist = qpos[:, :, None] - kpos[:, None, :]
    valid = (dist >= 0) & (dist <= w_sub) & (kpos[:, None, :] >= 0)
    s = jnp.where(valid[:, None], s, -jnp.inf)
    m = jnp.max(s, axis=-1, keepdims=True)
    e = jnp.exp(s - m)
    l = jnp.sum(e, axis=-1, keepdims=True)
    o = jnp.einsum('brnhqk,brnkhd->brnqhd', (e / l).astype(v.dtype), vb)
    lse = (m + jnp.log(l))[..., 0].transpose(0, 1, 2, 4, 3)

    def from_sub(t):
        t = t.reshape((B, dilation, Lp) + t.shape[4:])[:, :, :L]
        t = jnp.moveaxis(t, 1, 2)
        return t.reshape((B, S) + t.shape[3:])

    return from_sub(o), from_sub(lse)


def hybrid_layer(x, p_i, w_in, b_gate, g_mix, q_gain, k_gain, ret_gn, w_ret_out, w_att_out, w_o,
                 g_mlp, w_up, w_down, g_ple, w_ple_proj, w_ple_gate):
    B, S, _ = x.shape
    dt = x.dtype
    h = rmsnorm(x, g_mix)
    z = h @ w_in
    rq, rk, rv, rg, aq, ak, av, gl = jnp.split(z, IN_OFFSETS, axis=-1)

    ret_freq = 1.0 / (10000.0 ** jnp.linspace(0.0, 1.0, RET_DK // 2, dtype=jnp.float32))
    rq = rotate(rq.reshape(B, S, RET_HEADS, RET_DK), ret_freq)
    rk = rotate(rk.reshape(B, S, RET_HEADS, RET_DK), ret_freq) * (RET_DK ** -0.5)
    rv = rv.reshape(B, S, RET_HEADS, RET_DV)
    y = retention_chunkwise(rq, rk, rv)
    y = rmsnorm(y, ret_gn.reshape(RET_HEADS, RET_DV)).reshape(B, S, RET_V)
    ret_branch = (jax.nn.silu(rg) * y) @ w_ret_out

    aq = rmsnorm(aq.reshape(B, S, N_GROUPS, ATT_HEADS, ATT_HD), q_gain[:, None, :])
    ak = rmsnorm(ak.reshape(B, S, N_GROUPS, ATT_HEADS, ATT_HD), k_gain[:, None, :])
    av = av.reshape(B, S, N_GROUPS, ATT_HEADS, ATT_HD)
    rope_freq = ROPE_THETA ** (-jnp.arange(0, ATT_HD, 2, dtype=jnp.float32) / ATT_HD)
    aq = rotate(aq.reshape(B, S, N_GROUPS * ATT_HEADS, ATT_HD), rope_freq).reshape(B, S, N_GROUPS, ATT_HEADS, ATT_HD)
    ak = rotate(ak.reshape(B, S, N_GROUPS * ATT_HEADS, ATT_HD), rope_freq).reshape(B, S, N_GROUPS, ATT_HEADS, ATT_HD)
    outs, lses = [], []
    for g, (window, dilation) in enumerate(ATT_PATTERNS):
        o_g, lse_g = dilated_window_attention(aq[:, :, g], ak[:, :, g], av[:, :, g], window, dilation)
        outs.append(o_g)
        lses.append(lse_g)
    wts = jax.nn.softmax(jnp.stack(lses, axis=0), axis=0)
    o = jnp.einsum('gbsh,gbshd->bshd', wts.astype(dt), jnp.stack(outs, axis=0))
    att_branch = o.reshape(B, S, ATT_W) @ w_att_out

    gate_r, gate_a = jnp.split(jax.nn.sigmoid(gl + b_gate), N_BRANCH, axis=-1)
    x = x + (gate_r * ret_branch + gate_a * att_branch) @ w_o

    u = rmsnorm(x, g_mlp) @ w_up
    x = x + jnp.square(jax.nn.relu(u)) @ w_down

    gate_p = jax.nn.sigmoid(rmsnorm(x, g_ple) @ w_ple_gate)
    return x + gate_p * (p_i @ w_ple_proj)


def setup_inputs(seed: int = 0) -> dict:
    key = jax.random.key(seed)
    ks = jax.random.split(key, 20)
    f32 = jnp.float32

    def nrm(k, shape, scale):
        return jax.random.normal(k, shape, f32) * scale

    def gain(k, shape):
        return 1.0 + 0.02 * jax.random.normal(k, shape, f32)

    return {
        "x": nrm(ks[0], (BATCH, SEQ, D_MODEL), 1.0),
        "p": nrm(ks[1], (DEPTH, BATCH, SEQ, PLE_DIM), 1.0),
        "w_in": nrm(ks[2], (DEPTH, D_MODEL, D_IN), D_MODEL ** -0.5),
        "b_gate": nrm(ks[3], (DEPTH, N_BRANCH * D_MODEL), 0.01),
        "g_mix": gain(ks[4], (DEPTH, D_MODEL)),
        "q_gain": gain(ks[5], (DEPTH, N_GROUPS, ATT_HD)),
        "k_gain": gain(ks[6], (DEPTH, N_GROUPS, ATT_HD)),
        "ret_gn": gain(ks[7], (DEPTH, RET_V)),
        "w_ret_out": nrm(ks[8], (DEPTH, RET_V, D_MODEL), RET_V ** -0.5),
        "w_att_out": nrm(ks[9], (DEPTH, ATT_W, D_MODEL), ATT_W ** -0.5),
        "w_o": nrm(ks[10], (DEPTH, D_MODEL, D_MODEL), D_MODEL ** -0.5),
        "g_mlp": gain(ks[11], (DEPTH, D_MODEL)),
        "w_up": nrm(ks[12], (DEPTH, D_MODEL, D_FF), D_MODEL ** -0.5),
        "w_down": nrm(ks[13], (DEPTH, D_FF, D_MODEL), D_FF ** -0.5),
        "g_ple": gain(ks[14], (DEPTH, D_MODEL)),
        "w_ple_proj": nrm(ks[15], (DEPTH, PLE_DIM, D_MODEL), PLE_DIM ** -0.5),
        "w_ple_gate": nrm(ks[16], (DEPTH, D_MODEL, D_MODEL), D_MODEL ** -0.5),
    }


def reference(x, p, w_in, b_gate, g_mix, q_gain, k_gain, ret_gn, w_ret_out, w_att_out, w_o,
              g_mlp, w_up, w_down, g_ple, w_ple_proj, w_ple_gate):
    for i in range(DEPTH):
        x = hybrid_layer(x, p[i], w_in[i], b_gate[i], g_mix[i], q_gain[i], k_gain[i], ret_gn[i],
                         w_ret_out[i], w_att_out[i], w_o[i], g_mlp[i], w_up[i], w_down[i],
                         g_ple[i], w_ple_proj[i], w_ple_gate[i])
    return x
```

```python
import functools

import jax
import jax.numpy as jnp
from jax import lax
from jax.experimental import pallas as pl
from jax.experimental.pallas import tpu as pltpu

F32 = jnp.float32
BF16 = jnp.bfloat16

D_MODEL = 1024
PLE_DIM = 256
D_FF = 4 * D_MODEL
EPS = 1e-6
RET_HEADS = 4
RET_DK = 128
RET_DV = 256
RET_CHUNK = 128
RET_QK = RET_HEADS * RET_DK
RET_V = RET_HEADS * RET_DV
ATT_PATTERNS = ((128, 1), (512, 4), (2048, 16))
N_GROUPS = len(ATT_PATTERNS)
ATT_HEADS = 4
ATT_HD = 128
ATT_W = ATT_HEADS * ATT_HD
ATT_BLK = 128
ROPE_THETA = 10000.0

LANES = 128
SEG = 512
W_RET = 2 * RET_QK + RET_V
W_GATE = RET_V + 2 * D_MODEL
W_ATT = 3 * N_GROUPS * ATT_W
VMEM_LIMIT = 56 * 1024 * 1024


def _resident(shape):
    return pl.BlockSpec(shape, lambda *_: (0,) * len(shape), pipeline_mode=pl.Buffered(1))


def _rms(x):
    return x * lax.rsqrt(jnp.mean(x * x, axis=-1, keepdims=True) + EPS)


def _rotate(a, cos, sin_signed):
    return a * cos + pltpu.roll(a, LANES // 2, axis=1) * sin_signed


def _inproj_kernel(x_ref, g_ref, w_ref, cr_ref, sr_ref, ca_ref, sa_ref, qg_ref, kg_ref,
                   zr_ref, zg_ref, za_ref):
    h = (_rms(x_ref[...]) * g_ref[...]).astype(BF16)
    n_ret, n_rg, n_att = W_RET // SEG, RET_V // SEG, W_ATT // SEG
    for j in range((W_RET + W_GATE + W_ATT) // SEG):
        acc = jnp.dot(h, w_ref[:, j * SEG:(j + 1) * SEG], preferred_element_type=F32)
        if j < 2:
            cos, sin = cr_ref[...], sr_ref[...]
            for hh in range(SEG // LANES):
                r = _rotate(acc[:, hh * LANES:(hh + 1) * LANES], cos, sin)
                if j == 1:
                    r = r * (RET_DK ** -0.5)
                zr_ref[:, j * SEG + hh * LANES:j * SEG + (hh + 1) * LANES] = r.astype(BF16)
        elif j < n_ret:
            zr_ref[:, j * SEG:(j + 1) * SEG] = acc.astype(BF16)
        elif j < n_ret + n_rg:
            jj = j - n_ret
            zg_ref[:, jj * SEG:(jj + 1) * SEG] = acc.astype(BF16)
        elif j >= n_ret + n_rg + n_att:
            jj = j - n_ret - n_att
            zg_ref[:, jj * SEG:(jj + 1) * SEG] = acc.astype(BF16)
        else:
            jj = j - n_ret - n_rg
            if jj < 2 * N_GROUPS:
                g = jj % N_GROUPS
                gain = (qg_ref if jj < N_GROUPS else kg_ref)[g:g + 1, :]
                cos, sin = ca_ref[...], sa_ref[...]
                for hh in range(SEG // LANES):
                    a = _rms(acc[:, hh * LANES:(hh + 1) * LANES]) * gain
                    za_ref[:, jj * SEG + hh * LANES:jj * SEG + (hh + 1) * LANES] = (
                        _rotate(a, cos, sin).astype(BF16))
            else:
                za_ref[:, jj * SEG:(jj + 1) * SEG] = acc.astype(BF16)


def _in_proj(x2, g_mix, w_in, tabs, q_gain, k_gain, seq, tm=512):
    T = x2.shape[0]
    npos = seq // tm
    tab_spec = pl.BlockSpec((tm, LANES), lambda i: (i % npos, 0))
    return pl.pallas_call(
        _inproj_kernel,
        grid=(T // tm,),
        in_specs=[
            pl.BlockSpec((tm, D_MODEL), lambda i: (i, 0)),
            _resident((1, D_MODEL)),
            _resident(w_in.shape),
            tab_spec, tab_spec, tab_spec, tab_spec,
            _resident(q_gain.shape), _resident(k_gain.shape),
        ],
        out_specs=[
            pl.BlockSpec((tm, W_RET), lambda i: (i, 0)),
            pl.BlockSpec((tm, W_GATE), lambda i: (i, 0)),
            pl.BlockSpec((tm, W_ATT), lambda i: (i, 0)),
        ],
        out_shape=[
            jax.ShapeDtypeStruct((T, W_RET), BF16),
            jax.ShapeDtypeStruct((T, W_GATE), BF16),
            jax.ShapeDtypeStruct((T, W_ATT), BF16),
        ],
        compiler_params=pltpu.CompilerParams(
            dimension_semantics=("parallel",), vmem_limit_bytes=VMEM_LIMIT),
        name="in_proj",
    )(x2, g_mix, w_in, *tabs, q_gain, k_gain)


def _retention_kernel(q_ref, k_ref, v_ref, rg_ref, gn_ref, inner_ref, qd_ref, kd_ref, cd_ref,
                      y_ref, state_ref, *, n_chunks):
    @pl.when(pl.program_id(2) == 0)
    def _():
        state_ref[...] = jnp.zeros_like(state_ref)

    C = RET_CHUNK
    inner_decay, q_decay, k_decay, chunk_decay = inner_ref[...], qd_ref[...], kd_ref[...], cd_ref[...]
    gn = gn_ref[...]
    for c in range(n_chunks):
        rows = slice(c * C, (c + 1) * C)
        q, k, v = q_ref[rows, :], k_ref[rows, :], v_ref[rows, :]
        scores = lax.dot_general(q, k, (((1,), (1,)), ((), ())), preferred_element_type=F32) * inner_decay
        inner = jnp.dot(scores.astype(BF16), v, preferred_element_type=F32)
        state = state_ref[...]
        cross = jnp.dot((q.astype(F32) * q_decay).astype(BF16), state.astype(BF16),
                        preferred_element_type=F32)
        kd = (k.astype(F32) * k_decay).astype(BF16)
        kv = lax.dot_general(kd, v, (((0,), (0,)), ((), ())), preferred_element_type=F32)
        state_ref[...] = state * chunk_decay + kv
        y = _rms(inner + cross) * gn
        rg = rg_ref[rows, :].astype(F32)
        y_ref[rows, :] = (rg * jax.nn.sigmoid(rg) * y).astype(BF16)


def _retention(zr, zg, ret_gn, dec, ts=1024):
    B, S, _ = zr.shape
    inner_decay, q_decay, k_decay, chunk_decay = dec
    head_tab = lambda shape: pl.BlockSpec((None,) + shape, lambda b, h, s: (h, 0, 0))
    return pl.pallas_call(
        functools.partial(_retention_kernel, n_chunks=ts // RET_CHUNK),
        grid=(B, RET_HEADS, S // ts),
        in_specs=[
            pl.BlockSpec((None, ts, RET_DK), lambda b, h, s: (b, s, h)),
            pl.BlockSpec((None, ts, RET_DK), lambda b, h, s: (b, s, RET_HEADS + h)),
            pl.BlockSpec((None, ts, RET_DV), lambda b, h, s: (b, s, 2 * RET_QK // RET_DV + h)),
            pl.BlockSpec((None, ts, RET_DV), lambda b, h, s: (b, s, h)),
            pl.BlockSpec((1, RET_DV), lambda b, h, s: (0, h)),
            head_tab((RET_CHUNK, RET_CHUNK)),
            head_tab((RET_CHUNK, RET_DK)),
            head_tab((RET_CHUNK, RET_DK)),
            head_tab((RET_DK, RET_DV)),
        ],
        out_specs=pl.BlockSpec((None, ts, RET_DV), lambda b, h, s: (b, s, h)),
        out_shape=jax.ShapeDtypeStruct((B, S, RET_V), BF16),
        scratch_shapes=[pltpu.VMEM((RET_DK, RET_DV), F32)],
        compiler_params=pltpu.CompilerParams(
            dimension_semantics=("parallel", "parallel", "arbitrary"), vmem_limit_bytes=VMEM_LIMIT),
        name="retention",
    )(zr, zr, zr, zg, ret_gn, inner_decay, q_decay, k_decay, chunk_decay)


def _attn_block(q, k, v, valid):
    s = lax.dot_general(q, k, (((1,), (1,)), ((), ())), preferred_element_type=F32) * (ATT_HD ** -0.5)
    s = jnp.where(valid, s, -jnp.inf)
    m = jnp.max(s, axis=-1, keepdims=True)
    e = jnp.exp(s - m)
    l = jnp.sum(e, axis=-1, keepdims=True)
    o = jnp.dot((e / l).astype(BF16), v, preferred_element_type=F32)
    return o, m + jnp.log(l)


def _attention_kernel(*refs, n_res, n_blk, carry):
    q_refs, k_refs, v_refs = refs[:n_res], refs[n_res:2 * n_res], refs[2 * n_res:3 * n_res]
    o_ref, l_ref = refs[3 * n_res], refs[3 * n_res + 1]
    W = ATT_BLK
    row = lax.broadcasted_iota(jnp.int32, (W, 2 * W), 0)
    col = lax.broadcasted_iota(jnp.int32, (W, 2 * W), 1)
    band = (col >= row) & (col <= row + W)
    causal = (lax.broadcasted_iota(jnp.int32, (W, W), 1)
              <= lax.broadcasted_iota(jnp.int32, (W, W), 0))
    lane_head = lax.broadcasted_iota(jnp.int32, (W, LANES), 1) // (LANES // ATT_HEADS)
    if carry:
        kprev_ref, vprev_ref = refs[3 * n_res + 2:]
        band0 = band & (col + W * jnp.minimum(pl.program_id(1), 1) >= W)

        @pl.when(pl.program_id(1) == 0)
        def _():
            kprev_ref[...] = jnp.zeros_like(kprev_ref)
            vprev_ref[...] = jnp.zeros_like(vprev_ref)

    for r in range(n_res):
        q_ref, k_ref, v_ref = q_refs[r], k_refs[r], v_refs[r]
        for b in range(n_blk):
            rows = slice(b * W, (b + 1) * W)
            lse_tile = jnp.zeros((W, LANES), F32)
            for hh in range(ATT_HEADS):
                cols = slice(hh * ATT_HD, (hh + 1) * ATT_HD)
                q = q_ref[rows, cols]
                if b > 0:
                    keys = slice((b - 1) * W, (b + 1) * W)
                    o, lse = _attn_block(q, k_ref[keys, cols], v_ref[keys, cols], band)
                elif carry:
                    k2 = jnp.concatenate([kprev_ref[:, cols], k_ref[rows, cols]], axis=0)
                    v2 = jnp.concatenate([vprev_ref[:, cols], v_ref[rows, cols]], axis=0)
                    o, lse = _attn_block(q, k2, v2, band0)
                else:
                    o, lse = _attn_block(q, k_ref[rows, cols], v_ref[rows, cols], causal)
                o_ref[rows, r * ATT_W + hh * ATT_HD:r * ATT_W + (hh + 1) * ATT_HD] = o.astype(BF16)
                lse_tile = jnp.where(lane_head == hh, lse, lse_tile)
            l_ref[rows, r * LANES:(r + 1) * LANES] = lse_tile
    if carry:
        last = slice((n_blk - 1) * W, n_blk * W)
        kprev_ref[...] = k_refs[0][last, :]
        vprev_ref[...] = v_refs[0][last, :]


def _attention(za, g, batch, seq, rows_per_step=1024):
    window, dil = ATT_PATTERNS[g]
    assert window // dil == ATT_BLK
    L = seq // dil
    nseg = W_ATT // ATT_W
    zv = za.reshape(batch, L, dil * W_ATT)
    carry = L > rows_per_step
    lc = min(L, rows_per_step)
    n_res = rows_per_step // lc
    grid = (batch, L // lc if carry else dil // n_res)
    blk_of = (lambda j: j) if carry else (lambda j: 0)
    res_of = (lambda j: 0) if carry else (lambda j: j)

    def in_spec(seg, i):
        return pl.BlockSpec(
            (None, lc, ATT_W),
            lambda b, j: (b, blk_of(j), (res_of(j) * n_res + i) * nseg + seg * N_GROUPS + g))

    in_specs = [in_spec(seg, i) for seg in range(3) for i in range(n_res)]
    out_specs = [pl.BlockSpec((None, lc, n_res * ATT_W), lambda b, j: (b, blk_of(j), res_of(j))),
                 pl.BlockSpec((None, lc, n_res * LANES), lambda b, j: (b, blk_of(j), res_of(j)))]
    out_shape = [jax.ShapeDtypeStruct((batch, L, dil * ATT_W), BF16),
                 jax.ShapeDtypeStruct((batch, L, dil * LANES), F32)]
    o, lse = pl.pallas_call(
        functools.partial(_attention_kernel, n_res=n_res, n_blk=lc // ATT_BLK, carry=carry),
        grid=grid, in_specs=in_specs, out_specs=out_specs, out_shape=out_shape,
        scratch_shapes=[pltpu.VMEM((ATT_BLK, ATT_W), BF16)] * 2 if carry else [],
        compiler_params=pltpu.CompilerParams(
            dimension_semantics=("parallel", "arbitrary"), vmem_limit_bytes=VMEM_LIMIT),
        name=f"attention_g{g}",
    )(*([zv] * (3 * n_res)))
    return o.reshape(batch * seq, ATT_W), lse.reshape(batch * seq, LANES)


def _merge_proj_kernel(x_ref, y_ref, o0_ref, o1_ref, o2_ref, l0_ref, l1_ref, l2_ref, gr_ref, ga_ref,
                       bg_ref, wr_ref, wa_ref, wo_ref, out_ref):
    lses = [l0_ref[...], l1_ref[...], l2_ref[...]]
    m = jnp.maximum(jnp.maximum(lses[0], lses[1]), lses[2])
    es = [jnp.exp(l - m) for l in lses]
    den = es[0] + es[1] + es[2]
    wts = [e / den for e in es]
    per_head = LANES // ATT_HEADS
    parts = []
    for hh in range(ATT_HEADS):
        cols = slice(hh * ATT_HD, (hh + 1) * ATT_HD)
        acc = None
        for w, o_ref in zip(wts, (o0_ref, o1_ref, o2_ref)):
            t = w[:, hh * per_head:hh * per_head + 1] * o_ref[:, cols].astype(F32)
            acc = t if acc is None else acc + t
        parts.append(acc)
    o = jnp.concatenate(parts, axis=1).astype(BF16)
    ret_branch = jnp.dot(y_ref[...], wr_ref[...], preferred_element_type=F32)
    att_branch = jnp.dot(o, wa_ref[...], preferred_element_type=F32)
    bg = bg_ref[...]
    gate_r = jax.nn.sigmoid(gr_ref[...].astype(F32) + bg[:, :D_MODEL])
    gate_a = jax.nn.sigmoid(ga_ref[...].astype(F32) + bg[:, D_MODEL:])
    mix = (gate_r * ret_branch + gate_a * att_branch).astype(BF16)
    out_ref[...] = x_ref[...] + jnp.dot(mix, wo_ref[...], preferred_element_type=F32)


def _merge_proj(x2, yg, os_, ls_, zg, b_gate, w_ret_out, w_att_out, w_o, tm=512):
    T = x2.shape[0]
    row = lambda w: pl.BlockSpec((tm, w), lambda i: (i, 0))
    return pl.pallas_call(
        _merge_proj_kernel,
        grid=(T // tm,),
        in_specs=[
            row(D_MODEL), row(RET_V), row(ATT_W), row(ATT_W), row(ATT_W),
            row(LANES), row(LANES), row(LANES),
            pl.BlockSpec((tm, D_MODEL), lambda i: (i, RET_V // D_MODEL)),
            pl.BlockSpec((tm, D_MODEL), lambda i: (i, RET_V // D_MODEL + 1)),
            _resident(b_gate.shape), _resident(w_ret_out.shape), _resident(w_att_out.shape),
            _resident(w_o.shape),
        ],
        out_specs=row(D_MODEL),
        out_shape=jax.ShapeDtypeStruct((T, D_MODEL), F32),
        compiler_params=pltpu.CompilerParams(
            dimension_semantics=("parallel",), vmem_limit_bytes=VMEM_LIMIT),
        name="merge_proj",
    )(x2, yg, *os_, *ls_, zg, zg, b_gate, w_ret_out, w_att_out, w_o)


def _mlp_ple_kernel(x_ref, p_ref, gm_ref, wu_ref, wd_ref, gp_ref, wpg_ref, wpp_ref, out_ref, *, ff_chunk):
    x = x_ref[...]
    hn = (_rms(x) * gm_ref[...]).astype(BF16)
    acc = x
    for c in range(D_FF // ff_chunk):
        cols = slice(c * ff_chunk, (c + 1) * ff_chunk)
        u = jnp.dot(hn, wu_ref[:, cols], preferred_element_type=F32)
        a = jnp.square(jnp.maximum(u, 0.0)).astype(BF16)
        acc = acc + jnp.dot(a, wd_ref[cols, :], preferred_element_type=F32)
    hp = (_rms(acc) * gp_ref[...]).astype(BF16)
    gate_p = jax.nn.sigmoid(jnp.dot(hp, wpg_ref[...], preferred_element_type=F32))
    proj = jnp.dot(p_ref[...].astype(BF16), wpp_ref[...], preferred_element_type=F32)
    out_ref[...] = acc + gate_p * proj


def _mlp_ple(x1, p2, g_mlp, w_up, w_down, g_ple, w_pg, w_pp, tm=512, ff_chunk=1024):
    T = x1.shape[0]
    return pl.pallas_call(
        functools.partial(_mlp_ple_kernel, ff_chunk=ff_chunk),
        grid=(T // tm,),
        in_specs=[
            pl.BlockSpec((tm, D_MODEL), lambda i: (i, 0)),
            pl.BlockSpec((tm, PLE_DIM), lambda i: (i, 0)),
            _resident((1, D_MODEL)), _resident(w_up.shape), _resident(w_down.shape),
            _resident((1, D_MODEL)), _resident(w_pg.shape), _resident(w_pp.shape),
        ],
        out_specs=pl.BlockSpec((tm, D_MODEL), lambda i: (i, 0)),
        out_shape=jax.ShapeDtypeStruct((T, D_MODEL), F32),
        compiler_params=pltpu.CompilerParams(
            dimension_semantics=("parallel",), vmem_limit_bytes=VMEM_LIMIT),
        name="mlp_ple",
    )(x1, p2, g_mlp, w_up, w_down, g_ple, w_pg, w_pp)


def _rot_tables(inv_freq, seq):
    ang = jnp.arange(seq, dtype=F32)[:, None] * inv_freq[None, :]
    cos, sin = jnp.cos(ang), jnp.sin(ang)
    return jnp.concatenate([cos, cos], axis=1), jnp.concatenate([-sin, sin], axis=1)


def _decay_tables():
    H, C = RET_HEADS, RET_CHUNK
    log_g = jnp.log1p(-jnp.exp2(-5.0 - jnp.arange(H, dtype=F32)))
    idx = jnp.arange(C, dtype=F32)
    diff = idx[:, None] - idx[None, :]
    inner = jnp.where(diff >= 0, jnp.exp(log_g[:, None, None] * jnp.maximum(diff, 0.0)), 0.0)
    q_decay = jnp.exp(log_g[:, None] * (idx[None, :] + 1.0))
    k_decay = jnp.exp(log_g[:, None] * (C - 1.0 - idx[None, :]))
    chunk_decay = jnp.exp(log_g * C)
    return (inner,
            jnp.broadcast_to(q_decay[:, :, None], (H, C, RET_DK)),
            jnp.broadcast_to(k_decay[:, :, None], (H, C, RET_DK)),
            jnp.broadcast_to(chunk_decay[:, None, None], (H, RET_DK, RET_DV)))


def _layer(x, p_i, w_in, b_gate, g_mix, q_gain, k_gain, ret_gn, w_ret_out, w_att_out, w_o,
           g_mlp, w_up, w_down, g_ple, w_ple_proj, w_ple_gate):
    B, S, D = x.shape
    T = B * S
    x2 = x.reshape(T, D)
    ret_freq = 1.0 / (10000.0 ** jnp.linspace(0.0, 1.0, RET_DK // 2, dtype=F32))
    rope_freq = ROPE_THETA ** (-jnp.arange(0, ATT_HD, 2, dtype=F32) / ATT_HD)
    tabs = _rot_tables(ret_freq, S) + _rot_tables(rope_freq, S)

    zr, zg, za = _in_proj(x2, g_mix.reshape(1, D), w_in.astype(BF16), tabs, q_gain, k_gain, S)
    yg = _retention(zr.reshape(B, S, W_RET), zg.reshape(B, S, W_GATE), ret_gn.reshape(1, RET_V),
                    _decay_tables()).reshape(T, RET_V)
    att = [_attention(za, g, B, S) for g in range(N_GROUPS)]
    x1 = _merge_proj(x2, yg, [o for o, _ in att], [l for _, l in att], zg, b_gate.reshape(1, -1),
                     w_ret_out.astype(BF16), w_att_out.astype(BF16), w_o.astype(BF16))
    out = _mlp_ple(x1, p_i.reshape(T, PLE_DIM), g_mlp.reshape(1, D), w_up.astype(BF16),
                   w_down.astype(BF16), g_ple.reshape(1, D), w_ple_gate.astype(BF16),
                   w_ple_proj.astype(BF16))
    return out.reshape(B, S, D)


def kernel(x, p, w_in, b_gate, g_mix, q_gain, k_gain, ret_gn, w_ret_out, w_att_out, w_o, g_mlp, w_up,
           w_down, g_ple, w_ple_proj, w_ple_gate):
    for i in range(p.shape[0]):
        x = _layer(x, p[i], w_in[i], b_gate[i], g_mix[i], q_gain[i], k_gain[i], ret_gn[i],
                   w_ret_out[i], w_att_out[i], w_o[i], g_mlp[i], w_up[i], w_down[i], g_ple[i],
                   w_ple_proj[i], w_ple_gate[i])
    return x
```

```python
import functools

import jax
import jax.numpy as jnp
from jax import lax
from jax.experimental import pallas as pl
from jax.experimental.pallas import tpu as pltpu

F32 = jnp.float32
BF16 = jnp.bfloat16

D_MODEL = 1024
PLE_DIM = 256
D_FF = 4 * D_MODEL
EPS = 1e-6
RET_HEADS = 4
RET_DK = 128
RET_DV = 256
RET_CHUNK = 128
RET_QK = RET_HEADS * RET_DK
RET_V = RET_HEADS * RET_DV
ATT_PATTERNS = ((128, 1), (512, 4), (2048, 16))
N_GROUPS = len(ATT_PATTERNS)
ATT_HEADS = 4
ATT_HD = 128
ATT_W = ATT_HEADS * ATT_HD
ATT_BLK = 128
ROPE_THETA = 10000.0

LANES = 128
SEG = 512
W_RET = 2 * RET_QK + RET_V
W_GATE = RET_V + 2 * D_MODEL
W_ATT = 3 * N_GROUPS * ATT_W
VMEM_LIMIT = 56 * 1024 * 1024


def _resident(shape):
    return pl.BlockSpec(shape, lambda *_: (0,) * len(shape), pipeline_mode=pl.Buffered(1))


def _rms(x):
    return x * lax.rsqrt(jnp.mean(x * x, axis=-1, keepdims=True) + EPS)


def _rotate(a, cos, sin_signed):
    return a * cos + pltpu.roll(a, LANES // 2, axis=1) * sin_signed


def _inproj_kernel(x_ref, g_ref, w_ref, cr_ref, sr_ref, ca_ref, sa_ref, qg_ref, kg_ref,
                   zr_ref, zg_ref, za0_ref, za1_ref, za2_ref, perm_ref):
    za_refs = (za0_ref, za1_ref, za2_ref)
    tm = x_ref.shape[0]

    def store_att(g, seg, val):
        cols = slice(seg * ATT_W, (seg + 1) * ATT_W)
        dil = ATT_PATTERNS[g][1]
        if dil == 1:
            za_refs[g][0, :, cols] = val.astype(BF16)
            return
        for hh in range(ATT_HEADS):
            perm_ref[hh] = val[:, hh * ATT_HD:(hh + 1) * ATT_HD]
        for r in range(dil):
            for hh in range(ATT_HEADS):
                c0 = seg * ATT_W + hh * ATT_HD
                za_refs[g][r, :, c0:c0 + ATT_HD] = (
                    perm_ref[hh, pl.ds(r, tm // dil, stride=dil), :].astype(BF16))

    h = (_rms(x_ref[...]) * g_ref[...]).astype(BF16)
    n_ret, n_rg, n_att = W_RET // SEG, RET_V // SEG, W_ATT // SEG
    for j in range((W_RET + W_GATE + W_ATT) // SEG):
        acc = jnp.dot(h, w_ref[:, j * SEG:(j + 1) * SEG], preferred_element_type=F32)
        if j < 2:
            cos, sin = cr_ref[...], sr_ref[...]
            for hh in range(SEG // LANES):
                r = _rotate(acc[:, hh * LANES:(hh + 1) * LANES], cos, sin)
                if j == 1:
                    r = r * (RET_DK ** -0.5)
                zr_ref[:, j * SEG + hh * LANES:j * SEG + (hh + 1) * LANES] = r.astype(BF16)
        elif j < n_ret:
            zr_ref[:, j * SEG:(j + 1) * SEG] = acc.astype(BF16)
        elif j < n_ret + n_rg:
            jj = j - n_ret
            zg_ref[:, jj * SEG:(jj + 1) * SEG] = acc.astype(BF16)
        elif j >= n_ret + n_rg + n_att:
            jj = j - n_ret - n_att
            zg_ref[:, jj * SEG:(jj + 1) * SEG] = acc.astype(BF16)
        else:
            jj = j - n_ret - n_rg
            seg, g = jj // N_GROUPS, jj % N_GROUPS
            if seg < 2:
                gain = (qg_ref if seg == 0 else kg_ref)[g:g + 1, :]
                cos, sin = ca_ref[...], sa_ref[...]
                parts = []
                for hh in range(SEG // LANES):
                    a = _rms(acc[:, hh * LANES:(hh + 1) * LANES]) * gain
                    parts.append(_rotate(a, cos, sin))
                store_att(g, seg, jnp.concatenate(parts, axis=1))
            else:
                store_att(g, seg, acc)


def _in_proj(x2, g_mix, w_in, tabs, q_gain, k_gain, seq, tm=512):
    T = x2.shape[0]
    npos = seq // tm
    tab_spec = pl.BlockSpec((tm, LANES), lambda i: (i % npos, 0))
    return pl.pallas_call(
        _inproj_kernel,
        grid=(T // tm,),
        in_specs=[
            pl.BlockSpec((tm, D_MODEL), lambda i: (i, 0)),
            _resident((1, D_MODEL)),
            _resident(w_in.shape),
            tab_spec, tab_spec, tab_spec, tab_spec,
            _resident(q_gain.shape), _resident(k_gain.shape),
        ],
        out_specs=[
            pl.BlockSpec((tm, W_RET), lambda i: (i, 0)),
            pl.BlockSpec((tm, W_GATE), lambda i: (i, 0)),
        ] + [
            pl.BlockSpec((None, dil, tm // dil, 3 * ATT_W), lambda i: (i // npos, 0, i % npos, 0))
            for _, dil in ATT_PATTERNS
        ],
        out_shape=[
            jax.ShapeDtypeStruct((T, W_RET), BF16),
            jax.ShapeDtypeStruct((T, W_GATE), BF16),
        ] + [
            jax.ShapeDtypeStruct((T // seq, dil, seq // dil, 3 * ATT_W), BF16)
            for _, dil in ATT_PATTERNS
        ],
        scratch_shapes=[pltpu.VMEM((ATT_HEADS, tm, ATT_HD), F32)],
        compiler_params=pltpu.CompilerParams(
            dimension_semantics=("parallel",), vmem_limit_bytes=VMEM_LIMIT),
        name="in_proj",
    )(x2, g_mix, w_in, *tabs, q_gain, k_gain)


def _retention_kernel(q_ref, k_ref, v_ref, rg_ref, gn_ref, inner_ref, qd_ref, kd_ref, cd_ref,
                      y_ref, state_ref, *, n_chunks):
    @pl.when(pl.program_id(2) == 0)
    def _():
        state_ref[...] = jnp.zeros_like(state_ref)

    C = RET_CHUNK
    inner_decay, q_decay, k_decay, chunk_decay = inner_ref[...], qd_ref[...], kd_ref[...], cd_ref[...]
    gn = gn_ref[...]
    for c in range(n_chunks):
        rows = slice(c * C, (c + 1) * C)
        q, k, v = q_ref[rows, :], k_ref[rows, :], v_ref[rows, :]
        scores = lax.dot_general(q, k, (((1,), (1,)), ((), ())), preferred_element_type=F32) * inner_decay
        inner = jnp.dot(scores.astype(BF16), v, preferred_element_type=F32)
        state = state_ref[...]
        cross = jnp.dot((q.astype(F32) * q_decay).astype(BF16), state.astype(BF16),
                        preferred_element_type=F32)
        kd = (k.astype(F32) * k_decay).astype(BF16)
        kv = lax.dot_general(kd, v, (((0,), (0,)), ((), ())), preferred_element_type=F32)
        state_ref[...] = state * chunk_decay + kv
        y = _rms(inner + cross) * gn
        rg = rg_ref[rows, :].astype(F32)
        y_ref[rows, :] = (rg * jax.nn.sigmoid(rg) * y).astype(BF16)


def _retention(zr, zg, ret_gn, dec, ts=1024):
    B, S, _ = zr.shape
    inner_decay, q_decay, k_decay, chunk_decay = dec
    head_tab = lambda shape: pl.BlockSpec((None,) + shape, lambda b, h, s: (h, 0, 0))
    return pl.pallas_call(
        functools.partial(_retention_kernel, n_chunks=ts // RET_CHUNK),
        grid=(B, RET_HEADS, S // ts),
        in_specs=[
            pl.BlockSpec((None, ts, RET_DK), lambda b, h, s: (b, s, h)),
            pl.BlockSpec((None, ts, RET_DK), lambda b, h, s: (b, s, RET_HEADS + h)),
            pl.BlockSpec((None, ts, RET_DV), lambda b, h, s: (b, s, 2 * RET_QK // RET_DV + h)),
            pl.BlockSpec((None, ts, RET_DV), lambda b, h, s: (b, s, h)),
            pl.BlockSpec((1, RET_DV), lambda b, h, s: (0, h)),
            head_tab((RET_CHUNK, RET_CHUNK)),
            head_tab((RET_CHUNK, RET_DK)),
            head_tab((RET_CHUNK, RET_DK)),
            head_tab((RET_DK, RET_DV)),
        ],
        out_specs=pl.BlockSpec((None, ts, RET_DV), lambda b, h, s: (b, s, h)),
        out_shape=jax.ShapeDtypeStruct((B, S, RET_V), BF16),
        scratch_shapes=[pltpu.VMEM((RET_DK, RET_DV), F32)],
        compiler_params=pltpu.CompilerParams(
            dimension_semantics=("parallel", "parallel", "arbitrary"), vmem_limit_bytes=VMEM_LIMIT),
        name="retention",
    )(zr, zr, zr, zg, ret_gn, inner_decay, q_decay, k_decay, chunk_decay)


def _attn_block(q, k, v, valid):
    s = lax.dot_general(q, k, (((1,), (1,)), ((), ())), preferred_element_type=F32) * (ATT_HD ** -0.5)
    s = jnp.where(valid, s, -jnp.inf)
    m = jnp.max(s, axis=-1, keepdims=True)
    e = jnp.exp(s - m)
    l = jnp.sum(e, axis=-1, keepdims=True)
    o = jnp.dot((e / l).astype(BF16), v, preferred_element_type=F32)
    return o, m + jnp.log(l)


def _attention_kernel(q_ref, k_ref, v_ref, o_ref, l_ref, *prev_refs, n_res, n_blk, carry):
    W = ATT_BLK
    row = lax.broadcasted_iota(jnp.int32, (W, 2 * W), 0)
    col = lax.broadcasted_iota(jnp.int32, (W, 2 * W), 1)
    band = (col >= row) & (col <= row + W)
    causal = (lax.broadcasted_iota(jnp.int32, (W, W), 1)
              <= lax.broadcasted_iota(jnp.int32, (W, W), 0))
    lane_head = lax.broadcasted_iota(jnp.int32, (W, LANES), 1) // (LANES // ATT_HEADS)
    if carry:
        kprev_ref, vprev_ref = prev_refs
        band0 = band & (col + W * jnp.minimum(pl.program_id(1), 1) >= W)

        @pl.when(pl.program_id(1) == 0)
        def _():
            kprev_ref[...] = jnp.zeros_like(kprev_ref)
            vprev_ref[...] = jnp.zeros_like(vprev_ref)

    for r in range(n_res):
        for b in range(n_blk):
            rows = slice(b * W, (b + 1) * W)
            lse_tile = jnp.zeros((W, LANES), F32)
            for hh in range(ATT_HEADS):
                cols = slice(hh * ATT_HD, (hh + 1) * ATT_HD)
                q = q_ref[r, rows, cols]
                if b > 0:
                    keys = slice((b - 1) * W, (b + 1) * W)
                    o, lse = _attn_block(q, k_ref[r, keys, cols], v_ref[r, keys, cols], band)
                elif carry:
                    k2 = jnp.concatenate([kprev_ref[:, cols], k_ref[r, rows, cols]], axis=0)
                    v2 = jnp.concatenate([vprev_ref[:, cols], v_ref[r, rows, cols]], axis=0)
                    o, lse = _attn_block(q, k2, v2, band0)
                else:
                    o, lse = _attn_block(q, k_ref[r, rows, cols], v_ref[r, rows, cols], causal)
                o_ref[r, rows, cols] = o.astype(BF16)
                lse_tile = jnp.where(lane_head == hh, lse, lse_tile)
            l_ref[r, rows, :] = lse_tile
    if carry:
        last = slice((n_blk - 1) * W, n_blk * W)
        kprev_ref[...] = k_ref[0, last, :]
        vprev_ref[...] = v_ref[0, last, :]


def _attention(za, g, rows_per_step=1024):
    window, dil = ATT_PATTERNS[g]
    assert window // dil == ATT_BLK
    batch, _, L, _ = za.shape
    carry = L > rows_per_step
    lc = min(L, rows_per_step)
    n_res = rows_per_step // lc
    grid = (batch, L // lc if carry else dil // n_res)
    idx = (lambda b, j, seg: (b, 0, j, seg)) if carry else (lambda b, j, seg: (b, j, 0, seg))
    in_specs = [pl.BlockSpec((None, n_res, lc, ATT_W), functools.partial(idx, seg=seg))
                for seg in range(3)]
    out_specs = [pl.BlockSpec((None, n_res, lc, ATT_W), functools.partial(idx, seg=0)),
                 pl.BlockSpec((None, n_res, lc, LANES), functools.partial(idx, seg=0))]
    out_shape = [jax.ShapeDtypeStruct((batch, dil, L, ATT_W), BF16),
                 jax.ShapeDtypeStruct((batch, dil, L, LANES), F32)]
    return pl.pallas_call(
        functools.partial(_attention_kernel, n_res=n_res, n_blk=lc // ATT_BLK, carry=carry),
        grid=grid, in_specs=in_specs, out_specs=out_specs, out_shape=out_shape,
        scratch_shapes=[pltpu.VMEM((ATT_BLK, ATT_W), BF16)] * 2 if carry else [],
        compiler_params=pltpu.CompilerParams(
            dimension_semantics=("parallel", "arbitrary"), vmem_limit_bytes=VMEM_LIMIT),
        name=f"attention_g{g}",
    )(za, za, za)


def _merge_proj_kernel(x_ref, y_ref, o0_ref, o1_ref, o2_ref, l0_ref, l1_ref, l2_ref, gr_ref, ga_ref,
                       bg_ref, wr_ref, wa_ref, wo_ref, out_ref, *perm_refs):
    tm = x_ref.shape[0]

    def token_order(ref, scratch):
        dil = ref.shape[0]
        if dil == 1:
            return ref[0].astype(F32)
        n_slab = ref.shape[2] // LANES
        for r in range(dil):
            for c in range(n_slab):
                scratch[c, pl.ds(r, tm // dil, stride=dil), :] = (
                    ref[r, :, c * LANES:(c + 1) * LANES].astype(F32))
        return jnp.concatenate([scratch[c] for c in range(n_slab)], axis=1)

    o_tok = [token_order(o0_ref, None), token_order(o1_ref, perm_refs[0]),
             token_order(o2_ref, perm_refs[1])]
    lses = [token_order(l0_ref, None), token_order(l1_ref, perm_refs[2]),
            token_order(l2_ref, perm_refs[3])]
    m = jnp.maximum(jnp.maximum(lses[0], lses[1]), lses[2])
    es = [jnp.exp(l - m) for l in lses]
    den = es[0] + es[1] + es[2]
    wts = [e / den for e in es]
    per_head = LANES // ATT_HEADS
    parts = []
    for hh in range(ATT_HEADS):
        cols = slice(hh * ATT_HD, (hh + 1) * ATT_HD)
        acc = None
        for w, o_g in zip(wts, o_tok):
            t = w[:, hh * per_head:hh * per_head + 1] * o_g[:, cols]
            acc = t if acc is None else acc + t
        parts.append(acc)
    o = jnp.concatenate(parts, axis=1).astype(BF16)
    ret_branch = jnp.dot(y_ref[...], wr_ref[...], preferred_element_type=F32)
    att_branch = jnp.dot(o, wa_ref[...], preferred_element_type=F32)
    bg = bg_ref[...]
    gate_r = jax.nn.sigmoid(gr_ref[...].astype(F32) + bg[:, :D_MODEL])
    gate_a = jax.nn.sigmoid(ga_ref[...].astype(F32) + bg[:, D_MODEL:])
    mix = (gate_r * ret_branch + gate_a * att_branch).astype(BF16)
    out_ref[...] = x_ref[...] + jnp.dot(mix, wo_ref[...], preferred_element_type=F32)


def _merge_proj(x2, yg, os_, ls_, zg, b_gate, w_ret_out, w_att_out, w_o, tm=512):
    T = x2.shape[0]
    npos = os_[0].shape[1] * os_[0].shape[2] // tm
    row = lambda w: pl.BlockSpec((tm, w), lambda i: (i, 0))
    res_major = lambda a: pl.BlockSpec((None, a.shape[1], tm // a.shape[1], a.shape[3]),
                                       lambda i: (i // npos, 0, i % npos, 0))
    dilated = [a for a in os_ + ls_ if a.shape[1] > 1]
    return pl.pallas_call(
        _merge_proj_kernel,
        grid=(T // tm,),
        in_specs=[
            row(D_MODEL), row(RET_V), *[res_major(a) for a in os_ + ls_],
            pl.BlockSpec((tm, D_MODEL), lambda i: (i, RET_V // D_MODEL)),
            pl.BlockSpec((tm, D_MODEL), lambda i: (i, RET_V // D_MODEL + 1)),
            _resident(b_gate.shape), _resident(w_ret_out.shape), _resident(w_att_out.shape),
            _resident(w_o.shape),
        ],
        out_specs=row(D_MODEL),
        out_shape=jax.ShapeDtypeStruct((T, D_MODEL), F32),
        scratch_shapes=[pltpu.VMEM((a.shape[3] // LANES, tm, LANES), F32) for a in dilated],
        compiler_params=pltpu.CompilerParams(
            dimension_semantics=("parallel",), vmem_limit_bytes=VMEM_LIMIT),
        name="merge_proj",
    )(x2, yg, *os_, *ls_, zg, zg, b_gate, w_ret_out, w_att_out, w_o)


def _mlp_ple_kernel(x_ref, p_ref, gm_ref, wu_ref, wd_ref, gp_ref, wpg_ref, wpp_ref, out_ref, *, ff_chunk):
    x = x_ref[...]
    hn = (_rms(x) * gm_ref[...]).astype(BF16)
    acc = x
    for c in range(D_FF // ff_chunk):
        cols = slice(c * ff_chunk, (c + 1) * ff_chunk)
        u = jnp.dot(hn, wu_ref[:, cols], preferred_element_type=F32)
        a = jnp.square(jnp.maximum(u, 0.0)).astype(BF16)
        acc = acc + jnp.dot(a, wd_ref[cols, :], preferred_element_type=F32)
    hp = (_rms(acc) * gp_ref[...]).astype(BF16)
    gate_p = jax.nn.sigmoid(jnp.dot(hp, wpg_ref[...], preferred_element_type=F32))
    proj = jnp.dot(p_ref[...].astype(BF16), wpp_ref[...], preferred_element_type=F32)
    out_ref[...] = acc + gate_p * proj


def _mlp_ple(x1, p2, g_mlp, w_up, w_down, g_ple, w_pg, w_pp, tm=512, ff_chunk=1024):
    T = x1.shape[0]
    return pl.pallas_call(
        functools.partial(_mlp_ple_kernel, ff_chunk=ff_chunk),
        grid=(T // tm,),
        in_specs=[
            pl.BlockSpec((tm, D_MODEL), lambda i: (i, 0)),
            pl.BlockSpec((tm, PLE_DIM), lambda i: (i, 0)),
            _resident((1, D_MODEL)), _resident(w_up.shape), _resident(w_down.shape),
            _resident((1, D_MODEL)), _resident(w_pg.shape), _resident(w_pp.shape),
        ],
        out_specs=pl.BlockSpec((tm, D_MODEL), lambda i: (i, 0)),
        out_shape=jax.ShapeDtypeStruct((T, D_MODEL), F32),
        compiler_params=pltpu.CompilerParams(
            dimension_semantics=("parallel",), vmem_limit_bytes=VMEM_LIMIT),
        name="mlp_ple",
    )(x1, p2, g_mlp, w_up, w_down, g_ple, w_pg, w_pp)


def _rot_tables(inv_freq, seq):
    ang = jnp.arange(seq, dtype=F32)[:, None] * inv_freq[None, :]
    cos, sin = jnp.cos(ang), jnp.sin(ang)
    return jnp.concatenate([cos, cos], axis=1), jnp.concatenate([-sin, sin], axis=1)


def _decay_tables():
    H, C = RET_HEADS, RET_CHUNK
    log_g = jnp.log1p(-jnp.exp2(-5.0 - jnp.arange(H, dtype=F32)))
    idx = jnp.arange(C, dtype=F32)
    diff = idx[:, None] - idx[None, :]
    inner = jnp.where(diff >= 0, jnp.exp(log_g[:, None, None] * jnp.maximum(diff, 0.0)), 0.0)
    q_decay = jnp.exp(log_g[:, None] * (idx[None, :] + 1.0))
    k_decay = jnp.exp(log_g[:, None] * (C - 1.0 - idx[None, :]))
    chunk_decay = jnp.exp(log_g * C)
    return (inner,
            jnp.broadcast_to(q_decay[:, :, None], (H, C, RET_DK)),
            jnp.broadcast_to(k_decay[:, :, None], (H, C, RET_DK)),
            jnp.broadcast_to(chunk_decay[:, None, None], (H, RET_DK, RET_DV)))


def _layer(x, p_i, w_in, b_gate, g_mix, q_gain, k_gain, ret_gn, w_ret_out, w_att_out, w_o,
           g_mlp, w_up, w_down, g_ple, w_ple_proj, w_ple_gate):
    B, S, D = x.shape
    T = B * S
    x2 = x.reshape(T, D)
    ret_freq = 1.0 / (10000.0 ** jnp.linspace(0.0, 1.0, RET_DK // 2, dtype=F32))
    rope_freq = ROPE_THETA ** (-jnp.arange(0, ATT_HD, 2, dtype=F32) / ATT_HD)
    tabs = _rot_tables(ret_freq, S) + _rot_tables(rope_freq, S)

    zr, zg, *zas = _in_proj(x2, g_mix.reshape(1, D), w_in.astype(BF16), tabs, q_gain, k_gain, S)
    yg = _retention(zr.reshape(B, S, W_RET), zg.reshape(B, S, W_GATE), ret_gn.reshape(1, RET_V),
                    _decay_tables()).reshape(T, RET_V)
    att = [_attention(zas[g], g) for g in range(N_GROUPS)]
    x1 = _merge_proj(x2, yg, [o for o, _ in att], [l for _, l in att], zg, b_gate.reshape(1, -1),
                     w_ret_out.astype(BF16), w_att_out.astype(BF16), w_o.astype(BF16))
    out = _mlp_ple(x1, p_i.reshape(T, PLE_DIM), g_mlp.reshape(1, D), w_up.astype(BF16),
                   w_down.astype(BF16), g_ple.reshape(1, D), w_ple_gate.astype(BF16),
                   w_ple_proj.astype(BF16))
    return out.reshape(B, S, D)


def kernel(x, p, w_in, b_gate, g_mix, q_gain, k_gain, ret_gn, w_ret_out, w_att_out, w_o, g_mlp, w_up,
           w_down, g_ple, w_ple_proj, w_ple_gate):
    for i in range(p.shape[0]):
        x = _layer(x, p[i], w_in[i], b_gate[i], g_mix[i], q_gain[i], k_gain[i], ret_gn[i],
                   w_ret_out[i], w_att_out[i], w_o[i], g_mlp[i], w_up[i], w_down[i], g_ple[i],
                   w_ple_proj[i], w_ple_gate[i])
    return x
```

```python
import functools

import jax
import jax.numpy as jnp
from jax import lax
from jax.experimental import pallas as pl
from jax.experimental.pallas import tpu as pltpu

F32 = jnp.float32
BF16 = jnp.bfloat16

D_MODEL = 1024
PLE_DIM = 256
D_FF = 4 * D_MODEL
EPS = 1e-6
RET_HEADS = 4
RET_DK = 128
RET_DV = 256
RET_CHUNK = 128
RET_QK = RET_HEADS * RET_DK
RET_V = RET_HEADS * RET_DV
ATT_PATTERNS = ((128, 1), (512, 4), (2048, 16))
N_GROUPS = len(ATT_PATTERNS)
ATT_HEADS = 4
ATT_HD = 128
ATT_W = ATT_HEADS * ATT_HD
ATT_BLK = 128
ROPE_THETA = 10000.0
LOG2E = 1.4426950408889634
ATT_QSCALE = ATT_HD ** -0.5 * LOG2E

LANES = 128
SEG = 512
W_RET = 2 * RET_QK + RET_V
W_GATE = RET_V + 2 * D_MODEL
W_ATT = 3 * N_GROUPS * ATT_W
VMEM_LIMIT = 56 * 1024 * 1024


def _resident(shape):
    return pl.BlockSpec(shape, lambda *_: (0,) * len(shape), pipeline_mode=pl.Buffered(1))


def _rms(x):
    return x * lax.rsqrt(jnp.mean(x * x, axis=-1, keepdims=True) + EPS)


def _rotate(a, cos, sin_signed):
    return a * cos + pltpu.roll(a, LANES // 2, axis=1) * sin_signed


def _inproj_kernel(x_ref, g_ref, w_ref, cr_ref, sr_ref, ca_ref, sa_ref, qg_ref, kg_ref,
                   zr_ref, zg_ref, za0_ref, za1_ref, za2_ref, perm_ref):
    za_refs = (za0_ref, za1_ref, za2_ref)
    tm = x_ref.shape[0]

    def store_att(g, seg, val):
        cols = slice(seg * ATT_W, (seg + 1) * ATT_W)
        dil = ATT_PATTERNS[g][1]
        if dil == 1:
            za_refs[g][0, :, cols] = val.astype(BF16)
            return
        for hh in range(ATT_HEADS):
            perm_ref[hh] = val[:, hh * ATT_HD:(hh + 1) * ATT_HD]
        for r in range(dil):
            for hh in range(ATT_HEADS):
                c0 = seg * ATT_W + hh * ATT_HD
                za_refs[g][r, :, c0:c0 + ATT_HD] = (
                    perm_ref[hh, pl.ds(r, tm // dil, stride=dil), :].astype(BF16))

    h = (_rms(x_ref[...]) * g_ref[...]).astype(BF16)
    n_ret, n_rg, n_att = W_RET // SEG, RET_V // SEG, W_ATT // SEG
    for j in range((W_RET + W_GATE + W_ATT) // SEG):
        acc = jnp.dot(h, w_ref[:, j * SEG:(j + 1) * SEG], preferred_element_type=F32)
        if j < 2:
            cos, sin = cr_ref[...], sr_ref[...]
            for hh in range(SEG // LANES):
                r = _rotate(acc[:, hh * LANES:(hh + 1) * LANES], cos, sin)
                if j == 1:
                    r = r * (RET_DK ** -0.5)
                zr_ref[:, j * SEG + hh * LANES:j * SEG + (hh + 1) * LANES] = r.astype(BF16)
        elif j < n_ret:
            zr_ref[:, j * SEG:(j + 1) * SEG] = acc.astype(BF16)
        elif j < n_ret + n_rg:
            jj = j - n_ret
            zg_ref[:, jj * SEG:(jj + 1) * SEG] = acc.astype(BF16)
        elif j >= n_ret + n_rg + n_att:
            jj = j - n_ret - n_att
            zg_ref[:, jj * SEG:(jj + 1) * SEG] = acc.astype(BF16)
        else:
            jj = j - n_ret - n_rg
            seg, g = jj // N_GROUPS, jj % N_GROUPS
            if seg < 2:
                gain = (qg_ref if seg == 0 else kg_ref)[g:g + 1, :]
                if seg == 0:
                    gain = gain * ATT_QSCALE
                cos, sin = ca_ref[...], sa_ref[...]
                parts = []
                for hh in range(SEG // LANES):
                    a = _rms(acc[:, hh * LANES:(hh + 1) * LANES]) * gain
                    parts.append(_rotate(a, cos, sin))
                store_att(g, seg, jnp.concatenate(parts, axis=1))
            else:
                store_att(g, seg, acc)


def _in_proj(x2, g_mix, w_in, tabs, q_gain, k_gain, seq, tm=512):
    T = x2.shape[0]
    npos = seq // tm
    tab_spec = pl.BlockSpec((tm, LANES), lambda i: (i % npos, 0))
    return pl.pallas_call(
        _inproj_kernel,
        grid=(T // tm,),
        in_specs=[
            pl.BlockSpec((tm, D_MODEL), lambda i: (i, 0)),
            _resident((1, D_MODEL)),
            _resident(w_in.shape),
            tab_spec, tab_spec, tab_spec, tab_spec,
            _resident(q_gain.shape), _resident(k_gain.shape),
        ],
        out_specs=[
            pl.BlockSpec((tm, W_RET), lambda i: (i, 0)),
            pl.BlockSpec((tm, W_GATE), lambda i: (i, 0)),
        ] + [
            pl.BlockSpec((None, dil, tm // dil, 3 * ATT_W), lambda i: (i // npos, 0, i % npos, 0))
            for _, dil in ATT_PATTERNS
        ],
        out_shape=[
            jax.ShapeDtypeStruct((T, W_RET), BF16),
            jax.ShapeDtypeStruct((T, W_GATE), BF16),
        ] + [
            jax.ShapeDtypeStruct((T // seq, dil, seq // dil, 3 * ATT_W), BF16)
            for _, dil in ATT_PATTERNS
        ],
        scratch_shapes=[pltpu.VMEM((ATT_HEADS, tm, ATT_HD), F32)],
        compiler_params=pltpu.CompilerParams(
            dimension_semantics=("parallel",), vmem_limit_bytes=VMEM_LIMIT),
        name="in_proj",
    )(x2, g_mix, w_in, *tabs, q_gain, k_gain)


def _retention_kernel(q_ref, k_ref, v_ref, rg_ref, gn_ref, inner_ref, qd_ref, kd_ref, cd_ref,
                      y_ref, state_ref, *, n_chunks):
    @pl.when(pl.program_id(2) == 0)
    def _():
        state_ref[...] = jnp.zeros_like(state_ref)

    C = RET_CHUNK
    inner_decay, q_decay, k_decay, chunk_decay = inner_ref[...], qd_ref[...], kd_ref[...], cd_ref[...]
    gn = gn_ref[...]
    for c in range(n_chunks):
        rows = slice(c * C, (c + 1) * C)
        q, k, v = q_ref[rows, :], k_ref[rows, :], v_ref[rows, :]
        scores = lax.dot_general(q, k, (((1,), (1,)), ((), ())), preferred_element_type=F32) * inner_decay
        inner = jnp.dot(scores.astype(BF16), v, preferred_element_type=F32)
        state = state_ref[...]
        cross = jnp.dot((q.astype(F32) * q_decay).astype(BF16), state.astype(BF16),
                        preferred_element_type=F32)
        kd = (k.astype(F32) * k_decay).astype(BF16)
        kv = lax.dot_general(kd, v, (((0,), (0,)), ((), ())), preferred_element_type=F32)
        state_ref[...] = state * chunk_decay + kv
        y = _rms(inner + cross) * gn
        rg = rg_ref[rows, :].astype(F32)
        y_ref[rows, :] = (rg * jax.nn.sigmoid(rg) * y).astype(BF16)


def _retention(zr, zg, ret_gn, dec, ts=1024):
    B, S, _ = zr.shape
    inner_decay, q_decay, k_decay, chunk_decay = dec
    head_tab = lambda shape: pl.BlockSpec((None,) + shape, lambda b, h, s: (h, 0, 0))
    return pl.pallas_call(
        functools.partial(_retention_kernel, n_chunks=ts // RET_CHUNK),
        grid=(B, RET_HEADS, S // ts),
        in_specs=[
            pl.BlockSpec((None, ts, RET_DK), lambda b, h, s: (b, s, h)),
            pl.BlockSpec((None, ts, RET_DK), lambda b, h, s: (b, s, RET_HEADS + h)),
            pl.BlockSpec((None, ts, RET_DV), lambda b, h, s: (b, s, 2 * RET_QK // RET_DV + h)),
            pl.BlockSpec((None, ts, RET_DV), lambda b, h, s: (b, s, h)),
            pl.BlockSpec((1, RET_DV), lambda b, h, s: (0, h)),
            head_tab((RET_CHUNK, RET_CHUNK)),
            head_tab((RET_CHUNK, RET_DK)),
            head_tab((RET_CHUNK, RET_DK)),
            head_tab((RET_DK, RET_DV)),
        ],
        out_specs=pl.BlockSpec((None, ts, RET_DV), lambda b, h, s: (b, s, h)),
        out_shape=jax.ShapeDtypeStruct((B, S, RET_V), BF16),
        scratch_shapes=[pltpu.VMEM((RET_DK, RET_DV), F32)],
        compiler_params=pltpu.CompilerParams(
            dimension_semantics=("parallel", "parallel", "arbitrary"), vmem_limit_bytes=VMEM_LIMIT),
        name="retention",
    )(zr, zr, zr, zg, ret_gn, inner_decay, q_decay, k_decay, chunk_decay)


def _attn_unit(q, parts):
    nt = (((1,), (1,)), ((), ()))
    s = [lax.dot_general(q, k, nt, preferred_element_type=F32) + bias for k, _, bias in parts]
    s = s[0] if len(s) == 1 else jnp.concatenate(s, axis=1)
    m = jnp.max(s, axis=-1, keepdims=True)
    e = jnp.exp2(s - m).astype(BF16)
    ol, off = None, 0
    for _, v, _ in parts:
        n = v.shape[0]
        t = jnp.dot(e[:, off:off + n], jnp.concatenate([v, jnp.ones_like(v)], axis=1),
                    preferred_element_type=F32)
        ol = t if ol is None else ol + t
        off += n
    return ol[:, :ATT_HD], m, ol[:, ATT_HD:]


def _attention_kernel(q_ref, k_ref, v_ref, *rest, n_res, n_blk, has_prev):
    if has_prev:
        kp_ref, vp_ref, o_ref, st_ref = rest
    else:
        o_ref, st_ref = rest
    W = ATT_BLK
    neg = jnp.float32(-jnp.inf)
    row2 = lax.broadcasted_iota(jnp.int32, (W, 2 * W), 0)
    col2 = lax.broadcasted_iota(jnp.int32, (W, 2 * W), 1)
    band = jnp.where((col2 >= row2) & (col2 <= row2 + W), 0.0, neg)
    first = jnp.where(col2 <= row2, 0.0, neg)
    lane_head = lax.broadcasted_iota(jnp.int32, (W, LANES), 1) // (LANES // ATT_HEADS)
    if has_prev:
        no_prev = W * (1 - jnp.minimum(pl.program_id(1), 1))
        prev_own = jnp.where((col2 >= row2) & (col2 <= row2 + W) & (col2 >= no_prev), 0.0, neg)

    for r in range(n_res):
        for b in range(n_blk):
            rows = slice(b * W, (b + 1) * W)
            m_tile = jnp.zeros((W, LANES), F32)
            l_tile = jnp.zeros((W, LANES), F32)
            for hh in range(ATT_HEADS):
                cols = slice(hh * ATT_HD, (hh + 1) * ATT_HD)
                q = q_ref[r, rows, cols]
                if b > 0:
                    keys = slice((b - 1) * W, (b + 1) * W)
                    parts = [(k_ref[r, keys, cols], v_ref[r, keys, cols], band)]
                elif has_prev:
                    parts = [(jnp.concatenate([kp_ref[0, :, cols], k_ref[r, rows, cols]], axis=0),
                              jnp.concatenate([vp_ref[0, :, cols], v_ref[r, rows, cols]], axis=0),
                              prev_own)]
                else:
                    keys = slice(0, 2 * W)
                    parts = [(k_ref[r, keys, cols], v_ref[r, keys, cols], first)]
                o, m, l = _attn_unit(q, parts)
                o_ref[r, rows, cols] = o.astype(BF16)
                m_tile = jnp.where(lane_head == hh, m, m_tile)
                l_tile = jnp.where(lane_head == hh, l, l_tile)
            st_ref[r, rows, :LANES] = m_tile
            st_ref[r, rows, LANES:] = l_tile


def _attention(za, g, rows_per_step=1024):
    window, dil = ATT_PATTERNS[g]
    assert window // dil == ATT_BLK
    batch, _, L, _ = za.shape
    has_prev = L > rows_per_step
    lc = min(L, rows_per_step)
    n_res = rows_per_step // lc
    n_blk = lc // ATT_BLK
    assert n_blk >= 2
    grid = (batch, L // lc if has_prev else dil // n_res)
    idx = (lambda b, j, seg: (b, 0, j, seg)) if has_prev else (lambda b, j, seg: (b, j, 0, seg))
    in_specs = [pl.BlockSpec((None, n_res, lc, ATT_W), functools.partial(idx, seg=seg))
                for seg in range(3)]
    operands = [za, za, za]
    if has_prev:
        in_specs += [pl.BlockSpec((None, 1, ATT_BLK, ATT_W),
                                  lambda b, j, seg=seg: (b, 0, jnp.maximum(j * n_blk - 1, 0), seg))
                     for seg in (1, 2)]
        operands += [za, za]
    out_specs = [pl.BlockSpec((None, n_res, lc, ATT_W), functools.partial(idx, seg=0)),
                 pl.BlockSpec((None, n_res, lc, 2 * LANES), functools.partial(idx, seg=0))]
    out_shape = [jax.ShapeDtypeStruct((batch, dil, L, ATT_W), BF16),
                 jax.ShapeDtypeStruct((batch, dil, L, 2 * LANES), F32)]
    return pl.pallas_call(
        functools.partial(_attention_kernel, n_res=n_res, n_blk=n_blk, has_prev=has_prev),
        grid=grid, in_specs=in_specs, out_specs=out_specs, out_shape=out_shape,
        compiler_params=pltpu.CompilerParams(
            dimension_semantics=("parallel", "parallel"), vmem_limit_bytes=VMEM_LIMIT),
        name=f"attention_g{g}",
    )(*operands)


def _merge_proj_kernel(x_ref, y_ref, o0_ref, o1_ref, o2_ref, s0_ref, s1_ref, s2_ref, gr_ref, ga_ref,
                       bg_ref, wr_ref, wa_ref, wo_ref, out_ref, *perm_refs):
    tm = x_ref.shape[0]

    def token_order(ref, scratch):
        dil = ref.shape[0]
        if dil == 1:
            return ref[0].astype(F32)
        n_slab = ref.shape[2] // LANES
        for r in range(dil):
            for c in range(n_slab):
                scratch[c, pl.ds(r, tm // dil, stride=dil), :] = (
                    ref[r, :, c * LANES:(c + 1) * LANES].astype(F32))
        return jnp.concatenate([scratch[c] for c in range(n_slab)], axis=1)

    o_tok = [token_order(o0_ref, None), token_order(o1_ref, perm_refs[0]),
             token_order(o2_ref, perm_refs[1])]
    stats = [token_order(s0_ref, None), token_order(s1_ref, perm_refs[2]),
             token_order(s2_ref, perm_refs[3])]
    ms = [st[:, :LANES] for st in stats]
    lses = [st[:, :LANES] + jnp.log2(st[:, LANES:]) for st in stats]
    top = jnp.maximum(jnp.maximum(lses[0], lses[1]), lses[2])
    den = jnp.exp2(lses[0] - top) + jnp.exp2(lses[1] - top) + jnp.exp2(lses[2] - top)
    wts = [jnp.exp2(m - top) / den for m in ms]
    per_head = LANES // ATT_HEADS
    parts = []
    for hh in range(ATT_HEADS):
        cols = slice(hh * ATT_HD, (hh + 1) * ATT_HD)
        acc = None
        for w, o_g in zip(wts, o_tok):
            t = w[:, hh * per_head:hh * per_head + 1] * o_g[:, cols]
            acc = t if acc is None else acc + t
        parts.append(acc)
    o = jnp.concatenate(parts, axis=1).astype(BF16)
    ret_branch = jnp.dot(y_ref[...], wr_ref[...], preferred_element_type=F32)
    att_branch = jnp.dot(o, wa_ref[...], preferred_element_type=F32)
    bg = bg_ref[...]
    gate_r = jax.nn.sigmoid(gr_ref[...].astype(F32) + bg[:, :D_MODEL])
    gate_a = jax.nn.sigmoid(ga_ref[...].astype(F32) + bg[:, D_MODEL:])
    mix = (gate_r * ret_branch + gate_a * att_branch).astype(BF16)
    out_ref[...] = x_ref[...] + jnp.dot(mix, wo_ref[...], preferred_element_type=F32)


def _merge_proj(x2, yg, os_, ss_, zg, b_gate, w_ret_out, w_att_out, w_o, tm=512):
    T = x2.shape[0]
    npos = os_[0].shape[1] * os_[0].shape[2] // tm
    row = lambda w: pl.BlockSpec((tm, w), lambda i: (i, 0))
    res_major = lambda a: pl.BlockSpec((None, a.shape[1], tm // a.shape[1], a.shape[3]),
                                       lambda i: (i // npos, 0, i % npos, 0))
    dilated = [a for a in os_ + ss_ if a.shape[1] > 1]
    return pl.pallas_call(
        _merge_proj_kernel,
        grid=(T // tm,),
        in_specs=[
            row(D_MODEL), row(RET_V), *[res_major(a) for a in os_ + ss_],
            pl.BlockSpec((tm, D_MODEL), lambda i: (i, RET_V // D_MODEL)),
            pl.BlockSpec((tm, D_MODEL), lambda i: (i, RET_V // D_MODEL + 1)),
            _resident(b_gate.shape), _resident(w_ret_out.shape), _resident(w_att_out.shape),
            _resident(w_o.shape),
        ],
        out_specs=row(D_MODEL),
        out_shape=jax.ShapeDtypeStruct((T, D_MODEL), F32),
        scratch_shapes=[pltpu.VMEM((a.shape[3] // LANES, tm, LANES), F32) for a in dilated],
        compiler_params=pltpu.CompilerParams(
            dimension_semantics=("parallel",), vmem_limit_bytes=VMEM_LIMIT),
        name="merge_proj",
    )(x2, yg, *os_, *ss_, zg, zg, b_gate, w_ret_out, w_att_out, w_o)


def _mlp_ple_kernel(x_ref, p_ref, gm_ref, wu_ref, wd_ref, gp_ref, wpg_ref, wpp_ref, out_ref, *, ff_chunk):
    x = x_ref[...]
    hn = (_rms(x) * gm_ref[...]).astype(BF16)
    acc = x
    for c in range(D_FF // ff_chunk):
        cols = slice(c * ff_chunk, (c + 1) * ff_chunk)
        u = jnp.dot(hn, wu_ref[:, cols], preferred_element_type=F32)
        a = jnp.square(jnp.maximum(u, 0.0)).astype(BF16)
        acc = acc + jnp.dot(a, wd_ref[cols, :], preferred_element_type=F32)
    hp = (_rms(acc) * gp_ref[...]).astype(BF16)
    gate_p = jax.nn.sigmoid(jnp.dot(hp, wpg_ref[...], preferred_element_type=F32))
    proj = jnp.dot(p_ref[...].astype(BF16), wpp_ref[...], preferred_element_type=F32)
    out_ref[...] = acc + gate_p * proj


def _mlp_ple(x1, p2, g_mlp, w_up, w_down, g_ple, w_pg, w_pp, tm=512, ff_chunk=1024):
    T = x1.shape[0]
    return pl.pallas_call(
        functools.partial(_mlp_ple_kernel, ff_chunk=ff_chunk),
        grid=(T // tm,),
        in_specs=[
            pl.BlockSpec((tm, D_MODEL), lambda i: (i, 0)),
            pl.BlockSpec((tm, PLE_DIM), lambda i: (i, 0)),
            _resident((1, D_MODEL)), _resident(w_up.shape), _resident(w_down.shape),
            _resident((1, D_MODEL)), _resident(w_pg.shape), _resident(w_pp.shape),
        ],
        out_specs=pl.BlockSpec((tm, D_MODEL), lambda i: (i, 0)),
        out_shape=jax.ShapeDtypeStruct((T, D_MODEL), F32),
        compiler_params=pltpu.CompilerParams(
            dimension_semantics=("parallel",), vmem_limit_bytes=VMEM_LIMIT),
        name="mlp_ple",
    )(x1, p2, g_mlp, w_up, w_down, g_ple, w_pg, w_pp)


def _rot_tables(inv_freq, seq):
    ang = jnp.arange(seq, dtype=F32)[:, None] * inv_freq[None, :]
    cos, sin = jnp.cos(ang), jnp.sin(ang)
    return jnp.concatenate([cos, cos], axis=1), jnp.concatenate([-sin, sin], axis=1)


def _decay_tables():
    H, C = RET_HEADS, RET_CHUNK
    log_g = jnp.log1p(-jnp.exp2(-5.0 - jnp.arange(H, dtype=F32)))
    idx = jnp.arange(C, dtype=F32)
    diff = idx[:, None] - idx[None, :]
    inner = jnp.where(diff >= 0, jnp.exp(log_g[:, None, None] * jnp.maximum(diff, 0.0)), 0.0)
    q_decay = jnp.exp(log_g[:, None] * (idx[None, :] + 1.0))
    k_decay = jnp.exp(log_g[:, None] * (C - 1.0 - idx[None, :]))
    chunk_decay = jnp.exp(log_g * C)
    return (inner,
            jnp.broadcast_to(q_decay[:, :, None], (H, C, RET_DK)),
            jnp.broadcast_to(k_decay[:, :, None], (H, C, RET_DK)),
            jnp.broadcast_to(chunk_decay[:, None, None], (H, RET_DK, RET_DV)))


def _layer(x, p_i, w_in, b_gate, g_mix, q_gain, k_gain, ret_gn, w_ret_out, w_att_out, w_o,
           g_mlp, w_up, w_down, g_ple, w_ple_proj, w_ple_gate):
    B, S, D = x.shape
    T = B * S
    x2 = x.reshape(T, D)
    ret_freq = 1.0 / (10000.0 ** jnp.linspace(0.0, 1.0, RET_DK // 2, dtype=F32))
    rope_freq = ROPE_THETA ** (-jnp.arange(0, ATT_HD, 2, dtype=F32) / ATT_HD)
    tabs = _rot_tables(ret_freq, S) + _rot_tables(rope_freq, S)

    zr, zg, *zas = _in_proj(x2, g_mix.reshape(1, D), w_in.astype(BF16), tabs, q_gain, k_gain, S)
    yg = _retention(zr.reshape(B, S, W_RET), zg.reshape(B, S, W_GATE), ret_gn.reshape(1, RET_V),
                    _decay_tables()).reshape(T, RET_V)
    att = [_attention(zas[g], g) for g in range(N_GROUPS)]
    x1 = _merge_proj(x2, yg, [o for o, _ in att], [s for _, s in att], zg, b_gate.reshape(1, -1),
                     w_ret_out.astype(BF16), w_att_out.astype(BF16), w_o.astype(BF16))
    out = _mlp_ple(x1, p_i.reshape(T, PLE_DIM), g_mlp.reshape(1, D), w_up.astype(BF16),
                   w_down.astype(BF16), g_ple.reshape(1, D), w_ple_gate.astype(BF16),
                   w_ple_proj.astype(BF16))
    return out.reshape(B, S, D)


def kernel(x, p, w_in, b_gate, g_mix, q_gain, k_gain, ret_gn, w_ret_out, w_att_out, w_o, g_mlp, w_up,
           w_down, g_ple, w_ple_proj, w_ple_gate):
    for i in range(p.shape[0]):
        x = _layer(x, p[i], w_in[i], b_gate[i], g_mix[i], q_gain[i], k_gain[i], ret_gn[i],
                   w_ret_out[i], w_att_out[i], w_o[i], g_mlp[i], w_up[i], w_down[i], g_ple[i],
                   w_ple_proj[i], w_ple_gate[i])
    return x
```

```python
import functools

import jax
import jax.numpy as jnp
import numpy as np
from jax import lax
from jax.experimental import pallas as pl
from jax.experimental.pallas import tpu as pltpu

F32 = jnp.float32
BF16 = jnp.bfloat16

D_MODEL = 1024
PLE_DIM = 256
D_FF = 4 * D_MODEL
EPS = 1e-6
RET_HEADS = 4
RET_DK = 128
RET_DV = 256
RET_CHUNK = 128
RET_QK = RET_HEADS * RET_DK
RET_V = RET_HEADS * RET_DV
ATT_PATTERNS = ((128, 1), (512, 4), (2048, 16))
N_GROUPS = len(ATT_PATTERNS)
ATT_HEADS = 4
ATT_HD = 128
ATT_W = ATT_HEADS * ATT_HD
ATT_BLK = 128
ROPE_THETA = 10000.0
LOG2E = 1.4426950408889634
ATT_QSCALE = ATT_HD ** -0.5 * LOG2E

LANES = 128
SEG = 512
W_RET = 2 * RET_QK + RET_V
W_GATE = RET_V + 2 * D_MODEL
W_ATT = 3 * N_GROUPS * ATT_W
VMEM_LIMIT = 56 * 1024 * 1024


def _resident(shape):
    return pl.BlockSpec(shape, lambda *_: (0,) * len(shape), pipeline_mode=pl.Buffered(1))


def _rms(x):
    return x * lax.rsqrt(jnp.mean(x * x, axis=-1, keepdims=True) + EPS)


def _sigmoid(x):
    return 0.5 * jnp.tanh(0.5 * x) + 0.5


def _rotate(a, cos, sin_signed):
    return a * cos + pltpu.roll(a, LANES // 2, axis=1) * sin_signed


def _inproj_kernel(x_ref, g_ref, w_ref, cr_ref, sr_ref, ca_ref, sa_ref, qg_ref, kg_ref,
                   zr_ref, zg_ref, za0_ref, za1_ref, za2_ref, perm_ref):
    za_refs = (za0_ref, za1_ref, za2_ref)
    tm = x_ref.shape[0]

    def store_att(g, seg, val):
        cols = slice(seg * ATT_W, (seg + 1) * ATT_W)
        dil = ATT_PATTERNS[g][1]
        if dil == 1:
            za_refs[g][0, :, cols] = val.astype(BF16)
            return
        for hh in range(ATT_HEADS):
            perm_ref[hh] = val[:, hh * ATT_HD:(hh + 1) * ATT_HD]
        for r in range(dil):
            for hh in range(ATT_HEADS):
                c0 = seg * ATT_W + hh * ATT_HD
                za_refs[g][r, :, c0:c0 + ATT_HD] = (
                    perm_ref[hh, pl.ds(r, tm // dil, stride=dil), :].astype(BF16))

    h = (_rms(x_ref[...]) * g_ref[...]).astype(BF16)
    n_ret, n_rg, n_att = W_RET // SEG, RET_V // SEG, W_ATT // SEG
    n_all = (W_RET + W_GATE + W_ATT) // SEG
    heavy = [0, 1] + list(range(n_ret + n_rg, n_ret + n_rg + n_att))
    for j in heavy + [j for j in range(n_all) if j not in heavy]:
        acc = jnp.dot(h, w_ref[:, j * SEG:(j + 1) * SEG], preferred_element_type=F32)
        if j < 2:
            cos, sin = cr_ref[...], sr_ref[...]
            for hh in range(SEG // LANES):
                r = _rotate(acc[:, hh * LANES:(hh + 1) * LANES], cos, sin)
                if j == 1:
                    r = r * (RET_DK ** -0.5)
                zr_ref[:, j * SEG + hh * LANES:j * SEG + (hh + 1) * LANES] = r.astype(BF16)
        elif j < n_ret:
            zr_ref[:, j * SEG:(j + 1) * SEG] = acc.astype(BF16)
        elif j < n_ret + n_rg:
            jj = j - n_ret
            zg_ref[:, jj * SEG:(jj + 1) * SEG] = acc.astype(BF16)
        elif j >= n_ret + n_rg + n_att:
            jj = j - n_ret - n_att
            zg_ref[:, jj * SEG:(jj + 1) * SEG] = acc.astype(BF16)
        else:
            jj = j - n_ret - n_rg
            seg, g = jj // N_GROUPS, jj % N_GROUPS
            if seg < 2:
                gain = (qg_ref if seg == 0 else kg_ref)[g:g + 1, :]
                if seg == 0:
                    gain = gain * ATT_QSCALE
                cos, sin = ca_ref[...], sa_ref[...]
                parts = []
                for hh in range(SEG // LANES):
                    a = _rms(acc[:, hh * LANES:(hh + 1) * LANES]) * gain
                    parts.append(_rotate(a, cos, sin))
                store_att(g, seg, jnp.concatenate(parts, axis=1))
            else:
                store_att(g, seg, acc)


def _in_proj(x2, g_mix, w_in, tabs, q_gain, k_gain, seq, tm=512):
    T = x2.shape[0]
    npos = seq // tm
    tab_spec = pl.BlockSpec((tm, LANES), lambda i: (i % npos, 0))
    return pl.pallas_call(
        _inproj_kernel,
        grid=(T // tm,),
        in_specs=[
            pl.BlockSpec((tm, D_MODEL), lambda i: (i, 0)),
            _resident((1, D_MODEL)),
            _resident(w_in.shape),
            tab_spec, tab_spec, tab_spec, tab_spec,
            _resident(q_gain.shape), _resident(k_gain.shape),
        ],
        out_specs=[
            pl.BlockSpec((tm, W_RET), lambda i: (i, 0)),
            pl.BlockSpec((tm, W_GATE), lambda i: (i, 0)),
        ] + [
            pl.BlockSpec((None, dil, tm // dil, 3 * ATT_W), lambda i: (i // npos, 0, i % npos, 0))
            for _, dil in ATT_PATTERNS
        ],
        out_shape=[
            jax.ShapeDtypeStruct((T, W_RET), BF16),
            jax.ShapeDtypeStruct((T, W_GATE), BF16),
        ] + [
            jax.ShapeDtypeStruct((T // seq, dil, seq // dil, 3 * ATT_W), BF16)
            for _, dil in ATT_PATTERNS
        ],
        scratch_shapes=[pltpu.VMEM((ATT_HEADS, tm, ATT_HD), F32)],
        compiler_params=pltpu.CompilerParams(
            dimension_semantics=("parallel",), vmem_limit_bytes=VMEM_LIMIT),
        name="in_proj",
    )(x2, g_mix, w_in, *tabs, q_gain, k_gain)


def _retention_kernel(q_ref, k_ref, v_ref, rg_ref, gn_ref, inner_ref, qd_ref, kd_ref, cd_ref,
                      y_ref, *, n_chunks):
    C = RET_CHUNK
    inner_decay, q_decay, k_decay, chunk_decay = inner_ref[...], qd_ref[...], kd_ref[...], cd_ref[...]
    gn = gn_ref[...]
    lhs, kvs = [], []
    for c in range(n_chunks):
        rows = slice(c * C, (c + 1) * C)
        q, k, v = q_ref[rows, :], k_ref[rows, :], v_ref[rows, :]
        scores = lax.dot_general(q, k, (((1,), (1,)), ((), ())), preferred_element_type=F32) * inner_decay
        qd = (q.astype(F32) * q_decay).astype(BF16)
        lhs.append(jnp.concatenate([scores.astype(BF16), qd], axis=1))
        kd = (k.astype(F32) * k_decay).astype(BF16)
        kvs.append(lax.dot_general(kd, v, (((0,), (0,)), ((), ())), preferred_element_type=F32))
    state = jnp.zeros((RET_DK, RET_DV), F32)
    for c in range(n_chunks):
        rows = slice(c * C, (c + 1) * C)
        rhs = jnp.concatenate([v_ref[rows, :], state.astype(BF16)], axis=0)
        y = jnp.dot(lhs[c], rhs, preferred_element_type=F32)
        state = state * chunk_decay + kvs[c]
        half_rg = rg_ref[rows, :].astype(F32)
        silu = half_rg * jnp.tanh(half_rg) + half_rg
        y_ref[rows, :] = (silu * (_rms(y) * gn)).astype(BF16)


def _retention(zr, zg, ret_gn, dec):
    B, S, _ = zr.shape
    ts = S
    inner_decay, q_decay, k_decay, chunk_decay = dec
    head_tab = lambda shape: pl.BlockSpec((None,) + shape, lambda b, h, s: (h, 0, 0))
    return pl.pallas_call(
        functools.partial(_retention_kernel, n_chunks=ts // RET_CHUNK),
        grid=(B, RET_HEADS, S // ts),
        in_specs=[
            pl.BlockSpec((None, ts, RET_DK), lambda b, h, s: (b, s, h)),
            pl.BlockSpec((None, ts, RET_DK), lambda b, h, s: (b, s, RET_HEADS + h)),
            pl.BlockSpec((None, ts, RET_DV), lambda b, h, s: (b, s, 2 * RET_QK // RET_DV + h)),
            pl.BlockSpec((None, ts, RET_DV), lambda b, h, s: (b, s, h)),
            pl.BlockSpec((1, RET_DV), lambda b, h, s: (0, h)),
            head_tab((RET_CHUNK, RET_CHUNK)),
            head_tab((RET_CHUNK, RET_DK)),
            head_tab((RET_CHUNK, RET_DK)),
            head_tab((RET_DK, RET_DV)),
        ],
        out_specs=pl.BlockSpec((None, ts, RET_DV), lambda b, h, s: (b, s, h)),
        out_shape=jax.ShapeDtypeStruct((B, S, RET_V), BF16),
        compiler_params=pltpu.CompilerParams(
            dimension_semantics=("parallel", "parallel", "arbitrary"), vmem_limit_bytes=VMEM_LIMIT),
        name="retention",
    )(zr, zr, zr, zg, ret_gn, inner_decay, q_decay, k_decay, chunk_decay)


def _attn_unit(q, parts):
    nt = (((1,), (1,)), ((), ()))
    s = [lax.dot_general(q, k, nt, preferred_element_type=F32) + bias for k, _, bias in parts]
    s = s[0] if len(s) == 1 else jnp.concatenate(s, axis=1)
    m = jnp.max(s, axis=-1, keepdims=True)
    e = jnp.exp2(s - m).astype(BF16)
    ol, off = None, 0
    for _, v, _ in parts:
        n = v.shape[0]
        t = jnp.dot(e[:, off:off + n], jnp.concatenate([v, jnp.ones_like(v)], axis=1),
                    preferred_element_type=F32)
        ol = t if ol is None else ol + t
        off += n
    return ol[:, :ATT_HD], m, ol[:, ATT_HD:]


def _attention_kernel(q_ref, k_ref, v_ref, *rest, n_res, n_blk, has_prev):
    if has_prev:
        kp_ref, vp_ref, o_ref, st_ref = rest
    else:
        o_ref, st_ref = rest
    W = ATT_BLK
    neg = jnp.float32(-jnp.inf)
    row2 = lax.broadcasted_iota(jnp.int32, (W, 2 * W), 0)
    col2 = lax.broadcasted_iota(jnp.int32, (W, 2 * W), 1)
    band = jnp.where((col2 >= row2) & (col2 <= row2 + W), 0.0, neg)
    first = jnp.where(col2 <= row2, 0.0, neg)
    lane_head = lax.broadcasted_iota(jnp.int32, (W, LANES), 1) // (LANES // ATT_HEADS)
    if has_prev:
        no_prev = W * (1 - jnp.minimum(pl.program_id(1), 1))
        prev_own = jnp.where((col2 >= row2) & (col2 <= row2 + W) & (col2 >= no_prev), 0.0, neg)

    for r in range(n_res):
        for b in range(n_blk):
            rows = slice(b * W, (b + 1) * W)
            m_tile = jnp.zeros((W, LANES), F32)
            l_tile = jnp.zeros((W, LANES), F32)
            for hh in range(ATT_HEADS):
                cols = slice(hh * ATT_HD, (hh + 1) * ATT_HD)
                q = q_ref[r, rows, cols]
                if b > 0:
                    keys = slice((b - 1) * W, (b + 1) * W)
                    parts = [(k_ref[r, keys, cols], v_ref[r, keys, cols], band)]
                elif has_prev:
                    parts = [(jnp.concatenate([kp_ref[0, :, cols], k_ref[r, rows, cols]], axis=0),
                              jnp.concatenate([vp_ref[0, :, cols], v_ref[r, rows, cols]], axis=0),
                              prev_own)]
                else:
                    keys = slice(0, 2 * W)
                    parts = [(k_ref[r, keys, cols], v_ref[r, keys, cols], first)]
                o, m, l = _attn_unit(q, parts)
                o_ref[r, rows, cols] = o.astype(BF16)
                m_tile = jnp.where(lane_head == hh, m, m_tile)
                l_tile = jnp.where(lane_head == hh, l, l_tile)
            st_ref[r, rows, :LANES] = m_tile
            st_ref[r, rows, LANES:] = l_tile


def _attention(za, g, rows_per_step=1024):
    window, dil = ATT_PATTERNS[g]
    assert window // dil == ATT_BLK
    batch, _, L, _ = za.shape
    has_prev = L > rows_per_step
    lc = min(L, rows_per_step)
    n_res = rows_per_step // lc
    n_blk = lc // ATT_BLK
    assert n_blk >= 2
    grid = (batch, L // lc if has_prev else dil // n_res)
    idx = (lambda b, j, seg: (b, 0, j, seg)) if has_prev else (lambda b, j, seg: (b, j, 0, seg))
    in_specs = [pl.BlockSpec((None, n_res, lc, ATT_W), functools.partial(idx, seg=seg))
                for seg in range(3)]
    operands = [za, za, za]
    if has_prev:
        in_specs += [pl.BlockSpec((None, 1, ATT_BLK, ATT_W),
                                  lambda b, j, seg=seg: (b, 0, jnp.maximum(j * n_blk - 1, 0), seg))
                     for seg in (1, 2)]
        operands += [za, za]
    out_specs = [pl.BlockSpec((None, n_res, lc, ATT_W), functools.partial(idx, seg=0)),
                 pl.BlockSpec((None, n_res, lc, 2 * LANES), functools.partial(idx, seg=0))]
    out_shape = [jax.ShapeDtypeStruct((batch, dil, L, ATT_W), BF16),
                 jax.ShapeDtypeStruct((batch, dil, L, 2 * LANES), F32)]
    return pl.pallas_call(
        functools.partial(_attention_kernel, n_res=n_res, n_blk=n_blk, has_prev=has_prev),
        grid=grid, in_specs=in_specs, out_specs=out_specs, out_shape=out_shape,
        compiler_params=pltpu.CompilerParams(
            dimension_semantics=("parallel", "parallel"), vmem_limit_bytes=VMEM_LIMIT),
        name=f"attention_g{g}",
    )(*operands)


def _merge_proj_kernel(x_ref, y_ref, o0_ref, o1_ref, o2_ref, s0_ref, s1_ref, s2_ref, gr_ref, ga_ref,
                       bg_ref, wr_ref, wa_ref, wo_ref, out_ref, *perm_refs):
    tm = x_ref.shape[0]

    def token_order(ref, scratch):
        dil = ref.shape[0]
        if dil == 1:
            return ref[0].astype(F32)
        n_slab = ref.shape[2] // LANES
        for r in range(dil):
            for c in range(n_slab):
                scratch[c, pl.ds(r, tm // dil, stride=dil), :] = (
                    ref[r, :, c * LANES:(c + 1) * LANES].astype(F32))
        return jnp.concatenate([scratch[c] for c in range(n_slab)], axis=1)

    o_tok = [token_order(o0_ref, None), token_order(o1_ref, perm_refs[0]),
             token_order(o2_ref, perm_refs[1])]
    stats = [token_order(s0_ref, None), token_order(s1_ref, perm_refs[2]),
             token_order(s2_ref, perm_refs[3])]
    ms = [st[:, :LANES] for st in stats]
    lses = [st[:, :LANES] + jnp.log2(st[:, LANES:]) for st in stats]
    top = jnp.maximum(jnp.maximum(lses[0], lses[1]), lses[2])
    den = jnp.exp2(lses[0] - top) + jnp.exp2(lses[1] - top) + jnp.exp2(lses[2] - top)
    wts = [jnp.exp2(m - top) / den for m in ms]
    per_head = LANES // ATT_HEADS
    parts = []
    for hh in range(ATT_HEADS):
        cols = slice(hh * ATT_HD, (hh + 1) * ATT_HD)
        acc = None
        for w, o_g in zip(wts, o_tok):
            t = w[:, hh * per_head:hh * per_head + 1] * o_g[:, cols]
            acc = t if acc is None else acc + t
        parts.append(acc)
    o = jnp.concatenate(parts, axis=1).astype(BF16)
    ret_branch = jnp.dot(y_ref[...], wr_ref[...], preferred_element_type=F32)
    att_branch = jnp.dot(o, wa_ref[...], preferred_element_type=F32)
    bg = bg_ref[...]
    gate_r = _sigmoid(gr_ref[...].astype(F32) + bg[:, :D_MODEL])
    gate_a = _sigmoid(ga_ref[...].astype(F32) + bg[:, D_MODEL:])
    mix = (gate_r * ret_branch + gate_a * att_branch).astype(BF16)
    out_ref[...] = x_ref[...] + jnp.dot(mix, wo_ref[...], preferred_element_type=F32)


def _merge_proj(x2, yg, os_, ss_, zg, b_gate, w_ret_out, w_att_out, w_o, tm=512):
    T = x2.shape[0]
    npos = os_[0].shape[1] * os_[0].shape[2] // tm
    row = lambda w: pl.BlockSpec((tm, w), lambda i: (i, 0))
    res_major = lambda a: pl.BlockSpec((None, a.shape[1], tm // a.shape[1], a.shape[3]),
                                       lambda i: (i // npos, 0, i % npos, 0))
    dilated = [a for a in os_ + ss_ if a.shape[1] > 1]
    return pl.pallas_call(
        _merge_proj_kernel,
        grid=(T // tm,),
        in_specs=[
            row(D_MODEL), row(RET_V), *[res_major(a) for a in os_ + ss_],
            pl.BlockSpec((tm, D_MODEL), lambda i: (i, RET_V // D_MODEL)),
            pl.BlockSpec((tm, D_MODEL), lambda i: (i, RET_V // D_MODEL + 1)),
            _resident(b_gate.shape), _resident(w_ret_out.shape), _resident(w_att_out.shape),
            _resident(w_o.shape),
        ],
        out_specs=row(D_MODEL),
        out_shape=jax.ShapeDtypeStruct((T, D_MODEL), F32),
        scratch_shapes=[pltpu.VMEM((a.shape[3] // LANES, tm, LANES), F32) for a in dilated],
        compiler_params=pltpu.CompilerParams(
            dimension_semantics=("parallel",), vmem_limit_bytes=VMEM_LIMIT),
        name="merge_proj",
    )(x2, yg, *os_, *ss_, zg, zg, b_gate, w_ret_out, w_att_out, w_o)


def _mlp_ple_kernel(x_ref, p_ref, gm_ref, wu_ref, wd_ref, gp_ref, wpg_ref, wpp_ref, out_ref, *, ff_chunk):
    x = x_ref[...]
    hn = (_rms(x) * gm_ref[...]).astype(BF16)
    acc = x
    for c in range(D_FF // ff_chunk):
        cols = slice(c * ff_chunk, (c + 1) * ff_chunk)
        u = jnp.dot(hn, wu_ref[:, cols], preferred_element_type=F32)
        a = jnp.square(jnp.maximum(u, 0.0)).astype(BF16)
        acc = acc + jnp.dot(a, wd_ref[cols, :], preferred_element_type=F32)
    hp = (_rms(acc) * gp_ref[...]).astype(BF16)
    gate_p = _sigmoid(jnp.dot(hp, wpg_ref[...], preferred_element_type=F32))
    proj = jnp.dot(p_ref[...].astype(BF16), wpp_ref[...], preferred_element_type=F32)
    out_ref[...] = acc + gate_p * proj


def _mlp_ple(x1, p2, g_mlp, w_up, w_down, g_ple, w_pg, w_pp, tm=512, ff_chunk=1024):
    T = x1.shape[0]
    return pl.pallas_call(
        functools.partial(_mlp_ple_kernel, ff_chunk=ff_chunk),
        grid=(T // tm,),
        in_specs=[
            pl.BlockSpec((tm, D_MODEL), lambda i: (i, 0)),
            pl.BlockSpec((tm, PLE_DIM), lambda i: (i, 0)),
            _resident((1, D_MODEL)), _resident(w_up.shape), _resident(w_down.shape),
            _resident((1, D_MODEL)), _resident(w_pg.shape), _resident(w_pp.shape),
        ],
        out_specs=pl.BlockSpec((tm, D_MODEL), lambda i: (i, 0)),
        out_shape=jax.ShapeDtypeStruct((T, D_MODEL), F32),
        compiler_params=pltpu.CompilerParams(
            dimension_semantics=("parallel",), vmem_limit_bytes=VMEM_LIMIT),
        name="mlp_ple",
    )(x1, p2, g_mlp, w_up, w_down, g_ple, w_pg, w_pp)


def _rot_tables(inv_freq, seq):
    ang = np.arange(seq, dtype=np.float64)[:, None] * inv_freq[None, :]
    cos, sin = np.cos(ang), np.sin(ang)
    return (np.concatenate([cos, cos], axis=1).astype(np.float32),
            np.concatenate([-sin, sin], axis=1).astype(np.float32))


def _decay_tables():
    H, C = RET_HEADS, RET_CHUNK
    log_g = np.log1p(-np.exp2(-5.0 - np.arange(H, dtype=np.float64)))
    idx = np.arange(C, dtype=np.float64)
    diff = idx[:, None] - idx[None, :]
    inner = np.where(diff >= 0, np.exp(log_g[:, None, None] * np.maximum(diff, 0.0)), 0.0)
    q_decay = np.exp(log_g[:, None] * (idx[None, :] + 1.0))
    k_decay = np.exp(log_g[:, None] * (C - 1.0 - idx[None, :]))
    chunk_decay = np.exp(log_g * C)
    f32 = lambda a: np.ascontiguousarray(a, dtype=np.float32)
    return (f32(inner),
            f32(np.broadcast_to(q_decay[:, :, None], (H, C, RET_DK))),
            f32(np.broadcast_to(k_decay[:, :, None], (H, C, RET_DK))),
            f32(np.broadcast_to(chunk_decay[:, None, None], (H, RET_DK, RET_DV))))


def _layer(x, p_i, w_in, b_gate, g_mix, q_gain, k_gain, ret_gn, w_ret_out, w_att_out, w_o,
           g_mlp, w_up, w_down, g_ple, w_ple_proj, w_ple_gate):
    B, S, D = x.shape
    T = B * S
    x2 = x.reshape(T, D)
    ret_freq = 1.0 / (10000.0 ** np.linspace(0.0, 1.0, RET_DK // 2))
    rope_freq = ROPE_THETA ** (-np.arange(0, ATT_HD, 2, dtype=np.float64) / ATT_HD)
    tabs = _rot_tables(ret_freq, S) + _rot_tables(rope_freq, S)

    col_scale = jnp.ones((w_in.shape[1],), F32).at[W_RET:W_RET + RET_V].set(0.5)
    zr, zg, *zas = _in_proj(x2, g_mix.reshape(1, D), (w_in * col_scale).astype(BF16), tabs,
                            q_gain, k_gain, S)
    yg = _retention(zr.reshape(B, S, W_RET), zg.reshape(B, S, W_GATE), ret_gn.reshape(1, RET_V),
                    _decay_tables()).reshape(T, RET_V)
    att = [_attention(zas[g], g) for g in range(N_GROUPS)]
    x1 = _merge_proj(x2, yg, [o for o, _ in att], [s for _, s in att], zg, b_gate.reshape(1, -1),
                     w_ret_out.astype(BF16), w_att_out.astype(BF16), w_o.astype(BF16))
    out = _mlp_ple(x1, p_i.reshape(T, PLE_DIM), g_mlp.reshape(1, D), w_up.astype(BF16),
                   w_down.astype(BF16), g_ple.reshape(1, D), w_ple_gate.astype(BF16),
                   w_ple_proj.astype(BF16))
    return out.reshape(B, S, D)


def kernel(x, p, w_in, b_gate, g_mix, q_gain, k_gain, ret_gn, w_ret_out, w_att_out, w_o, g_mlp, w_up,
           w_down, g_ple, w_ple_proj, w_ple_gate):
    for i in range(p.shape[0]):
        x = _layer(x, p[i], w_in[i], b_gate[i], g_mix[i], q_gain[i], k_gain[i], ret_gn[i],
                   w_ret_out[i], w_att_out[i], w_o[i], g_mlp[i], w_up[i], w_down[i], g_ple[i],
                   w_ple_proj[i], w_ple_gate[i])
    return x
```

```python
import functools

import jax
import jax.numpy as jnp
import numpy as np
from jax import lax
from jax.experimental import pallas as pl
from jax.experimental.pallas import tpu as pltpu

F32 = jnp.float32
BF16 = jnp.bfloat16

D_MODEL = 1024
PLE_DIM = 256
D_FF = 4 * D_MODEL
EPS = 1e-6
RET_HEADS = 4
RET_DK = 128
RET_DV = 256
RET_CHUNK = 128
RET_QK = RET_HEADS * RET_DK
RET_V = RET_HEADS * RET_DV
ATT_PATTERNS = ((128, 1), (512, 4), (2048, 16))
N_GROUPS = len(ATT_PATTERNS)
ATT_HEADS = 4
ATT_HD = 128
ATT_W = ATT_HEADS * ATT_HD
ATT_BLK = 128
ROPE_THETA = 10000.0
LOG2E = 1.4426950408889634
ATT_QSCALE = ATT_HD ** -0.5 * LOG2E

LANES = 128
STAT_LANES = LANES // (2 * ATT_HEADS)
SEG = 512
W_RET = 2 * RET_QK + RET_V
W_GATE = RET_V + 2 * D_MODEL
W_ATT = 3 * N_GROUPS * ATT_W
VMEM_LIMIT = 56 * 1024 * 1024


def _resident(shape):
    return pl.BlockSpec(shape, lambda *_: (0,) * len(shape), pipeline_mode=pl.Buffered(1))


def _rms(x):
    return x * lax.rsqrt(jnp.mean(x * x, axis=-1, keepdims=True) + EPS)


def _sigmoid(x):
    return 0.5 * jnp.tanh(0.5 * x) + 0.5


def _rotate(a, cos, sin_signed):
    return a * cos + pltpu.roll(a, LANES // 2, axis=1) * sin_signed


def _inproj_kernel(x_ref, g_ref, w_ref, cr_ref, sr_ref, ca_ref, sa_ref, qg_ref, kg_ref,
                   zr_ref, zg_ref, za0_ref, za1_ref, za2_ref, perm_ref):
    za_refs = (za0_ref, za1_ref, za2_ref)
    tm = x_ref.shape[0]

    def store_att(g, seg, val):
        cols = slice(seg * ATT_W, (seg + 1) * ATT_W)
        dil = ATT_PATTERNS[g][1]
        if dil == 1:
            za_refs[g][0, :, cols] = val.astype(BF16)
            return
        for hh in range(ATT_HEADS):
            perm_ref[hh] = val[:, hh * ATT_HD:(hh + 1) * ATT_HD]
        for r in range(dil):
            for hh in range(ATT_HEADS):
                c0 = seg * ATT_W + hh * ATT_HD
                za_refs[g][r, :, c0:c0 + ATT_HD] = (
                    perm_ref[hh, pl.ds(r, tm // dil, stride=dil), :].astype(BF16))

    h = (_rms(x_ref[...]) * g_ref[...]).astype(BF16)
    n_ret, n_rg, n_att = W_RET // SEG, RET_V // SEG, W_ATT // SEG
    n_all = (W_RET + W_GATE + W_ATT) // SEG
    heavy = [0, 1] + list(range(n_ret + n_rg, n_ret + n_rg + n_att))
    for j in heavy + [j for j in range(n_all) if j not in heavy]:
        acc = jnp.dot(h, w_ref[:, j * SEG:(j + 1) * SEG], preferred_element_type=F32)
        if j < 2:
            cos, sin = cr_ref[...], sr_ref[...]
            for hh in range(SEG // LANES):
                r = _rotate(acc[:, hh * LANES:(hh + 1) * LANES], cos, sin)
                if j == 1:
                    r = r * (RET_DK ** -0.5)
                zr_ref[:, j * SEG + hh * LANES:j * SEG + (hh + 1) * LANES] = r.astype(BF16)
        elif j < n_ret:
            zr_ref[:, j * SEG:(j + 1) * SEG] = acc.astype(BF16)
        elif j < n_ret + n_rg:
            jj = j - n_ret
            zg_ref[:, jj * SEG:(jj + 1) * SEG] = acc.astype(BF16)
        elif j >= n_ret + n_rg + n_att:
            jj = j - n_ret - n_att
            zg_ref[:, jj * SEG:(jj + 1) * SEG] = acc.astype(BF16)
        else:
            jj = j - n_ret - n_rg
            seg, g = jj // N_GROUPS, jj % N_GROUPS
            if seg < 2:
                gain = (qg_ref if seg == 0 else kg_ref)[g:g + 1, :]
                if seg == 0:
                    gain = gain * ATT_QSCALE
                cos, sin = ca_ref[...], sa_ref[...]
                parts = []
                for hh in range(SEG // LANES):
                    a = _rms(acc[:, hh * LANES:(hh + 1) * LANES]) * gain
                    parts.append(_rotate(a, cos, sin))
                store_att(g, seg, jnp.concatenate(parts, axis=1))
            else:
                store_att(g, seg, acc)


def _in_proj(x2, g_mix, w_in, tabs, q_gain, k_gain, seq, tm=512):
    T = x2.shape[0]
    npos = seq // tm
    tab_spec = pl.BlockSpec((tm, LANES), lambda i: (i % npos, 0))
    return pl.pallas_call(
        _inproj_kernel,
        grid=(T // tm,),
        in_specs=[
            pl.BlockSpec((tm, D_MODEL), lambda i: (i, 0)),
            _resident((1, D_MODEL)),
            _resident(w_in.shape),
            tab_spec, tab_spec, tab_spec, tab_spec,
            _resident(q_gain.shape), _resident(k_gain.shape),
        ],
        out_specs=[
            pl.BlockSpec((tm, W_RET), lambda i: (i, 0)),
            pl.BlockSpec((tm, W_GATE), lambda i: (i, 0)),
        ] + [
            pl.BlockSpec((None, dil, tm // dil, 3 * ATT_W), lambda i: (i // npos, 0, i % npos, 0))
            for _, dil in ATT_PATTERNS
        ],
        out_shape=[
            jax.ShapeDtypeStruct((T, W_RET), BF16),
            jax.ShapeDtypeStruct((T, W_GATE), BF16),
        ] + [
            jax.ShapeDtypeStruct((T // seq, dil, seq // dil, 3 * ATT_W), BF16)
            for _, dil in ATT_PATTERNS
        ],
        scratch_shapes=[pltpu.VMEM((ATT_HEADS, tm, ATT_HD), F32)],
        compiler_params=pltpu.CompilerParams(
            dimension_semantics=("parallel",), vmem_limit_bytes=VMEM_LIMIT),
        name="in_proj",
    )(x2, g_mix, w_in, *tabs, q_gain, k_gain)


def _retention_kernel(q_ref, k_ref, v_ref, rg_ref, gn_ref, inner_ref, qd_ref, kd_ref, cd_ref,
                      y_ref, *, n_chunks):
    C = RET_CHUNK
    inner_decay, q_decay, k_decay, chunk_decay = inner_ref[...], qd_ref[...], kd_ref[...], cd_ref[...]
    gn = gn_ref[...]
    lhs, kvs = [], []
    for c in range(n_chunks):
        rows = slice(c * C, (c + 1) * C)
        q, k, v = q_ref[rows, :], k_ref[rows, :], v_ref[rows, :]
        scores = lax.dot_general(q, k, (((1,), (1,)), ((), ())), preferred_element_type=F32) * inner_decay
        qd = (q.astype(F32) * q_decay).astype(BF16)
        lhs.append(jnp.concatenate([scores.astype(BF16), qd], axis=1))
        kd = (k.astype(F32) * k_decay).astype(BF16)
        kvs.append(lax.dot_general(kd, v, (((0,), (0,)), ((), ())), preferred_element_type=F32))
    state = jnp.zeros((RET_DK, RET_DV), F32)
    for c in range(n_chunks):
        rows = slice(c * C, (c + 1) * C)
        rhs = jnp.concatenate([v_ref[rows, :], state.astype(BF16)], axis=0)
        y = jnp.dot(lhs[c], rhs, preferred_element_type=F32)
        state = state * chunk_decay + kvs[c]
        half_rg = rg_ref[rows, :].astype(F32)
        silu = half_rg * jnp.tanh(half_rg) + half_rg
        y_ref[rows, :] = (silu * (_rms(y) * gn)).astype(BF16)


def _retention(zr, zg, ret_gn, dec):
    B, S, _ = zr.shape
    ts = S
    inner_decay, q_decay, k_decay, chunk_decay = dec
    head_tab = lambda shape: pl.BlockSpec((None,) + shape, lambda b, h, s: (h, 0, 0))
    return pl.pallas_call(
        functools.partial(_retention_kernel, n_chunks=ts // RET_CHUNK),
        grid=(B, RET_HEADS, S // ts),
        in_specs=[
            pl.BlockSpec((None, ts, RET_DK), lambda b, h, s: (b, s, h)),
            pl.BlockSpec((None, ts, RET_DK), lambda b, h, s: (b, s, RET_HEADS + h)),
            pl.BlockSpec((None, ts, RET_DV), lambda b, h, s: (b, s, 2 * RET_QK // RET_DV + h)),
            pl.BlockSpec((None, ts, RET_DV), lambda b, h, s: (b, s, h)),
            pl.BlockSpec((1, RET_DV), lambda b, h, s: (0, h)),
            head_tab((RET_CHUNK, RET_CHUNK)),
            head_tab((RET_CHUNK, RET_DK)),
            head_tab((RET_CHUNK, RET_DK)),
            head_tab((RET_DK, RET_DV)),
        ],
        out_specs=pl.BlockSpec((None, ts, RET_DV), lambda b, h, s: (b, s, h)),
        out_shape=jax.ShapeDtypeStruct((B, S, RET_V), BF16),
        compiler_params=pltpu.CompilerParams(
            dimension_semantics=("parallel", "parallel", "arbitrary"), vmem_limit_bytes=VMEM_LIMIT),
        name="retention",
    )(zr, zr, zr, zg, ret_gn, inner_decay, q_decay, k_decay, chunk_decay)


def _attn_unit(q, parts):
    nt = (((1,), (1,)), ((), ()))
    s = [lax.dot_general(q, k, nt, preferred_element_type=F32) + bias for k, _, bias in parts]
    s = s[0] if len(s) == 1 else jnp.concatenate(s, axis=1)
    m = jnp.max(s, axis=-1, keepdims=True)
    e = jnp.exp2(s - m).astype(BF16)
    ol, off = None, 0
    for _, v, _ in parts:
        n = v.shape[0]
        t = jnp.dot(e[:, off:off + n], jnp.concatenate([v, jnp.ones_like(v)], axis=1),
                    preferred_element_type=F32)
        ol = t if ol is None else ol + t
        off += n
    return ol[:, :ATT_HD], m, ol[:, ATT_HD:]


def _attention_kernel(q_ref, k_ref, v_ref, *rest, n_res, n_blk, has_prev):
    if has_prev:
        kp_ref, vp_ref, o_ref, st_ref = rest
    else:
        o_ref, st_ref = rest
    W = ATT_BLK
    neg = jnp.float32(-jnp.inf)
    row2 = lax.broadcasted_iota(jnp.int32, (W, 2 * W), 0)
    col2 = lax.broadcasted_iota(jnp.int32, (W, 2 * W), 1)
    band = jnp.where((col2 >= row2) & (col2 <= row2 + W), 0.0, neg)
    first = jnp.where(col2 <= row2, 0.0, neg)
    lane_half = lax.broadcasted_iota(jnp.int32, (W, LANES), 1) // STAT_LANES
    if has_prev:
        no_prev = W * (1 - jnp.minimum(pl.program_id(1), 1))
        prev_own = jnp.where((col2 >= row2) & (col2 <= row2 + W) & (col2 >= no_prev), 0.0, neg)

    for r in range(n_res):
        for b in range(n_blk):
            rows = slice(b * W, (b + 1) * W)
            st_tile = jnp.zeros((W, LANES), F32)
            for hh in range(ATT_HEADS):
                cols = slice(hh * ATT_HD, (hh + 1) * ATT_HD)
                q = q_ref[r, rows, cols]
                if b > 0:
                    keys = slice((b - 1) * W, (b + 1) * W)
                    parts = [(k_ref[r, keys, cols], v_ref[r, keys, cols], band)]
                elif has_prev:
                    parts = [(jnp.concatenate([kp_ref[0, :, cols], k_ref[r, rows, cols]], axis=0),
                              jnp.concatenate([vp_ref[0, :, cols], v_ref[r, rows, cols]], axis=0),
                              prev_own)]
                else:
                    keys = slice(0, 2 * W)
                    parts = [(k_ref[r, keys, cols], v_ref[r, keys, cols], first)]
                o, m, l = _attn_unit(q, parts)
                o_ref[r, rows, cols] = o.astype(BF16)
                st_tile = jnp.where(lane_half == 2 * hh, m, st_tile)
                st_tile = jnp.where(lane_half == 2 * hh + 1, l, st_tile)
            st_ref[r, rows, :] = st_tile


def _attention(za, g, rows_per_step=1024):
    window, dil = ATT_PATTERNS[g]
    assert window // dil == ATT_BLK
    batch, _, L, _ = za.shape
    has_prev = L > rows_per_step
    lc = min(L, rows_per_step)
    n_res = rows_per_step // lc
    n_blk = lc // ATT_BLK
    assert n_blk >= 2
    grid = (batch, L // lc if has_prev else dil // n_res)
    idx = (lambda b, j, seg: (b, 0, j, seg)) if has_prev else (lambda b, j, seg: (b, j, 0, seg))
    in_specs = [pl.BlockSpec((None, n_res, lc, ATT_W), functools.partial(idx, seg=seg))
                for seg in range(3)]
    operands = [za, za, za]
    if has_prev:
        in_specs += [pl.BlockSpec((None, 1, ATT_BLK, ATT_W),
                                  lambda b, j, seg=seg: (b, 0, jnp.maximum(j * n_blk - 1, 0), seg))
                     for seg in (1, 2)]
        operands += [za, za]
    out_specs = [pl.BlockSpec((None, n_res, lc, ATT_W), functools.partial(idx, seg=0)),
                 pl.BlockSpec((None, n_res, lc, LANES), functools.partial(idx, seg=0))]
    out_shape = [jax.ShapeDtypeStruct((batch, dil, L, ATT_W), BF16),
                 jax.ShapeDtypeStruct((batch, dil, L, LANES), F32)]
    return pl.pallas_call(
        functools.partial(_attention_kernel, n_res=n_res, n_blk=n_blk, has_prev=has_prev),
        grid=grid, in_specs=in_specs, out_specs=out_specs, out_shape=out_shape,
        compiler_params=pltpu.CompilerParams(
            dimension_semantics=("parallel", "parallel"), vmem_limit_bytes=VMEM_LIMIT),
        name=f"attention_g{g}",
    )(*operands)


def _merge_mlp_kernel(x_ref, y_ref, o0_ref, o1_ref, o2_ref, s0_ref, s1_ref, s2_ref, gr_ref, ga_ref,
                      bg_ref, wr_ref, wa_ref, wo_ref, p_ref, gm_ref, wu_ref, wd_ref, gp_ref, wpg_ref,
                      wpp_ref, out_ref, *perm_refs, ff_chunk):
    tm = x_ref.shape[0]

    def token_order(ref, scratch):
        dil = ref.shape[0]
        if dil == 1:
            return ref[0].astype(F32)
        n_slab = ref.shape[2] // LANES
        for r in range(dil):
            for c in range(n_slab):
                scratch[c, pl.ds(r, tm // dil, stride=dil), :] = (
                    ref[r, :, c * LANES:(c + 1) * LANES].astype(F32))
        return jnp.concatenate([scratch[c] for c in range(n_slab)], axis=1)

    o_tok = [token_order(o0_ref, None), token_order(o1_ref, perm_refs[0]),
             token_order(o2_ref, perm_refs[1])]
    stats = [token_order(s0_ref, None), token_order(s1_ref, perm_refs[2]),
             token_order(s2_ref, perm_refs[3])]
    sum_lane = (lax.broadcasted_iota(jnp.int32, (tm, LANES), 1) // STAT_LANES) % 2 == 1
    ms = stats
    lses = [st + pltpu.roll(jnp.log2(jnp.where(sum_lane, st, 1.0)), LANES - STAT_LANES, axis=1)
            for st in stats]
    top = jnp.maximum(jnp.maximum(lses[0], lses[1]), lses[2])
    den = jnp.exp2(lses[0] - top) + jnp.exp2(lses[1] - top) + jnp.exp2(lses[2] - top)
    wts = [jnp.exp2(m - top) / den for m in ms]
    per_head = LANES // ATT_HEADS
    parts = []
    for hh in range(ATT_HEADS):
        cols = slice(hh * ATT_HD, (hh + 1) * ATT_HD)
        acc = None
        for w, o_g in zip(wts, o_tok):
            t = w[:, hh * per_head:hh * per_head + 1] * o_g[:, cols]
            acc = t if acc is None else acc + t
        parts.append(acc)
    o = jnp.concatenate(parts, axis=1).astype(BF16)
    ret_branch = jnp.dot(y_ref[...], wr_ref[...], preferred_element_type=F32)
    att_branch = jnp.dot(o, wa_ref[...], preferred_element_type=F32)
    bg = bg_ref[...]
    gate_r = _sigmoid(gr_ref[...].astype(F32) + bg[:, :D_MODEL])
    gate_a = _sigmoid(ga_ref[...].astype(F32) + bg[:, D_MODEL:])
    mix = (gate_r * ret_branch + gate_a * att_branch).astype(BF16)
    x1 = x_ref[...] + jnp.dot(mix, wo_ref[...], preferred_element_type=F32)

    hn = (_rms(x1) * gm_ref[...]).astype(BF16)
    acc = x1
    for c in range(D_FF // ff_chunk):
        cols = slice(c * ff_chunk, (c + 1) * ff_chunk)
        u = jnp.dot(hn, wu_ref[:, cols], preferred_element_type=F32)
        a = jnp.square(jnp.maximum(u, 0.0)).astype(BF16)
        acc = acc + jnp.dot(a, wd_ref[cols, :], preferred_element_type=F32)
    hp = (_rms(acc) * gp_ref[...]).astype(BF16)
    gate_p = _sigmoid(jnp.dot(hp, wpg_ref[...], preferred_element_type=F32))
    proj = jnp.dot(p_ref[...].astype(BF16), wpp_ref[...], preferred_element_type=F32)
    out_ref[...] = acc + gate_p * proj


def _merge_mlp(x2, yg, os_, ss_, zg, b_gate, w_ret_out, w_att_out, w_o, p2, g_mlp, w_up, w_down,
               g_ple, w_pg, w_pp, tm=512, ff_chunk=1024):
    T = x2.shape[0]
    npos = os_[0].shape[1] * os_[0].shape[2] // tm
    row = lambda w: pl.BlockSpec((tm, w), lambda i: (i, 0))
    res_major = lambda a: pl.BlockSpec((None, a.shape[1], tm // a.shape[1], a.shape[3]),
                                       lambda i: (i // npos, 0, i % npos, 0))
    dilated = [a for a in os_ + ss_ if a.shape[1] > 1]
    return pl.pallas_call(
        functools.partial(_merge_mlp_kernel, ff_chunk=ff_chunk),
        grid=(T // tm,),
        in_specs=[
            row(D_MODEL), row(RET_V), *[res_major(a) for a in os_ + ss_],
            pl.BlockSpec((tm, D_MODEL), lambda i: (i, RET_V // D_MODEL)),
            pl.BlockSpec((tm, D_MODEL), lambda i: (i, RET_V // D_MODEL + 1)),
            _resident(b_gate.shape), _resident(w_ret_out.shape), _resident(w_att_out.shape),
            _resident(w_o.shape),
            row(PLE_DIM), _resident((1, D_MODEL)), _resident(w_up.shape), _resident(w_down.shape),
            _resident((1, D_MODEL)), _resident(w_pg.shape), _resident(w_pp.shape),
        ],
        out_specs=row(D_MODEL),
        out_shape=jax.ShapeDtypeStruct((T, D_MODEL), F32),
        scratch_shapes=[pltpu.VMEM((a.shape[3] // LANES, tm, LANES), F32) for a in dilated],
        compiler_params=pltpu.CompilerParams(
            dimension_semantics=("parallel",), vmem_limit_bytes=VMEM_LIMIT),
        name="merge_mlp",
    )(x2, yg, *os_, *ss_, zg, zg, b_gate, w_ret_out, w_att_out, w_o, p2, g_mlp, w_up, w_down, g_ple,
      w_pg, w_pp)


def _rot_tables(inv_freq, seq):
    ang = np.arange(seq, dtype=np.float64)[:, None] * inv_freq[None, :]
    cos, sin = np.cos(ang), np.sin(ang)
    return (np.concatenate([cos, cos], axis=1).astype(np.float32),
            np.concatenate([-sin, sin], axis=1).astype(np.float32))


def _decay_tables():
    H, C = RET_HEADS, RET_CHUNK
    log_g = np.log1p(-np.exp2(-5.0 - np.arange(H, dtype=np.float64)))
    idx = np.arange(C, dtype=np.float64)
    diff = idx[:, None] - idx[None, :]
    inner = np.where(diff >= 0, np.exp(log_g[:, None, None] * np.maximum(diff, 0.0)), 0.0)
    q_decay = np.exp(log_g[:, None] * (idx[None, :] + 1.0))
    k_decay = np.exp(log_g[:, None] * (C - 1.0 - idx[None, :]))
    chunk_decay = np.exp(log_g * C)
    f32 = lambda a: np.ascontiguousarray(a, dtype=np.float32)
    return (f32(inner),
            f32(np.broadcast_to(q_decay[:, :, None], (H, C, RET_DK))),
            f32(np.broadcast_to(k_decay[:, :, None], (H, C, RET_DK))),
            f32(np.broadcast_to(chunk_decay[:, None, None], (H, RET_DK, RET_DV))))


def _layer(x, p_i, w_in, b_gate, g_mix, q_gain, k_gain, ret_gn, w_ret_out, w_att_out, w_o,
           g_mlp, w_up, w_down, g_ple, w_ple_proj, w_ple_gate):
    B, S, D = x.shape
    T = B * S
    x2 = x.reshape(T, D)
    ret_freq = 1.0 / (10000.0 ** np.linspace(0.0, 1.0, RET_DK // 2))
    rope_freq = ROPE_THETA ** (-np.arange(0, ATT_HD, 2, dtype=np.float64) / ATT_HD)
    tabs = _rot_tables(ret_freq, S) + _rot_tables(rope_freq, S)

    col_scale = jnp.ones((w_in.shape[1],), F32).at[W_RET:W_RET + RET_V].set(0.5)
    zr, zg, *zas = _in_proj(x2, g_mix.reshape(1, D), (w_in * col_scale).astype(BF16), tabs,
                            q_gain, k_gain, S)
    yg = _retention(zr.reshape(B, S, W_RET), zg.reshape(B, S, W_GATE), ret_gn.reshape(1, RET_V),
                    _decay_tables()).reshape(T, RET_V)
    att = [_attention(zas[g], g) for g in range(N_GROUPS)]
    out = _merge_mlp(x2, yg, [o for o, _ in att], [s for _, s in att], zg, b_gate.reshape(1, -1),
                     w_ret_out.astype(BF16), w_att_out.astype(BF16), w_o.astype(BF16),
                     p_i.reshape(T, PLE_DIM), g_mlp.reshape(1, D), w_up.astype(BF16),
                     w_down.astype(BF16), g_ple.reshape(1, D), w_ple_gate.astype(BF16),
                     w_ple_proj.astype(BF16))
    return out.reshape(B, S, D)


def kernel(x, p, w_in, b_gate, g_mix, q_gain, k_gain, ret_gn, w_ret_out, w_att_out, w_o, g_mlp, w_up,
           w_down, g_ple, w_ple_proj, w_ple_gate):
    for i in range(p.shape[0]):
        x = _layer(x, p[i], w_in[i], b_gate[i], g_mix[i], q_gain[i], k_gain[i], ret_gn[i],
                   w_ret_out[i], w_att_out[i], w_o[i], g_mlp[i], w_up[i], w_down[i], g_ple[i],
                   w_ple_proj[i], w_ple_gate[i])
    return x
```

```python
import functools

import jax
import jax.numpy as jnp
import numpy as np
from jax import lax
from jax.experimental import pallas as pl
from jax.experimental.pallas import tpu as pltpu

F32 = jnp.float32
BF16 = jnp.bfloat16

D_MODEL = 1024
PLE_DIM = 256
D_FF = 4 * D_MODEL
EPS = 1e-6
RET_HEADS = 4
RET_DK = 128
RET_DV = 256
RET_CHUNK = 128
RET_QK = RET_HEADS * RET_DK
RET_V = RET_HEADS * RET_DV
ATT_PATTERNS = ((128, 1), (512, 4), (2048, 16))
N_GROUPS = len(ATT_PATTERNS)
ATT_HEADS = 4
ATT_HD = 128
ATT_W = ATT_HEADS * ATT_HD
ATT_BLK = 128
ROPE_THETA = 10000.0
LOG2E = 1.4426950408889634
ATT_QSCALE = ATT_HD ** -0.5 * LOG2E

LANES = 128
STAT_LANES = LANES // (2 * ATT_HEADS)
SEG = 512
W_RET = 2 * RET_QK + RET_V
W_GATE = RET_V + 2 * D_MODEL
W_ATT = 3 * N_GROUPS * ATT_W
VMEM_LIMIT = 56 * 1024 * 1024


def _resident(shape):
    return pl.BlockSpec(shape, lambda *_: (0,) * len(shape), pipeline_mode=pl.Buffered(1))


def _rms(x):
    return x * lax.rsqrt(jnp.mean(x * x, axis=-1, keepdims=True) + EPS)


def _sigmoid(x):
    return 0.5 * jnp.tanh(0.5 * x) + 0.5


def _rotate(a, cos, sin_signed):
    return a * cos + pltpu.roll(a, LANES // 2, axis=1) * sin_signed


def _inproj_kernel(x_ref, g_ref, w_ref, cr_ref, sr_ref, ca_ref, sa_ref, qg_ref, kg_ref,
                   zr_ref, zg_ref, za0_ref, za1_ref, za2_ref, perm_ref):
    za_refs = (za0_ref, za1_ref, za2_ref)
    tm = x_ref.shape[0]

    def store_att(g, seg, val):
        cols = slice(seg * ATT_W, (seg + 1) * ATT_W)
        dil = ATT_PATTERNS[g][1]
        if dil == 1:
            za_refs[g][0, :, cols] = val.astype(BF16)
            return
        for hh in range(ATT_HEADS):
            perm_ref[hh] = val[:, hh * ATT_HD:(hh + 1) * ATT_HD]
        for r in range(dil):
            for hh in range(ATT_HEADS):
                c0 = seg * ATT_W + hh * ATT_HD
                za_refs[g][r, :, c0:c0 + ATT_HD] = (
                    perm_ref[hh, pl.ds(r, tm // dil, stride=dil), :].astype(BF16))

    h = (_rms(x_ref[...]) * g_ref[...]).astype(BF16)
    n_ret, n_rg, n_att = W_RET // SEG, RET_V // SEG, W_ATT // SEG
    n_all = (W_RET + W_GATE + W_ATT) // SEG
    heavy = [0, 1] + list(range(n_ret + n_rg, n_ret + n_rg + n_att))
    for j in heavy + [j for j in range(n_all) if j not in heavy]:
        acc = jnp.dot(h, w_ref[:, j * SEG:(j + 1) * SEG], preferred_element_type=F32)
        if j < 2:
            cos, sin = cr_ref[...], sr_ref[...]
            for hh in range(SEG // LANES):
                r = _rotate(acc[:, hh * LANES:(hh + 1) * LANES], cos, sin)
                if j == 1:
                    r = r * (RET_DK ** -0.5)
                zr_ref[:, j * SEG + hh * LANES:j * SEG + (hh + 1) * LANES] = r.astype(BF16)
        elif j < n_ret:
            zr_ref[:, j * SEG:(j + 1) * SEG] = acc.astype(BF16)
        elif j < n_ret + n_rg:
            jj = j - n_ret
            zg_ref[:, jj * SEG:(jj + 1) * SEG] = acc.astype(BF16)
        elif j >= n_ret + n_rg + n_att:
            jj = j - n_ret - n_att
            zg_ref[:, jj * SEG:(jj + 1) * SEG] = acc.astype(BF16)
        else:
            jj = j - n_ret - n_rg
            seg, g = jj // N_GROUPS, jj % N_GROUPS
            if seg < 2:
                gain = (qg_ref if seg == 0 else kg_ref)[g:g + 1, :]
                if seg == 0:
                    gain = gain * ATT_QSCALE
                cos, sin = ca_ref[...], sa_ref[...]
                parts = []
                for hh in range(SEG // LANES):
                    a = _rms(acc[:, hh * LANES:(hh + 1) * LANES]) * gain
                    parts.append(_rotate(a, cos, sin))
                store_att(g, seg, jnp.concatenate(parts, axis=1))
            else:
                store_att(g, seg, acc)


def _in_proj(x2, g_mix, w_in, tabs, q_gain, k_gain, seq, tm=512):
    T = x2.shape[0]
    npos = seq // tm
    tab_spec = pl.BlockSpec((tm, LANES), lambda i: (i % npos, 0))
    return pl.pallas_call(
        _inproj_kernel,
        grid=(T // tm,),
        in_specs=[
            pl.BlockSpec((tm, D_MODEL), lambda i: (i, 0)),
            _resident((1, D_MODEL)),
            _resident(w_in.shape),
            tab_spec, tab_spec, tab_spec, tab_spec,
            _resident(q_gain.shape), _resident(k_gain.shape),
        ],
        out_specs=[
            pl.BlockSpec((tm, W_RET), lambda i: (i, 0)),
            pl.BlockSpec((tm, W_GATE), lambda i: (i, 0)),
        ] + [
            pl.BlockSpec((None, dil, tm // dil, 3 * ATT_W), lambda i: (i // npos, 0, i % npos, 0))
            for _, dil in ATT_PATTERNS
        ],
        out_shape=[
            jax.ShapeDtypeStruct((T, W_RET), BF16),
            jax.ShapeDtypeStruct((T, W_GATE), BF16),
        ] + [
            jax.ShapeDtypeStruct((T // seq, dil, seq // dil, 3 * ATT_W), BF16)
            for _, dil in ATT_PATTERNS
        ],
        scratch_shapes=[pltpu.VMEM((ATT_HEADS, tm, ATT_HD), F32)],
        compiler_params=pltpu.CompilerParams(
            dimension_semantics=("parallel",), vmem_limit_bytes=VMEM_LIMIT),
        name="in_proj",
    )(x2, g_mix, w_in, *tabs, q_gain, k_gain)


def _retention_kernel(q_ref, k_ref, v_ref, rg_ref, gn_ref, inner_ref, qd_ref, kd_ref, cd_ref,
                      y_ref, *, n_chunks):
    C = RET_CHUNK
    inner_decay, q_decay, k_decay, chunk_decay = inner_ref[...], qd_ref[...], kd_ref[...], cd_ref[...]
    gn = gn_ref[...]
    lhs, kvs = [], []
    for c in range(n_chunks):
        rows = slice(c * C, (c + 1) * C)
        q, k, v = q_ref[rows, :], k_ref[rows, :], v_ref[rows, :]
        scores = lax.dot_general(q, k, (((1,), (1,)), ((), ())), preferred_element_type=F32) * inner_decay
        qd = (q.astype(F32) * q_decay).astype(BF16)
        lhs.append(jnp.concatenate([scores.astype(BF16), qd], axis=1))
        kd = (k.astype(F32) * k_decay).astype(BF16)
        kvs.append(lax.dot_general(kd, v, (((0,), (0,)), ((), ())), preferred_element_type=F32))
    state = jnp.zeros((RET_DK, RET_DV), F32)
    for c in range(n_chunks):
        rows = slice(c * C, (c + 1) * C)
        rhs = jnp.concatenate([v_ref[rows, :], state.astype(BF16)], axis=0)
        y = jnp.dot(lhs[c], rhs, preferred_element_type=F32)
        state = state * chunk_decay + kvs[c]
        half_rg = rg_ref[rows, :].astype(F32)
        silu = half_rg * jnp.tanh(half_rg) + half_rg
        y_ref[rows, :] = (silu * (_rms(y) * gn)).astype(BF16)


def _retention(zr, zg, ret_gn, dec):
    B, S, _ = zr.shape
    ts = S
    inner_decay, q_decay, k_decay, chunk_decay = dec
    head_tab = lambda shape: pl.BlockSpec((None,) + shape, lambda b, h, s: (h, 0, 0))
    return pl.pallas_call(
        functools.partial(_retention_kernel, n_chunks=ts // RET_CHUNK),
        grid=(B, RET_HEADS, S // ts),
        in_specs=[
            pl.BlockSpec((None, ts, RET_DK), lambda b, h, s: (b, s, h)),
            pl.BlockSpec((None, ts, RET_DK), lambda b, h, s: (b, s, RET_HEADS + h)),
            pl.BlockSpec((None, ts, RET_DV), lambda b, h, s: (b, s, 2 * RET_QK // RET_DV + h)),
            pl.BlockSpec((None, ts, RET_DV), lambda b, h, s: (b, s, h)),
            pl.BlockSpec((1, RET_DV), lambda b, h, s: (0, h)),
            head_tab((RET_CHUNK, RET_CHUNK)),
            head_tab((RET_CHUNK, RET_DK)),
            head_tab((RET_CHUNK, RET_DK)),
            head_tab((RET_DK, RET_DV)),
        ],
        out_specs=pl.BlockSpec((None, ts, RET_DV), lambda b, h, s: (b, s, h)),
        out_shape=jax.ShapeDtypeStruct((B, S, RET_V), BF16),
        compiler_params=pltpu.CompilerParams(
            dimension_semantics=("parallel", "parallel", "arbitrary"), vmem_limit_bytes=VMEM_LIMIT),
        name="retention",
    )(zr, zr, zr, zg, ret_gn, inner_decay, q_decay, k_decay, chunk_decay)


def _attn_unit(q, parts):
    nt = (((1,), (1,)), ((), ()))
    s = [lax.dot_general(q, k, nt, preferred_element_type=F32) + bias for k, _, bias in parts]
    s = s[0] if len(s) == 1 else jnp.concatenate(s, axis=1)
    m = jnp.max(s, axis=-1, keepdims=True)
    e = jnp.exp2(s - m).astype(BF16)
    ol, off = None, 0
    for _, v, _ in parts:
        n = v.shape[0]
        t = jnp.dot(e[:, off:off + n], jnp.concatenate([v, jnp.ones_like(v)], axis=1),
                    preferred_element_type=F32)
        ol = t if ol is None else ol + t
        off += n
    return ol[:, :ATT_HD], m, ol[:, ATT_HD:]


def _attention_kernel(q_ref, k_ref, v_ref, *rest, n_res, n_blk, has_prev):
    if has_prev:
        kp_ref, vp_ref, o_ref, st_ref = rest
    else:
        o_ref, st_ref = rest
    W = ATT_BLK
    neg = jnp.float32(-jnp.inf)
    row2 = lax.broadcasted_iota(jnp.int32, (W, 2 * W), 0)
    col2 = lax.broadcasted_iota(jnp.int32, (W, 2 * W), 1)
    band = jnp.where((col2 >= row2) & (col2 <= row2 + W), 0.0, neg)
    first = jnp.where(col2 <= row2, 0.0, neg)
    lane_half = lax.broadcasted_iota(jnp.int32, (W, LANES), 1) // STAT_LANES
    if has_prev:
        no_prev = W * (1 - jnp.minimum(pl.program_id(1), 1))
        prev_own = jnp.where((col2 >= row2) & (col2 <= row2 + W) & (col2 >= no_prev), 0.0, neg)

    for r in range(n_res):
        for b in range(n_blk):
            rows = slice(b * W, (b + 1) * W)
            st_tile = jnp.zeros((W, LANES), F32)
            for hh in range(ATT_HEADS):
                cols = slice(hh * ATT_HD, (hh + 1) * ATT_HD)
                q = q_ref[r, rows, cols]
                if b > 0:
                    keys = slice((b - 1) * W, (b + 1) * W)
                    parts = [(k_ref[r, keys, cols], v_ref[r, keys, cols], band)]
                elif has_prev:
                    parts = [(jnp.concatenate([kp_ref[0, :, cols], k_ref[r, rows, cols]], axis=0),
                              jnp.concatenate([vp_ref[0, :, cols], v_ref[r, rows, cols]], axis=0),
                              prev_own)]
                else:
                    keys = slice(0, 2 * W)
                    parts = [(k_ref[r, keys, cols], v_ref[r, keys, cols], first)]
                o, m, l = _attn_unit(q, parts)
                o_ref[r, rows, cols] = o.astype(BF16)
                st_tile = jnp.where(lane_half == 2 * hh, m, st_tile)
                st_tile = jnp.where(lane_half == 2 * hh + 1, l, st_tile)
            st_ref[r, rows, :] = st_tile


def _attention(za, g, rows_per_step=4096):
    window, dil = ATT_PATTERNS[g]
    assert window // dil == ATT_BLK
    batch, _, L, _ = za.shape
    has_prev = L > rows_per_step
    lc = min(L, rows_per_step)
    n_res = rows_per_step // lc
    n_blk = lc // ATT_BLK
    assert n_blk >= 2
    grid = (batch, L // lc if has_prev else dil // n_res)
    idx = (lambda b, j, seg: (b, 0, j, seg)) if has_prev else (lambda b, j, seg: (b, j, 0, seg))
    in_specs = [pl.BlockSpec((None, n_res, lc, ATT_W), functools.partial(idx, seg=seg))
                for seg in range(3)]
    operands = [za, za, za]
    if has_prev:
        in_specs += [pl.BlockSpec((None, 1, ATT_BLK, ATT_W),
                                  lambda b, j, seg=seg: (b, 0, jnp.maximum(j * n_blk - 1, 0), seg))
                     for seg in (1, 2)]
        operands += [za, za]
    out_specs = [pl.BlockSpec((None, n_res, lc, ATT_W), functools.partial(idx, seg=0)),
                 pl.BlockSpec((None, n_res, lc, LANES), functools.partial(idx, seg=0))]
    out_shape = [jax.ShapeDtypeStruct((batch, dil, L, ATT_W), BF16),
                 jax.ShapeDtypeStruct((batch, dil, L, LANES), F32)]
    return pl.pallas_call(
        functools.partial(_attention_kernel, n_res=n_res, n_blk=n_blk, has_prev=has_prev),
        grid=grid, in_specs=in_specs, out_specs=out_specs, out_shape=out_shape,
        compiler_params=pltpu.CompilerParams(
            dimension_semantics=("parallel", "parallel"), vmem_limit_bytes=VMEM_LIMIT),
        name=f"attention_g{g}",
    )(*operands)


def _merge_mlp_kernel(x_ref, y_ref, o0_ref, o1_ref, o2_ref, s0_ref, s1_ref, s2_ref, gr_ref, ga_ref,
                      bg_ref, wr_ref, wa_ref, wo_ref, p_ref, gm_ref, wu_ref, wd_ref, gp_ref, wpg_ref,
                      wpp_ref, out_ref, *perm_refs, ff_chunk):
    tm = x_ref.shape[0]

    def token_order(ref, scratch):
        dil = ref.shape[0]
        if dil == 1:
            return ref[0].astype(F32)
        n_slab = ref.shape[2] // LANES
        for r in range(dil):
            for c in range(n_slab):
                scratch[c, pl.ds(r, tm // dil, stride=dil), :] = (
                    ref[r, :, c * LANES:(c + 1) * LANES].astype(F32))
        return jnp.concatenate([scratch[c] for c in range(n_slab)], axis=1)

    o_tok = [token_order(o0_ref, None), token_order(o1_ref, perm_refs[0]),
             token_order(o2_ref, perm_refs[1])]
    stats = [token_order(s0_ref, None), token_order(s1_ref, perm_refs[2]),
             token_order(s2_ref, perm_refs[3])]
    sum_lane = (lax.broadcasted_iota(jnp.int32, (tm, LANES), 1) // STAT_LANES) % 2 == 1
    ms = stats
    lses = [st + pltpu.roll(jnp.log2(jnp.where(sum_lane, st, 1.0)), LANES - STAT_LANES, axis=1)
            for st in stats]
    top = jnp.maximum(jnp.maximum(lses[0], lses[1]), lses[2])
    den = jnp.exp2(lses[0] - top) + jnp.exp2(lses[1] - top) + jnp.exp2(lses[2] - top)
    wts = [jnp.exp2(m - top) / den for m in ms]
    per_head = LANES // ATT_HEADS
    parts = []
    for hh in range(ATT_HEADS):
        cols = slice(hh * ATT_HD, (hh + 1) * ATT_HD)
        acc = None
        for w, o_g in zip(wts, o_tok):
            t = w[:, hh * per_head:hh * per_head + 1] * o_g[:, cols]
            acc = t if acc is None else acc + t
        parts.append(acc)
    o = jnp.concatenate(parts, axis=1).astype(BF16)
    ret_branch = jnp.dot(y_ref[...], wr_ref[...], preferred_element_type=F32)
    att_branch = jnp.dot(o, wa_ref[...], preferred_element_type=F32)
    bg = bg_ref[...]
    gate_r = _sigmoid(gr_ref[...].astype(F32) + bg[:, :D_MODEL])
    gate_a = _sigmoid(ga_ref[...].astype(F32) + bg[:, D_MODEL:])
    mix = (gate_r * ret_branch + gate_a * att_branch).astype(BF16)
    x1 = x_ref[...] + jnp.dot(mix, wo_ref[...], preferred_element_type=F32)

    hn = (_rms(x1) * gm_ref[...]).astype(BF16)
    acc = x1
    for c in range(D_FF // ff_chunk):
        cols = slice(c * ff_chunk, (c + 1) * ff_chunk)
        u = jnp.dot(hn, wu_ref[:, cols], preferred_element_type=F32)
        a = jnp.square(jnp.maximum(u, 0.0)).astype(BF16)
        acc = acc + jnp.dot(a, wd_ref[cols, :], preferred_element_type=F32)
    hp = (_rms(acc) * gp_ref[...]).astype(BF16)
    gate_p = _sigmoid(jnp.dot(hp, wpg_ref[...], preferred_element_type=F32))
    proj = jnp.dot(p_ref[...].astype(BF16), wpp_ref[...], preferred_element_type=F32)
    out_ref[...] = acc + gate_p * proj


def _merge_mlp(x2, yg, os_, ss_, zg, b_gate, w_ret_out, w_att_out, w_o, p2, g_mlp, w_up, w_down,
               g_ple, w_pg, w_pp, tm=512, ff_chunk=1024):
    T = x2.shape[0]
    npos = os_[0].shape[1] * os_[0].shape[2] // tm
    row = lambda w: pl.BlockSpec((tm, w), lambda i: (i, 0))
    res_major = lambda a: pl.BlockSpec((None, a.shape[1], tm // a.shape[1], a.shape[3]),
                                       lambda i: (i // npos, 0, i % npos, 0))
    dilated = [a for a in os_ + ss_ if a.shape[1] > 1]
    return pl.pallas_call(
        functools.partial(_merge_mlp_kernel, ff_chunk=ff_chunk),
        grid=(T // tm,),
        in_specs=[
            row(D_MODEL), row(RET_V), *[res_major(a) for a in os_ + ss_],
            pl.BlockSpec((tm, D_MODEL), lambda i: (i, RET_V // D_MODEL)),
            pl.BlockSpec((tm, D_MODEL), lambda i: (i, RET_V // D_MODEL + 1)),
            _resident(b_gate.shape), _resident(w_ret_out.shape), _resident(w_att_out.shape),
            _resident(w_o.shape),
            row(PLE_DIM), _resident((1, D_MODEL)), _resident(w_up.shape), _resident(w_down.shape),
            _resident((1, D_MODEL)), _resident(w_pg.shape), _resident(w_pp.shape),
        ],
        out_specs=row(D_MODEL),
        out_shape=jax.ShapeDtypeStruct((T, D_MODEL), F32),
        scratch_shapes=[pltpu.VMEM((a.shape[3] // LANES, tm, LANES), F32) for a in dilated],
        compiler_params=pltpu.CompilerParams(
            dimension_semantics=("parallel",), vmem_limit_bytes=VMEM_LIMIT),
        name="merge_mlp",
    )(x2, yg, *os_, *ss_, zg, zg, b_gate, w_ret_out, w_att_out, w_o, p2, g_mlp, w_up, w_down, g_ple,
      w_pg, w_pp)


def _rot_tables(inv_freq, seq):
    ang = np.arange(seq, dtype=np.float64)[:, None] * inv_freq[None, :]
    cos, sin = np.cos(ang), np.sin(ang)
    return (np.concatenate([cos, cos], axis=1).astype(np.float32),
            np.concatenate([-sin, sin], axis=1).astype(np.float32))


def _decay_tables():
    H, C = RET_HEADS, RET_CHUNK
    log_g = np.log1p(-np.exp2(-5.0 - np.arange(H, dtype=np.float64)))
    idx = np.arange(C, dtype=np.float64)
    diff = idx[:, None] - idx[None, :]
    inner = np.where(diff >= 0, np.exp(log_g[:, None, None] * np.maximum(diff, 0.0)), 0.0)
    q_decay = np.exp(log_g[:, None] * (idx[None, :] + 1.0))
    k_decay = np.exp(log_g[:, None] * (C - 1.0 - idx[None, :]))
    chunk_decay = np.exp(log_g * C)
    f32 = lambda a: np.ascontiguousarray(a, dtype=np.float32)
    return (f32(inner),
            f32(np.broadcast_to(q_decay[:, :, None], (H, C, RET_DK))),
            f32(np.broadcast_to(k_decay[:, :, None], (H, C, RET_DK))),
            f32(np.broadcast_to(chunk_decay[:, None, None], (H, RET_DK, RET_DV))))


def _layer(x, p_i, w_in, b_gate, g_mix, q_gain, k_gain, ret_gn, w_ret_out, w_att_out, w_o,
           g_mlp, w_up, w_down, g_ple, w_ple_proj, w_ple_gate):
    B, S, D = x.shape
    T = B * S
    x2 = x.reshape(T, D)
    ret_freq = 1.0 / (10000.0 ** np.linspace(0.0, 1.0, RET_DK // 2))
    rope_freq = ROPE_THETA ** (-np.arange(0, ATT_HD, 2, dtype=np.float64) / ATT_HD)
    tabs = _rot_tables(ret_freq, S) + _rot_tables(rope_freq, S)

    col_scale = jnp.ones((w_in.shape[1],), F32).at[W_RET:W_RET + RET_V].set(0.5)
    zr, zg, *zas = _in_proj(x2, g_mix.reshape(1, D), (w_in * col_scale).astype(BF16), tabs,
                            q_gain, k_gain, S)
    yg = _retention(zr.reshape(B, S, W_RET), zg.reshape(B, S, W_GATE), ret_gn.reshape(1, RET_V),
                    _decay_tables()).reshape(T, RET_V)
    att = [_attention(zas[g], g) for g in range(N_GROUPS)]
    out = _merge_mlp(x2, yg, [o for o, _ in att], [s for _, s in att], zg, b_gate.reshape(1, -1),
                     w_ret_out.astype(BF16), w_att_out.astype(BF16), w_o.astype(BF16),
                     p_i.reshape(T, PLE_DIM), g_mlp.reshape(1, D), w_up.astype(BF16),
                     w_down.astype(BF16), g_ple.reshape(1, D), w_ple_gate.astype(BF16),
                     w_ple_proj.astype(BF16))
    return out.reshape(B, S, D)


def kernel(x, p, w_in, b_gate, g_mix, q_gain, k_gain, ret_gn, w_ret_out, w_att_out, w_o, g_mlp, w_up,
           w_down, g_ple, w_ple_proj, w_ple_gate):
    for i in range(p.shape[0]):
        x = _layer(x, p[i], w_in[i], b_gate[i], g_mix[i], q_gain[i], k_gain[i], ret_gn[i],
                   w_ret_out[i], w_att_out[i], w_o[i], g_mlp[i], w_up[i], w_down[i], g_ple[i],
                   w_ple_proj[i], w_ple_gate[i])
    return x
```

```python
import functools

import jax
import jax.numpy as jnp
import numpy as np
from jax import lax
from jax.experimental import pallas as pl
from jax.experimental.pallas import tpu as pltpu

F32 = jnp.float32
BF16 = jnp.bfloat16

D_MODEL = 1024
PLE_DIM = 256
D_FF = 4 * D_MODEL
EPS = 1e-6
RET_HEADS = 4
RET_DK = 128
RET_DV = 256
RET_CHUNK = 128
RET_QK = RET_HEADS * RET_DK
RET_V = RET_HEADS * RET_DV
ATT_PATTERNS = ((128, 1), (512, 4), (2048, 16))
N_GROUPS = len(ATT_PATTERNS)
ATT_HEADS = 4
ATT_HD = 128
ATT_W = ATT_HEADS * ATT_HD
ATT_BLK = 128
ROPE_THETA = 10000.0
LOG2E = 1.4426950408889634
ATT_QSCALE = ATT_HD ** -0.5 * LOG2E

LANES = 128
STAT_LANES = LANES // (2 * ATT_HEADS)
SEG = 512
W_RET = 2 * RET_QK + RET_V
W_GATE = RET_V + 2 * D_MODEL
W_ATT = 3 * N_GROUPS * ATT_W
VMEM_LIMIT = 56 * 1024 * 1024


def _resident(shape):
    return pl.BlockSpec(shape, lambda *_: (0,) * len(shape), pipeline_mode=pl.Buffered(1))


def _rms(x):
    return x * lax.rsqrt(jnp.mean(x * x, axis=-1, keepdims=True) + EPS)


def _sigmoid(x):
    return 0.5 * jnp.tanh(0.5 * x) + 0.5


def _rotate(a, cos, sin_signed):
    return a * cos + pltpu.roll(a, LANES // 2, axis=1) * sin_signed


def _inproj_kernel(x_ref, g_ref, w_ref, cr_ref, sr_ref, ca_ref, sa_ref, qg_ref, kg_ref,
                   zr_ref, zg_ref, za0_ref, za1_ref, za2_ref, perm_ref):
    za_refs = (za0_ref, za1_ref, za2_ref)
    tm = x_ref.shape[0]

    def store_att(g, seg, val):
        cols = slice(seg * ATT_W, (seg + 1) * ATT_W)
        dil = ATT_PATTERNS[g][1]
        if dil == 1:
            za_refs[g][0, :, cols] = val.astype(BF16)
            return
        for hh in range(ATT_HEADS):
            perm_ref[hh] = val[:, hh * ATT_HD:(hh + 1) * ATT_HD]
        for r in range(dil):
            for hh in range(ATT_HEADS):
                c0 = seg * ATT_W + hh * ATT_HD
                za_refs[g][r, :, c0:c0 + ATT_HD] = (
                    perm_ref[hh, pl.ds(r, tm // dil, stride=dil), :].astype(BF16))

    h = (_rms(x_ref[...]) * g_ref[...]).astype(BF16)
    n_ret, n_rg, n_att = W_RET // SEG, RET_V // SEG, W_ATT // SEG
    n_all = (W_RET + W_GATE + W_ATT) // SEG
    heavy = [0, 1] + list(range(n_ret + n_rg, n_ret + n_rg + n_att))
    for j in heavy + [j for j in range(n_all) if j not in heavy]:
        acc = jnp.dot(h, w_ref[:, j * SEG:(j + 1) * SEG], preferred_element_type=F32)
        if j < 2:
            cos, sin = cr_ref[...], sr_ref[...]
            for hh in range(SEG // LANES):
                r = _rotate(acc[:, hh * LANES:(hh + 1) * LANES], cos, sin)
                if j == 1:
                    r = r * (RET_DK ** -0.5)
                zr_ref[:, j * SEG + hh * LANES:j * SEG + (hh + 1) * LANES] = r.astype(BF16)
        elif j < n_ret:
            zr_ref[:, j * SEG:(j + 1) * SEG] = acc.astype(BF16)
        elif j < n_ret + n_rg:
            jj = j - n_ret
            zg_ref[:, jj * SEG:(jj + 1) * SEG] = acc.astype(BF16)
        elif j >= n_ret + n_rg + n_att:
            jj = j - n_ret - n_att
            zg_ref[:, jj * SEG:(jj + 1) * SEG] = acc.astype(BF16)
        else:
            jj = j - n_ret - n_rg
            seg, g = jj // N_GROUPS, jj % N_GROUPS
            if seg < 2:
                gain = (qg_ref if seg == 0 else kg_ref)[g:g + 1, :]
                if seg == 0:
                    gain = gain * ATT_QSCALE
                cos, sin = ca_ref[...], sa_ref[...]
                parts = []
                for hh in range(SEG // LANES):
                    a = _rms(acc[:, hh * LANES:(hh + 1) * LANES]) * gain
                    parts.append(_rotate(a, cos, sin))
                store_att(g, seg, jnp.concatenate(parts, axis=1))
            else:
                store_att(g, seg, acc)


def _in_proj(x2, g_mix, w_in, tabs, q_gain, k_gain, seq, tm=512):
    T = x2.shape[0]
    npos = seq // tm
    tab_spec = pl.BlockSpec((tm, LANES), lambda i: (i % npos, 0))
    return pl.pallas_call(
        _inproj_kernel,
        grid=(T // tm,),
        in_specs=[
            pl.BlockSpec((tm, D_MODEL), lambda i: (i, 0)),
            _resident((1, D_MODEL)),
            _resident(w_in.shape),
            tab_spec, tab_spec, tab_spec, tab_spec,
            _resident(q_gain.shape), _resident(k_gain.shape),
        ],
        out_specs=[
            pl.BlockSpec((tm, W_RET), lambda i: (i, 0)),
            pl.BlockSpec((tm, W_GATE), lambda i: (i, 0)),
        ] + [
            pl.BlockSpec((None, dil, tm // dil, 3 * ATT_W), lambda i: (i // npos, 0, i % npos, 0))
            for _, dil in ATT_PATTERNS
        ],
        out_shape=[
            jax.ShapeDtypeStruct((T, W_RET), BF16),
            jax.ShapeDtypeStruct((T, W_GATE), BF16),
        ] + [
            jax.ShapeDtypeStruct((T // seq, dil, seq // dil, 3 * ATT_W), BF16)
            for _, dil in ATT_PATTERNS
        ],
        scratch_shapes=[pltpu.VMEM((ATT_HEADS, tm, ATT_HD), F32)],
        compiler_params=pltpu.CompilerParams(
            dimension_semantics=("parallel",), vmem_limit_bytes=VMEM_LIMIT),
        name="in_proj",
    )(x2, g_mix, w_in, *tabs, q_gain, k_gain)


def _retention_kernel(q_ref, k_ref, v_ref, rg_ref, gn_ref, inner_ref, qd_ref, kd_ref, cd_ref,
                      *rest, n_chunks):
    n_w = len(rest) // 2
    y_ref = rest[n_w]
    for w_in_ref, w_out_ref in zip(rest[:n_w], rest[n_w + 1:]):
        w_out_ref[...] = w_in_ref[...].astype(BF16)
    C = RET_CHUNK
    inner_decay, q_decay, k_decay, chunk_decay = inner_ref[...], qd_ref[...], kd_ref[...], cd_ref[...]
    gn = gn_ref[...]
    lhs, kvs = [], []
    for c in range(n_chunks):
        rows = slice(c * C, (c + 1) * C)
        q, k, v = q_ref[rows, :], k_ref[rows, :], v_ref[rows, :]
        scores = lax.dot_general(q, k, (((1,), (1,)), ((), ())), preferred_element_type=F32) * inner_decay
        qd = (q.astype(F32) * q_decay).astype(BF16)
        lhs.append(jnp.concatenate([scores.astype(BF16), qd], axis=1))
        kd = (k.astype(F32) * k_decay).astype(BF16)
        kvs.append(lax.dot_general(kd, v, (((0,), (0,)), ((), ())), preferred_element_type=F32))
    state = jnp.zeros((RET_DK, RET_DV), F32)
    for c in range(n_chunks):
        rows = slice(c * C, (c + 1) * C)
        rhs = jnp.concatenate([v_ref[rows, :], state.astype(BF16)], axis=0)
        y = jnp.dot(lhs[c], rhs, preferred_element_type=F32)
        state = state * chunk_decay + kvs[c]
        half_rg = rg_ref[rows, :].astype(F32)
        silu = half_rg * jnp.tanh(half_rg) + half_rg
        y_ref[rows, :] = (silu * (_rms(y) * gn)).astype(BF16)


def _retention(zr, zg, ret_gn, dec, weights):
    B, S, _ = zr.shape
    ts = S
    inner_decay, q_decay, k_decay, chunk_decay = dec
    head_tab = lambda shape: pl.BlockSpec((None,) + shape, lambda b, h, s: (h, 0, 0))
    n_steps = B * RET_HEADS
    w_specs = [pl.BlockSpec((w.shape[0] // n_steps, w.shape[1]), lambda b, h, s: (b * RET_HEADS + h, 0))
               for w in weights]
    yg, *w_bf16 = pl.pallas_call(
        functools.partial(_retention_kernel, n_chunks=ts // RET_CHUNK),
        grid=(B, RET_HEADS, S // ts),
        in_specs=[
            pl.BlockSpec((None, ts, RET_DK), lambda b, h, s: (b, s, h)),
            pl.BlockSpec((None, ts, RET_DK), lambda b, h, s: (b, s, RET_HEADS + h)),
            pl.BlockSpec((None, ts, RET_DV), lambda b, h, s: (b, s, 2 * RET_QK // RET_DV + h)),
            pl.BlockSpec((None, ts, RET_DV), lambda b, h, s: (b, s, h)),
            pl.BlockSpec((1, RET_DV), lambda b, h, s: (0, h)),
            head_tab((RET_CHUNK, RET_CHUNK)),
            head_tab((RET_CHUNK, RET_DK)),
            head_tab((RET_CHUNK, RET_DK)),
            head_tab((RET_DK, RET_DV)),
        ] + w_specs,
        out_specs=[pl.BlockSpec((None, ts, RET_DV), lambda b, h, s: (b, s, h))] + w_specs,
        out_shape=[jax.ShapeDtypeStruct((B, S, RET_V), BF16)]
        + [jax.ShapeDtypeStruct(w.shape, BF16) for w in weights],
        compiler_params=pltpu.CompilerParams(
            dimension_semantics=("parallel", "parallel", "arbitrary"), vmem_limit_bytes=VMEM_LIMIT),
        name="retention",
    )(zr, zr, zr, zg, ret_gn, inner_decay, q_decay, k_decay, chunk_decay, *weights)
    return yg, w_bf16


def _attn_unit(q, parts):
    nt = (((1,), (1,)), ((), ()))
    s = [lax.dot_general(q, k, nt, preferred_element_type=F32) + bias for k, _, bias in parts]
    s = s[0] if len(s) == 1 else jnp.concatenate(s, axis=1)
    m = jnp.max(s, axis=-1, keepdims=True)
    e = jnp.exp2(s - m).astype(BF16)
    ol, off = None, 0
    for _, v, _ in parts:
        n = v.shape[0]
        t = jnp.dot(e[:, off:off + n], jnp.concatenate([v, jnp.ones_like(v)], axis=1),
                    preferred_element_type=F32)
        ol = t if ol is None else ol + t
        off += n
    return ol[:, :ATT_HD], m, ol[:, ATT_HD:]


def _attention_kernel(q_ref, k_ref, v_ref, *rest, n_res, n_blk, has_prev):
    if has_prev:
        kp_ref, vp_ref, o_ref, st_ref = rest
    else:
        o_ref, st_ref = rest
    W = ATT_BLK
    neg = jnp.float32(-jnp.inf)
    row2 = lax.broadcasted_iota(jnp.int32, (W, 2 * W), 0)
    col2 = lax.broadcasted_iota(jnp.int32, (W, 2 * W), 1)
    band = jnp.where((col2 >= row2) & (col2 <= row2 + W), 0.0, neg)
    first = jnp.where(col2 <= row2, 0.0, neg)
    lane_half = lax.broadcasted_iota(jnp.int32, (W, LANES), 1) // STAT_LANES
    if has_prev:
        no_prev = W * (1 - jnp.minimum(pl.program_id(1), 1))
        prev_own = jnp.where((col2 >= row2) & (col2 <= row2 + W) & (col2 >= no_prev), 0.0, neg)

    for r in range(n_res):
        for b in range(n_blk):
            rows = slice(b * W, (b + 1) * W)
            st_tile = jnp.zeros((W, LANES), F32)
            for hh in range(ATT_HEADS):
                cols = slice(hh * ATT_HD, (hh + 1) * ATT_HD)
                q = q_ref[r, rows, cols]
                if b > 0:
                    keys = slice((b - 1) * W, (b + 1) * W)
                    parts = [(k_ref[r, keys, cols], v_ref[r, keys, cols], band)]
                elif has_prev:
                    parts = [(jnp.concatenate([kp_ref[0, :, cols], k_ref[r, rows, cols]], axis=0),
                              jnp.concatenate([vp_ref[0, :, cols], v_ref[r, rows, cols]], axis=0),
                              prev_own)]
                else:
                    keys = slice(0, 2 * W)
                    parts = [(k_ref[r, keys, cols], v_ref[r, keys, cols], first)]
                o, m, l = _attn_unit(q, parts)
                o_ref[r, rows, cols] = o.astype(BF16)
                st_tile = jnp.where(lane_half == 2 * hh, m, st_tile)
                st_tile = jnp.where(lane_half == 2 * hh + 1, l, st_tile)
            st_ref[r, rows, :] = st_tile


def _attention(za, g, rows_per_step=4096):
    window, dil = ATT_PATTERNS[g]
    assert window // dil == ATT_BLK
    batch, _, L, _ = za.shape
    has_prev = L > rows_per_step
    lc = min(L, rows_per_step)
    n_res = rows_per_step // lc
    n_blk = lc // ATT_BLK
    assert n_blk >= 2
    grid = (batch, L // lc if has_prev else dil // n_res)
    idx = (lambda b, j, seg: (b, 0, j, seg)) if has_prev else (lambda b, j, seg: (b, j, 0, seg))
    in_specs = [pl.BlockSpec((None, n_res, lc, ATT_W), functools.partial(idx, seg=seg))
                for seg in range(3)]
    operands = [za, za, za]
    if has_prev:
        in_specs += [pl.BlockSpec((None, 1, ATT_BLK, ATT_W),
                                  lambda b, j, seg=seg: (b, 0, jnp.maximum(j * n_blk - 1, 0), seg))
                     for seg in (1, 2)]
        operands += [za, za]
    out_specs = [pl.BlockSpec((None, n_res, lc, ATT_W), functools.partial(idx, seg=0)),
                 pl.BlockSpec((None, n_res, lc, LANES), functools.partial(idx, seg=0))]
    out_shape = [jax.ShapeDtypeStruct((batch, dil, L, ATT_W), BF16),
                 jax.ShapeDtypeStruct((batch, dil, L, LANES), F32)]
    return pl.pallas_call(
        functools.partial(_attention_kernel, n_res=n_res, n_blk=n_blk, has_prev=has_prev),
        grid=grid, in_specs=in_specs, out_specs=out_specs, out_shape=out_shape,
        compiler_params=pltpu.CompilerParams(
            dimension_semantics=("parallel", "parallel"), vmem_limit_bytes=VMEM_LIMIT),
        name=f"attention_g{g}",
    )(*operands)


def _merge_mlp_kernel(x_ref, y_ref, o0_ref, o1_ref, o2_ref, s0_ref, s1_ref, s2_ref, gr_ref, ga_ref,
                      bg_ref, wr_ref, wa_ref, wo_ref, p_ref, gm_ref, wu_ref, wd_ref, gp_ref, wpg_ref,
                      wpp_ref, out_ref, *perm_refs, ff_chunk):
    tm = x_ref.shape[0]

    def token_order(ref, scratch):
        dil = ref.shape[0]
        if dil == 1:
            return ref[0].astype(F32)
        n_slab = ref.shape[2] // LANES
        for r in range(dil):
            for c in range(n_slab):
                scratch[c, pl.ds(r, tm // dil, stride=dil), :] = (
                    ref[r, :, c * LANES:(c + 1) * LANES].astype(F32))
        return jnp.concatenate([scratch[c] for c in range(n_slab)], axis=1)

    o_tok = [token_order(o0_ref, None), token_order(o1_ref, perm_refs[0]),
             token_order(o2_ref, perm_refs[1])]
    stats = [token_order(s0_ref, None), token_order(s1_ref, perm_refs[2]),
             token_order(s2_ref, perm_refs[3])]
    sum_lane = (lax.broadcasted_iota(jnp.int32, (tm, LANES), 1) // STAT_LANES) % 2 == 1
    ms = stats
    lses = [st + pltpu.roll(jnp.log2(jnp.where(sum_lane, st, 1.0)), LANES - STAT_LANES, axis=1)
            for st in stats]
    top = jnp.maximum(jnp.maximum(lses[0], lses[1]), lses[2])
    den = jnp.exp2(lses[0] - top) + jnp.exp2(lses[1] - top) + jnp.exp2(lses[2] - top)
    wts = [jnp.exp2(m - top) / den for m in ms]
    per_head = LANES // ATT_HEADS
    parts = []
    for hh in range(ATT_HEADS):
        cols = slice(hh * ATT_HD, (hh + 1) * ATT_HD)
        acc = None
        for w, o_g in zip(wts, o_tok):
            t = w[:, hh * per_head:hh * per_head + 1] * o_g[:, cols]
            acc = t if acc is None else acc + t
        parts.append(acc)
    o = jnp.concatenate(parts, axis=1).astype(BF16)
    ret_branch = jnp.dot(y_ref[...], wr_ref[...], preferred_element_type=F32)
    att_branch = jnp.dot(o, wa_ref[...], preferred_element_type=F32)
    bg = bg_ref[...]
    gate_r = _sigmoid(gr_ref[...].astype(F32) + bg[:, :D_MODEL])
    gate_a = _sigmoid(ga_ref[...].astype(F32) + bg[:, D_MODEL:])
    mix = (gate_r * ret_branch + gate_a * att_branch).astype(BF16)
    x1 = x_ref[...] + jnp.dot(mix, wo_ref[...], preferred_element_type=F32)

    hn = (_rms(x1) * gm_ref[...]).astype(BF16)
    acc = x1
    for c in range(D_FF // ff_chunk):
        cols = slice(c * ff_chunk, (c + 1) * ff_chunk)
        u = jnp.dot(hn, wu_ref[:, cols], preferred_element_type=F32)
        a = jnp.square(jnp.maximum(u, 0.0)).astype(BF16)
        acc = acc + jnp.dot(a, wd_ref[cols, :], preferred_element_type=F32)
    hp = (_rms(acc) * gp_ref[...]).astype(BF16)
    gate_p = _sigmoid(jnp.dot(hp, wpg_ref[...], preferred_element_type=F32))
    proj = jnp.dot(p_ref[...].astype(BF16), wpp_ref[...], preferred_element_type=F32)
    out_ref[...] = acc + gate_p * proj


def _merge_mlp(x2, yg, os_, ss_, zg, b_gate, w_ret_out, w_att_out, w_o, p2, g_mlp, w_up, w_down,
               g_ple, w_pg, w_pp, tm=512, ff_chunk=1024):
    T = x2.shape[0]
    npos = os_[0].shape[1] * os_[0].shape[2] // tm
    row = lambda w: pl.BlockSpec((tm, w), lambda i: (i, 0))
    res_major = lambda a: pl.BlockSpec((None, a.shape[1], tm // a.shape[1], a.shape[3]),
                                       lambda i: (i // npos, 0, i % npos, 0))
    dilated = [a for a in os_ + ss_ if a.shape[1] > 1]
    return pl.pallas_call(
        functools.partial(_merge_mlp_kernel, ff_chunk=ff_chunk),
        grid=(T // tm,),
        in_specs=[
            row(D_MODEL), row(RET_V), *[res_major(a) for a in os_ + ss_],
            pl.BlockSpec((tm, D_MODEL), lambda i: (i, RET_V // D_MODEL)),
            pl.BlockSpec((tm, D_MODEL), lambda i: (i, RET_V // D_MODEL + 1)),
            _resident(b_gate.shape), _resident(w_ret_out.shape), _resident(w_att_out.shape),
            _resident(w_o.shape),
            row(PLE_DIM), _resident((1, D_MODEL)), _resident(w_up.shape), _resident(w_down.shape),
            _resident((1, D_MODEL)), _resident(w_pg.shape), _resident(w_pp.shape),
        ],
        out_specs=row(D_MODEL),
        out_shape=jax.ShapeDtypeStruct((T, D_MODEL), F32),
        scratch_shapes=[pltpu.VMEM((a.shape[3] // LANES, tm, LANES), F32) for a in dilated],
        compiler_params=pltpu.CompilerParams(
            dimension_semantics=("parallel",), vmem_limit_bytes=VMEM_LIMIT),
        name="merge_mlp",
    )(x2, yg, *os_, *ss_, zg, zg, b_gate, w_ret_out, w_att_out, w_o, p2, g_mlp, w_up, w_down, g_ple,
      w_pg, w_pp)


def _rot_tables(inv_freq, seq):
    ang = np.arange(seq, dtype=np.float64)[:, None] * inv_freq[None, :]
    cos, sin = np.cos(ang), np.sin(ang)
    return (np.concatenate([cos, cos], axis=1).astype(np.float32),
            np.concatenate([-sin, sin], axis=1).astype(np.float32))


def _decay_tables():
    H, C = RET_HEADS, RET_CHUNK
    log_g = np.log1p(-np.exp2(-5.0 - np.arange(H, dtype=np.float64)))
    idx = np.arange(C, dtype=np.float64)
    diff = idx[:, None] - idx[None, :]
    inner = np.where(diff >= 0, np.exp(log_g[:, None, None] * np.maximum(diff, 0.0)), 0.0)
    q_decay = np.exp(log_g[:, None] * (idx[None, :] + 1.0))
    k_decay = np.exp(log_g[:, None] * (C - 1.0 - idx[None, :]))
    chunk_decay = np.exp(log_g * C)
    f32 = lambda a: np.ascontiguousarray(a, dtype=np.float32)
    return (f32(inner),
            f32(np.broadcast_to(q_decay[:, :, None], (H, C, RET_DK))),
            f32(np.broadcast_to(k_decay[:, :, None], (H, C, RET_DK))),
            f32(np.broadcast_to(chunk_decay[:, None, None], (H, RET_DK, RET_DV))))


def _layer(x, p_i, w_in, b_gate, g_mix, q_gain, k_gain, ret_gn, w_ret_out, w_att_out, w_o,
           g_mlp, w_up, w_down, g_ple, w_ple_proj, w_ple_gate):
    B, S, D = x.shape
    T = B * S
    x2 = x.reshape(T, D)
    ret_freq = 1.0 / (10000.0 ** np.linspace(0.0, 1.0, RET_DK // 2))
    rope_freq = ROPE_THETA ** (-np.arange(0, ATT_HD, 2, dtype=np.float64) / ATT_HD)
    tabs = _rot_tables(ret_freq, S) + _rot_tables(rope_freq, S)

    col_scale = jnp.ones((w_in.shape[1],), F32).at[W_RET:W_RET + RET_V].set(0.5)
    zr, zg, *zas = _in_proj(x2, g_mix.reshape(1, D), (w_in * col_scale).astype(BF16), tabs,
                            q_gain, k_gain, S)
    yg, (w_ret_out, w_att_out, w_o, w_up, w_down, w_ple_gate, w_ple_proj) = _retention(
        zr.reshape(B, S, W_RET), zg.reshape(B, S, W_GATE), ret_gn.reshape(1, RET_V), _decay_tables(),
        [w_ret_out, w_att_out, w_o, w_up, w_down, w_ple_gate, w_ple_proj])
    att = [_attention(zas[g], g) for g in range(N_GROUPS)]
    out = _merge_mlp(x2, yg.reshape(T, RET_V), [o for o, _ in att], [s for _, s in att], zg,
                     b_gate.reshape(1, -1), w_ret_out, w_att_out, w_o, p_i.reshape(T, PLE_DIM),
                     g_mlp.reshape(1, D), w_up, w_down, g_ple.reshape(1, D), w_ple_gate, w_ple_proj)
    return out.reshape(B, S, D)


def kernel(x, p, w_in, b_gate, g_mix, q_gain, k_gain, ret_gn, w_ret_out, w_att_out, w_o, g_mlp, w_up,
           w_down, g_ple, w_ple_proj, w_ple_gate):
    for i in range(p.shape[0]):
        x = _layer(x, p[i], w_in[i], b_gate[i], g_mix[i], q_gain[i], k_gain[i], ret_gn[i],
                   w_ret_out[i], w_att_out[i], w_o[i], g_mlp[i], w_up[i], w_down[i], g_ple[i],
                   w_ple_proj[i], w_ple_gate[i])
    return x
```

```python
import functools

import jax
import jax.numpy as jnp
import numpy as np
from jax import lax
from jax.experimental import pallas as pl
from jax.experimental.pallas import tpu as pltpu

F32 = jnp.float32
BF16 = jnp.bfloat16

D_MODEL = 1024
PLE_DIM = 256
D_FF = 4 * D_MODEL
EPS = 1e-6
RET_HEADS = 4
RET_DK = 128
RET_DV = 256
RET_CHUNK = 128
RET_QK = RET_HEADS * RET_DK
RET_V = RET_HEADS * RET_DV
ATT_PATTERNS = ((128, 1), (512, 4), (2048, 16))
N_GROUPS = len(ATT_PATTERNS)
ATT_HEADS = 4
ATT_HD = 128
ATT_W = ATT_HEADS * ATT_HD
ATT_BLK = 128
ROPE_THETA = 10000.0
LOG2E = 1.4426950408889634
ATT_QSCALE = ATT_HD ** -0.5 * LOG2E

LANES = 128
STAT_LANES = LANES // (2 * ATT_HEADS)
SEG = 512
W_RET = 2 * RET_QK + RET_V
W_GATE = RET_V + 2 * D_MODEL
W_ATT = 3 * N_GROUPS * ATT_W
VMEM_LIMIT = 60 * 1024 * 1024
BF16_ROWS = 16


def _resident(shape):
    return pl.BlockSpec(shape, lambda *_: (0,) * len(shape), pipeline_mode=pl.Buffered(1))


def _rms(x):
    return x * lax.rsqrt(jnp.mean(x * x, axis=-1, keepdims=True) + EPS)


def _sigmoid(x):
    return 0.5 * jnp.tanh(0.5 * x) + 0.5


def _rotate(a, cos, sin_signed):
    return a * cos + pltpu.roll(a, LANES // 2, axis=1) * sin_signed


def _inproj_kernel(x_ref, g_ref, w_ref, cr_ref, sr_ref, ca_ref, sa_ref, qg_ref, kg_ref, *rest):
    n_w = (len(rest) - 6) // 2
    zr_ref, zg_ref, *za_refs = rest[n_w:n_w + 5]
    perm_ref = rest[-1]
    for w_f32_ref, w_bf16_ref in zip(rest[:n_w], rest[n_w + 5:-1]):
        w_bf16_ref[...] = w_f32_ref[...].astype(BF16)
    tm = x_ref.shape[0]

    def store_att(g, seg, val):
        cols = slice(seg * ATT_W, (seg + 1) * ATT_W)
        dil = ATT_PATTERNS[g][1]
        if dil == 1:
            za_refs[g][0, :, cols] = val.astype(BF16)
            return
        for hh in range(ATT_HEADS):
            perm_ref[hh] = val[:, hh * ATT_HD:(hh + 1) * ATT_HD]
        for r in range(dil):
            for hh in range(ATT_HEADS):
                c0 = seg * ATT_W + hh * ATT_HD
                za_refs[g][r, :, c0:c0 + ATT_HD] = (
                    perm_ref[hh, pl.ds(r, tm // dil, stride=dil), :].astype(BF16))

    h = (_rms(x_ref[...]) * g_ref[...]).astype(BF16)
    n_ret, n_rg, n_att = W_RET // SEG, RET_V // SEG, W_ATT // SEG
    n_all = (W_RET + W_GATE + W_ATT) // SEG
    heavy = [0, 1] + list(range(n_ret + n_rg, n_ret + n_rg + n_att))
    for j in heavy + [j for j in range(n_all) if j not in heavy]:
        acc = jnp.dot(h, w_ref[:, j * SEG:(j + 1) * SEG], preferred_element_type=F32)
        if j < 2:
            cos, sin = cr_ref[...], sr_ref[...]
            for hh in range(SEG // LANES):
                r = _rotate(acc[:, hh * LANES:(hh + 1) * LANES], cos, sin)
                if j == 1:
                    r = r * (RET_DK ** -0.5)
                zr_ref[:, j * SEG + hh * LANES:j * SEG + (hh + 1) * LANES] = r.astype(BF16)
        elif j < n_ret:
            zr_ref[:, j * SEG:(j + 1) * SEG] = acc.astype(BF16)
        elif j < n_ret + n_rg:
            jj = j - n_ret
            zg_ref[:, jj * SEG:(jj + 1) * SEG] = acc.astype(BF16)
        elif j >= n_ret + n_rg + n_att:
            jj = j - n_ret - n_att
            zg_ref[:, jj * SEG:(jj + 1) * SEG] = acc.astype(BF16)
        else:
            jj = j - n_ret - n_rg
            seg, g = jj // N_GROUPS, jj % N_GROUPS
            if seg < 2:
                gain = (qg_ref if seg == 0 else kg_ref)[g:g + 1, :]
                if seg == 0:
                    gain = gain * ATT_QSCALE
                cos, sin = ca_ref[...], sa_ref[...]
                parts = []
                for hh in range(SEG // LANES):
                    a = _rms(acc[:, hh * LANES:(hh + 1) * LANES]) * gain
                    parts.append(_rotate(a, cos, sin))
                store_att(g, seg, jnp.concatenate(parts, axis=1))
            else:
                store_att(g, seg, acc)


def _in_proj(x2, g_mix, w_in, tabs, q_gain, k_gain, seq, weights, tm=512):
    T = x2.shape[0]
    npos = seq // tm
    tab_spec = pl.BlockSpec((tm, LANES), lambda i: (i % npos, 0))

    def slice_spec(w):
        rows = max(w.shape[0] // (T // tm), BF16_ROWS)
        reps = rows * (T // tm) // w.shape[0]
        return pl.BlockSpec((rows, w.shape[1]), lambda i: (i // reps, 0))

    w_specs = [slice_spec(w) for w in weights]
    return pl.pallas_call(
        _inproj_kernel,
        grid=(T // tm,),
        in_specs=[
            pl.BlockSpec((tm, D_MODEL), lambda i: (i, 0)),
            _resident((1, D_MODEL)),
            _resident(w_in.shape),
            tab_spec, tab_spec, tab_spec, tab_spec,
            _resident(q_gain.shape), _resident(k_gain.shape),
        ] + w_specs,
        out_specs=[
            pl.BlockSpec((tm, W_RET), lambda i: (i, 0)),
            pl.BlockSpec((tm, W_GATE), lambda i: (i, 0)),
        ] + [
            pl.BlockSpec((None, dil, tm // dil, 3 * ATT_W), lambda i: (i // npos, 0, i % npos, 0))
            for _, dil in ATT_PATTERNS
        ] + w_specs,
        out_shape=[
            jax.ShapeDtypeStruct((T, W_RET), BF16),
            jax.ShapeDtypeStruct((T, W_GATE), BF16),
        ] + [
            jax.ShapeDtypeStruct((T // seq, dil, seq // dil, 3 * ATT_W), BF16)
            for _, dil in ATT_PATTERNS
        ] + [jax.ShapeDtypeStruct(w.shape, BF16) for w in weights],
        scratch_shapes=[pltpu.VMEM((ATT_HEADS, tm, ATT_HD), F32)],
        compiler_params=pltpu.CompilerParams(
            dimension_semantics=("parallel",), vmem_limit_bytes=VMEM_LIMIT),
        name="in_proj",
    )(x2, g_mix, w_in, *tabs, q_gain, k_gain, *weights)


def _retention_kernel(q_ref, k_ref, v_ref, rg_ref, gn_ref, inner_ref, qd_ref, kd_ref, cd_ref,
                      y_ref, *, n_chunks):
    C = RET_CHUNK
    inner_decay, q_decay, k_decay, chunk_decay = inner_ref[...], qd_ref[...], kd_ref[...], cd_ref[...]
    gn = gn_ref[...]
    lhs, kvs = [], []
    for c in range(n_chunks):
        rows = slice(c * C, (c + 1) * C)
        q, k, v = q_ref[rows, :], k_ref[rows, :], v_ref[rows, :]
        scores = lax.dot_general(q, k, (((1,), (1,)), ((), ())), preferred_element_type=F32) * inner_decay
        qd = (q.astype(F32) * q_decay).astype(BF16)
        lhs.append(jnp.concatenate([scores.astype(BF16), qd], axis=1))
        kd = (k.astype(F32) * k_decay).astype(BF16)
        kvs.append(lax.dot_general(kd, v, (((0,), (0,)), ((), ())), preferred_element_type=F32))
    state = jnp.zeros((RET_DK, RET_DV), F32)
    for c in range(n_chunks):
        rows = slice(c * C, (c + 1) * C)
        rhs = jnp.concatenate([v_ref[rows, :], state.astype(BF16)], axis=0)
        y = jnp.dot(lhs[c], rhs, preferred_element_type=F32)
        state = state * chunk_decay + kvs[c]
        half_rg = rg_ref[rows, :].astype(F32)
        silu = half_rg * jnp.tanh(half_rg) + half_rg
        y_ref[rows, :] = (silu * (_rms(y) * gn)).astype(BF16)


def _retention(zr, zg, ret_gn, dec):
    B, S, _ = zr.shape
    ts = S
    inner_decay, q_decay, k_decay, chunk_decay = dec
    head_tab = lambda shape: pl.BlockSpec((None,) + shape, lambda b, h, s: (h, 0, 0))
    return pl.pallas_call(
        functools.partial(_retention_kernel, n_chunks=ts // RET_CHUNK),
        grid=(B, RET_HEADS, S // ts),
        in_specs=[
            pl.BlockSpec((None, ts, RET_DK), lambda b, h, s: (b, s, h)),
            pl.BlockSpec((None, ts, RET_DK), lambda b, h, s: (b, s, RET_HEADS + h)),
            pl.BlockSpec((None, ts, RET_DV), lambda b, h, s: (b, s, 2 * RET_QK // RET_DV + h)),
            pl.BlockSpec((None, ts, RET_DV), lambda b, h, s: (b, s, h)),
            pl.BlockSpec((1, RET_DV), lambda b, h, s: (0, h)),
            head_tab((RET_CHUNK, RET_CHUNK)),
            head_tab((RET_CHUNK, RET_DK)),
            head_tab((RET_CHUNK, RET_DK)),
            head_tab((RET_DK, RET_DV)),
        ],
        out_specs=pl.BlockSpec((None, ts, RET_DV), lambda b, h, s: (b, s, h)),
        out_shape=jax.ShapeDtypeStruct((B, S, RET_V), BF16),
        compiler_params=pltpu.CompilerParams(
            dimension_semantics=("parallel", "parallel", "arbitrary"), vmem_limit_bytes=VMEM_LIMIT),
        name="retention",
    )(zr, zr, zr, zg, ret_gn, inner_decay, q_decay, k_decay, chunk_decay)


def _attn_unit(q, parts):
    nt = (((1,), (1,)), ((), ()))
    s = [lax.dot_general(q, k, nt, preferred_element_type=F32) + bias for k, _, bias in parts]
    s = s[0] if len(s) == 1 else jnp.concatenate(s, axis=1)
    m = jnp.max(s, axis=-1, keepdims=True)
    e = jnp.exp2(s - m).astype(BF16)
    ol, off = None, 0
    for _, v, _ in parts:
        n = v.shape[0]
        t = jnp.dot(e[:, off:off + n], jnp.concatenate([v, jnp.ones_like(v)], axis=1),
                    preferred_element_type=F32)
        ol = t if ol is None else ol + t
        off += n
    return ol[:, :ATT_HD], m, ol[:, ATT_HD:]


def _attention_kernel(q_ref, k_ref, v_ref, *rest, n_res, n_blk, has_prev):
    if has_prev:
        kp_ref, vp_ref, o_ref, st_ref = rest
    else:
        o_ref, st_ref = rest
    W = ATT_BLK
    neg = jnp.float32(-jnp.inf)
    row2 = lax.broadcasted_iota(jnp.int32, (W, 2 * W), 0)
    col2 = lax.broadcasted_iota(jnp.int32, (W, 2 * W), 1)
    band = jnp.where((col2 >= row2) & (col2 <= row2 + W), 0.0, neg)
    first = jnp.where(col2 <= row2, 0.0, neg)
    lane_half = lax.broadcasted_iota(jnp.int32, (W, LANES), 1) // STAT_LANES
    if has_prev:
        no_prev = W * (1 - jnp.minimum(pl.program_id(1), 1))
        prev_own = jnp.where((col2 >= row2) & (col2 <= row2 + W) & (col2 >= no_prev), 0.0, neg)

    for r in range(n_res):
        for b in range(n_blk):
            rows = slice(b * W, (b + 1) * W)
            st_tile = jnp.zeros((W, LANES), F32)
            for hh in range(ATT_HEADS):
                cols = slice(hh * ATT_HD, (hh + 1) * ATT_HD)
                q = q_ref[r, rows, cols]
                if b > 0:
                    keys = slice((b - 1) * W, (b + 1) * W)
                    parts = [(k_ref[r, keys, cols], v_ref[r, keys, cols], band)]
                elif has_prev:
                    parts = [(jnp.concatenate([kp_ref[0, :, cols], k_ref[r, rows, cols]], axis=0),
                              jnp.concatenate([vp_ref[0, :, cols], v_ref[r, rows, cols]], axis=0),
                              prev_own)]
                else:
                    keys = slice(0, 2 * W)
                    parts = [(k_ref[r, keys, cols], v_ref[r, keys, cols], first)]
                o, m, l = _attn_unit(q, parts)
                o_ref[r, rows, cols] = o.astype(BF16)
                st_tile = jnp.where(lane_half == 2 * hh, m, st_tile)
                st_tile = jnp.where(lane_half == 2 * hh + 1, l, st_tile)
            st_ref[r, rows, :] = st_tile


def _attention(za, g, rows_per_step=4096):
    window, dil = ATT_PATTERNS[g]
    assert window // dil == ATT_BLK
    batch, _, L, _ = za.shape
    has_prev = L > rows_per_step
    lc = min(L, rows_per_step)
    n_res = rows_per_step // lc
    n_blk = lc // ATT_BLK
    assert n_blk >= 2
    grid = (batch, L // lc if has_prev else dil // n_res)
    idx = (lambda b, j, seg: (b, 0, j, seg)) if has_prev else (lambda b, j, seg: (b, j, 0, seg))
    in_specs = [pl.BlockSpec((None, n_res, lc, ATT_W), functools.partial(idx, seg=seg))
                for seg in range(3)]
    operands = [za, za, za]
    if has_prev:
        in_specs += [pl.BlockSpec((None, 1, ATT_BLK, ATT_W),
                                  lambda b, j, seg=seg: (b, 0, jnp.maximum(j * n_blk - 1, 0), seg))
                     for seg in (1, 2)]
        operands += [za, za]
    out_specs = [pl.BlockSpec((None, n_res, lc, ATT_W), functools.partial(idx, seg=0)),
                 pl.BlockSpec((None, n_res, lc, LANES), functools.partial(idx, seg=0))]
    out_shape = [jax.ShapeDtypeStruct((batch, dil, L, ATT_W), BF16),
                 jax.ShapeDtypeStruct((batch, dil, L, LANES), F32)]
    return pl.pallas_call(
        functools.partial(_attention_kernel, n_res=n_res, n_blk=n_blk, has_prev=has_prev),
        grid=grid, in_specs=in_specs, out_specs=out_specs, out_shape=out_shape,
        compiler_params=pltpu.CompilerParams(
            dimension_semantics=("parallel", "parallel"), vmem_limit_bytes=VMEM_LIMIT),
        name=f"attention_g{g}",
    )(*operands)


def _merge_mlp_kernel(x_ref, y_ref, o0_ref, o1_ref, o2_ref, s0_ref, s1_ref, s2_ref, gr_ref, ga_ref,
                      bg_ref, wr_ref, wa_ref, wo_ref, p_ref, gm_ref, wu_ref, wd_ref, gp_ref, wpg_ref,
                      wpp_ref, out_ref, *perm_refs, ff_chunk):
    tm = x_ref.shape[0]

    def token_order(ref, scratch):
        dil = ref.shape[0]
        if dil == 1:
            return ref[0].astype(F32)
        n_slab = ref.shape[2] // LANES
        for r in range(dil):
            for c in range(n_slab):
                scratch[c, pl.ds(r, tm // dil, stride=dil), :] = (
                    ref[r, :, c * LANES:(c + 1) * LANES].astype(F32))
        return jnp.concatenate([scratch[c] for c in range(n_slab)], axis=1)

    o_tok = [token_order(o0_ref, None), token_order(o1_ref, perm_refs[0]),
             token_order(o2_ref, perm_refs[1])]
    stats = [token_order(s0_ref, None), token_order(s1_ref, perm_refs[2]),
             token_order(s2_ref, perm_refs[3])]
    sum_lane = (lax.broadcasted_iota(jnp.int32, (tm, LANES), 1) // STAT_LANES) % 2 == 1
    ms = stats
    lses = [st + pltpu.roll(jnp.log2(jnp.where(sum_lane, st, 1.0)), LANES - STAT_LANES, axis=1)
            for st in stats]
    top = jnp.maximum(jnp.maximum(lses[0], lses[1]), lses[2])
    den = jnp.exp2(lses[0] - top) + jnp.exp2(lses[1] - top) + jnp.exp2(lses[2] - top)
    wts = [jnp.exp2(m - top) / den for m in ms]
    per_head = LANES // ATT_HEADS
    parts = []
    for hh in range(ATT_HEADS):
        cols = slice(hh * ATT_HD, (hh + 1) * ATT_HD)
        acc = None
        for w, o_g in zip(wts, o_tok):
            t = w[:, hh * per_head:hh * per_head + 1] * o_g[:, cols]
            acc = t if acc is None else acc + t
        parts.append(acc)
    o = jnp.concatenate(parts, axis=1).astype(BF16)
    ret_branch = jnp.dot(y_ref[...], wr_ref[...], preferred_element_type=F32)
    att_branch = jnp.dot(o, wa_ref[...], preferred_element_type=F32)
    bg = bg_ref[...]
    gate_r = _sigmoid(gr_ref[...].astype(F32) + bg[:, :D_MODEL])
    gate_a = _sigmoid(ga_ref[...].astype(F32) + bg[:, D_MODEL:])
    mix = (gate_r * ret_branch + gate_a * att_branch).astype(BF16)
    x1 = x_ref[...] + jnp.dot(mix, wo_ref[...], preferred_element_type=F32)

    hn = (_rms(x1) * gm_ref[...]).astype(BF16)
    acc = x1
    for c in range(D_FF // ff_chunk):
        cols = slice(c * ff_chunk, (c + 1) * ff_chunk)
        u = jnp.dot(hn, wu_ref[:, cols], preferred_element_type=F32)
        a = jnp.square(jnp.maximum(u, 0.0)).astype(BF16)
        acc = acc + jnp.dot(a, wd_ref[cols, :], preferred_element_type=F32)
    hp = (_rms(acc) * gp_ref[...]).astype(BF16)
    gate_p = _sigmoid(jnp.dot(hp, wpg_ref[...], preferred_element_type=F32))
    proj = jnp.dot(p_ref[...].astype(BF16), wpp_ref[...], preferred_element_type=F32)
    out_ref[...] = acc + gate_p * proj


def _merge_mlp(x2, yg, os_, ss_, zg, b_gate, w_ret_out, w_att_out, w_o, p2, g_mlp, w_up, w_down,
               g_ple, w_pg, w_pp, tm=512, ff_chunk=1024):
    T = x2.shape[0]
    npos = os_[0].shape[1] * os_[0].shape[2] // tm
    row = lambda w: pl.BlockSpec((tm, w), lambda i: (i, 0))
    res_major = lambda a: pl.BlockSpec((None, a.shape[1], tm // a.shape[1], a.shape[3]),
                                       lambda i: (i // npos, 0, i % npos, 0))
    dilated = [a for a in os_ + ss_ if a.shape[1] > 1]
    return pl.pallas_call(
        functools.partial(_merge_mlp_kernel, ff_chunk=ff_chunk),
        grid=(T // tm,),
        in_specs=[
            row(D_MODEL), row(RET_V), *[res_major(a) for a in os_ + ss_],
            pl.BlockSpec((tm, D_MODEL), lambda i: (i, RET_V // D_MODEL)),
            pl.BlockSpec((tm, D_MODEL), lambda i: (i, RET_V // D_MODEL + 1)),
            _resident(b_gate.shape), _resident(w_ret_out.shape), _resident(w_att_out.shape),
            _resident(w_o.shape),
            row(PLE_DIM), _resident((1, D_MODEL)), _resident(w_up.shape), _resident(w_down.shape),
            _resident((1, D_MODEL)), _resident(w_pg.shape), _resident(w_pp.shape),
        ],
        out_specs=row(D_MODEL),
        out_shape=jax.ShapeDtypeStruct((T, D_MODEL), F32),
        scratch_shapes=[pltpu.VMEM((a.shape[3] // LANES, tm, LANES), F32) for a in dilated],
        compiler_params=pltpu.CompilerParams(
            dimension_semantics=("parallel",), vmem_limit_bytes=VMEM_LIMIT),
        name="merge_mlp",
    )(x2, yg, *os_, *ss_, zg, zg, b_gate, w_ret_out, w_att_out, w_o, p2, g_mlp, w_up, w_down, g_ple,
      w_pg, w_pp)


def _rot_tables(inv_freq, seq):
    ang = np.arange(seq, dtype=np.float64)[:, None] * inv_freq[None, :]
    cos, sin = np.cos(ang), np.sin(ang)
    return (np.concatenate([cos, cos], axis=1).astype(np.float32),
            np.concatenate([-sin, sin], axis=1).astype(np.float32))


def _decay_tables():
    H, C = RET_HEADS, RET_CHUNK
    log_g = np.log1p(-np.exp2(-5.0 - np.arange(H, dtype=np.float64)))
    idx = np.arange(C, dtype=np.float64)
    diff = idx[:, None] - idx[None, :]
    inner = np.where(diff >= 0, np.exp(log_g[:, None, None] * np.maximum(diff, 0.0)), 0.0)
    q_decay = np.exp(log_g[:, None] * (idx[None, :] + 1.0))
    k_decay = np.exp(log_g[:, None] * (C - 1.0 - idx[None, :]))
    chunk_decay = np.exp(log_g * C)
    f32 = lambda a: np.ascontiguousarray(a, dtype=np.float32)
    return (f32(inner),
            f32(np.broadcast_to(q_decay[:, :, None], (H, C, RET_DK))),
            f32(np.broadcast_to(k_decay[:, :, None], (H, C, RET_DK))),
            f32(np.broadcast_to(chunk_decay[:, None, None], (H, RET_DK, RET_DV))))


def _layer(x, p_i, w_in, b_gate, g_mix, q_gain, k_gain, ret_gn, w_ret_out, w_att_out, w_o,
           g_mlp, w_up, w_down, g_ple, w_ple_proj, w_ple_gate):
    B, S, D = x.shape
    T = B * S
    x2 = x.reshape(T, D)
    ret_freq = 1.0 / (10000.0 ** np.linspace(0.0, 1.0, RET_DK // 2))
    rope_freq = ROPE_THETA ** (-np.arange(0, ATT_HD, 2, dtype=np.float64) / ATT_HD)
    tabs = _rot_tables(ret_freq, S) + _rot_tables(rope_freq, S)

    col_scale = jnp.ones((w_in.shape[1],), F32).at[W_RET:W_RET + RET_V].set(0.5)
    zr, zg, za0, za1, za2, w_ret_out, w_att_out, w_o, w_up, w_down, w_ple_gate, w_ple_proj = _in_proj(
        x2, g_mix.reshape(1, D), (w_in * col_scale).astype(BF16), tabs, q_gain, k_gain, S,
        [w_ret_out, w_att_out, w_o, w_up, w_down, w_ple_gate, w_ple_proj])
    zas = (za0, za1, za2)
    yg = _retention(zr.reshape(B, S, W_RET), zg.reshape(B, S, W_GATE), ret_gn.reshape(1, RET_V),
                    _decay_tables())
    att = [_attention(zas[g], g) for g in range(N_GROUPS)]
    out = _merge_mlp(x2, yg.reshape(T, RET_V), [o for o, _ in att], [s for _, s in att], zg,
                     b_gate.reshape(1, -1), w_ret_out, w_att_out, w_o, p_i.reshape(T, PLE_DIM),
                     g_mlp.reshape(1, D), w_up, w_down, g_ple.reshape(1, D), w_ple_gate, w_ple_proj)
    return out.reshape(B, S, D)


def kernel(x, p, w_in, b_gate, g_mix, q_gain, k_gain, ret_gn, w_ret_out, w_att_out, w_o, g_mlp, w_up,
           w_down, g_ple, w_ple_proj, w_ple_gate):
    for i in range(p.shape[0]):
        x = _layer(x, p[i], w_in[i], b_gate[i], g_mix[i], q_gain[i], k_gain[i], ret_gn[i],
                   w_ret_out[i], w_att_out[i], w_o[i], g_mlp[i], w_up[i], w_down[i], g_ple[i],
                   w_ple_proj[i], w_ple_gate[i])
    return x
```

```python
import functools

import jax
import jax.numpy as jnp
import numpy as np
from jax import lax
from jax.experimental import pallas as pl
from jax.experimental.pallas import tpu as pltpu

F32 = jnp.float32
BF16 = jnp.bfloat16

D_MODEL = 1024
PLE_DIM = 256
D_FF = 4 * D_MODEL
EPS = 1e-6
RET_HEADS = 4
RET_DK = 128
RET_DV = 256
RET_CHUNK = 128
RET_QK = RET_HEADS * RET_DK
RET_V = RET_HEADS * RET_DV
ATT_PATTERNS = ((128, 1), (512, 4), (2048, 16))
N_GROUPS = len(ATT_PATTERNS)
ATT_HEADS = 4
ATT_HD = 128
ATT_W = ATT_HEADS * ATT_HD
ATT_BLK = 128
ROPE_THETA = 10000.0
LOG2E = 1.4426950408889634
ATT_QSCALE = ATT_HD ** -0.5 * LOG2E

LANES = 128
STAT_LANES = LANES // (2 * ATT_HEADS)
SEG = 512
W_RET = 2 * RET_QK + RET_V
W_GATE = RET_V + 2 * D_MODEL
W_ATT = 3 * N_GROUPS * ATT_W
VMEM_LIMIT = 60 * 1024 * 1024
BF16_ROWS = 16


def _resident(shape):
    return pl.BlockSpec(shape, lambda *_: (0,) * len(shape), pipeline_mode=pl.Buffered(1))


def _rms(x):
    return x * lax.rsqrt(jnp.mean(x * x, axis=-1, keepdims=True) + EPS)


def _sigmoid(x):
    return 0.5 * jnp.tanh(0.5 * x) + 0.5


def _rotate(a, cos, sin_signed):
    return a * cos + pltpu.roll(a, LANES // 2, axis=1) * sin_signed


def _inproj_kernel(x_ref, g_ref, w_ref, cr_ref, sr_ref, ca_ref, sa_ref, qg_ref, kg_ref, *rest):
    n_w = (len(rest) - 6) // 2
    zr_ref, zg_ref, *za_refs = rest[n_w:n_w + 5]
    perm_ref = rest[-1]
    for w_f32_ref, w_bf16_ref in zip(rest[:n_w], rest[n_w + 5:-1]):
        w_bf16_ref[...] = w_f32_ref[...].astype(BF16)
    tm = x_ref.shape[0]
    n_slab = D_MODEL // LANES

    def residue_major(load, dil):
        return jnp.concatenate([load(pl.ds(r, tm // dil, stride=dil)) for r in range(dil)], axis=0)

    xn = _rms(x_ref[...]) * g_ref[...]
    for c in range(n_slab):
        perm_ref[c] = xn[:, c * LANES:(c + 1) * LANES]
    lhs, cos_a, sin_a = {}, {}, {}
    for dil in sorted({d for _, d in ATT_PATTERNS}):
        if dil == 1:
            lhs[dil], cos_a[dil], sin_a[dil] = xn.astype(BF16), ca_ref[...], sa_ref[...]
            continue
        lhs[dil] = jnp.concatenate(
            [residue_major(lambda rows, c=c: perm_ref[c, rows, :], dil) for c in range(n_slab)],
            axis=1).astype(BF16)
        cos_a[dil] = residue_major(lambda rows: ca_ref[rows, :], dil)
        sin_a[dil] = residue_major(lambda rows: sa_ref[rows, :], dil)
    h = lhs[1]

    def store_att(g, seg, val):
        dil = ATT_PATTERNS[g][1]
        za_refs[g][:, :, seg * ATT_W:(seg + 1) * ATT_W] = (
            val.astype(BF16).reshape(dil, tm // dil, ATT_W))

    n_ret, n_rg, n_att = W_RET // SEG, RET_V // SEG, W_ATT // SEG
    n_all = (W_RET + W_GATE + W_ATT) // SEG
    heavy = [0, 1] + list(range(n_ret + n_rg, n_ret + n_rg + n_att))
    for j in heavy + [j for j in range(n_all) if j not in heavy]:
        w_chunk = w_ref[:, j * SEG:(j + 1) * SEG]
        if n_ret + n_rg <= j < n_ret + n_rg + n_att:
            jj = j - n_ret - n_rg
            seg, g = jj // N_GROUPS, jj % N_GROUPS
            dil = ATT_PATTERNS[g][1]
            acc = jnp.dot(lhs[dil], w_chunk, preferred_element_type=F32)
            if seg < 2:
                gain = (qg_ref if seg == 0 else kg_ref)[g:g + 1, :]
                if seg == 0:
                    gain = gain * ATT_QSCALE
                parts = []
                for hh in range(SEG // LANES):
                    a = _rms(acc[:, hh * LANES:(hh + 1) * LANES]) * gain
                    parts.append(_rotate(a, cos_a[dil], sin_a[dil]))
                store_att(g, seg, jnp.concatenate(parts, axis=1))
            else:
                store_att(g, seg, acc)
            continue
        acc = jnp.dot(h, w_chunk, preferred_element_type=F32)
        if j < 2:
            cos, sin = cr_ref[...], sr_ref[...]
            for hh in range(SEG // LANES):
                r = _rotate(acc[:, hh * LANES:(hh + 1) * LANES], cos, sin)
                if j == 1:
                    r = r * (RET_DK ** -0.5)
                zr_ref[:, j * SEG + hh * LANES:j * SEG + (hh + 1) * LANES] = r.astype(BF16)
        elif j < n_ret:
            zr_ref[:, j * SEG:(j + 1) * SEG] = acc.astype(BF16)
        elif j < n_ret + n_rg:
            jj = j - n_ret
            zg_ref[:, jj * SEG:(jj + 1) * SEG] = acc.astype(BF16)
        else:
            jj = j - n_ret - n_att
            zg_ref[:, jj * SEG:(jj + 1) * SEG] = acc.astype(BF16)


def _in_proj(x2, g_mix, w_in, tabs, q_gain, k_gain, seq, weights, tm=512):
    T = x2.shape[0]
    npos = seq // tm
    tab_spec = pl.BlockSpec((tm, LANES), lambda i: (i % npos, 0))

    def slice_spec(w):
        rows = max(w.shape[0] // (T // tm), BF16_ROWS)
        reps = rows * (T // tm) // w.shape[0]
        return pl.BlockSpec((rows, w.shape[1]), lambda i: (i // reps, 0))

    w_specs = [slice_spec(w) for w in weights]
    return pl.pallas_call(
        _inproj_kernel,
        grid=(T // tm,),
        in_specs=[
            pl.BlockSpec((tm, D_MODEL), lambda i: (i, 0)),
            _resident((1, D_MODEL)),
            _resident(w_in.shape),
            tab_spec, tab_spec, tab_spec, tab_spec,
            _resident(q_gain.shape), _resident(k_gain.shape),
        ] + w_specs,
        out_specs=[
            pl.BlockSpec((tm, W_RET), lambda i: (i, 0)),
            pl.BlockSpec((tm, W_GATE), lambda i: (i, 0)),
        ] + [
            pl.BlockSpec((None, dil, tm // dil, 3 * ATT_W), lambda i: (i // npos, 0, i % npos, 0))
            for _, dil in ATT_PATTERNS
        ] + w_specs,
        out_shape=[
            jax.ShapeDtypeStruct((T, W_RET), BF16),
            jax.ShapeDtypeStruct((T, W_GATE), BF16),
        ] + [
            jax.ShapeDtypeStruct((T // seq, dil, seq // dil, 3 * ATT_W), BF16)
            for _, dil in ATT_PATTERNS
        ] + [jax.ShapeDtypeStruct(w.shape, BF16) for w in weights],
        scratch_shapes=[pltpu.VMEM((D_MODEL // LANES, tm, LANES), F32)],
        compiler_params=pltpu.CompilerParams(
            dimension_semantics=("arbitrary",), vmem_limit_bytes=VMEM_LIMIT),
        name="in_proj",
    )(x2, g_mix, w_in, *tabs, q_gain, k_gain, *weights)


def _retention_kernel(q_ref, k_ref, v_ref, rg_ref, gn_ref, inner_ref, qd_ref, kd_ref, cd_ref,
                      y_ref, *, n_chunks):
    C = RET_CHUNK
    inner_decay, q_decay, k_decay, chunk_decay = inner_ref[...], qd_ref[...], kd_ref[...], cd_ref[...]
    gn = gn_ref[...]
    lhs, kvs = [], []
    for c in range(n_chunks):
        rows = slice(c * C, (c + 1) * C)
        q, k, v = q_ref[rows, :], k_ref[rows, :], v_ref[rows, :]
        scores = lax.dot_general(q, k, (((1,), (1,)), ((), ())), preferred_element_type=F32) * inner_decay
        qd = (q.astype(F32) * q_decay).astype(BF16)
        lhs.append(jnp.concatenate([scores.astype(BF16), qd], axis=1))
        kd = (k.astype(F32) * k_decay).astype(BF16)
        kvs.append(lax.dot_general(kd, v, (((0,), (0,)), ((), ())), preferred_element_type=F32))
    state = jnp.zeros((RET_DK, RET_DV), F32)
    for c in range(n_chunks):
        rows = slice(c * C, (c + 1) * C)
        rhs = jnp.concatenate([v_ref[rows, :], state.astype(BF16)], axis=0)
        y = jnp.dot(lhs[c], rhs, preferred_element_type=F32)
        state = state * chunk_decay + kvs[c]
        half_rg = rg_ref[rows, :].astype(F32)
        silu = half_rg * jnp.tanh(half_rg) + half_rg
        y_ref[rows, :] = (silu * (_rms(y) * gn)).astype(BF16)


def _retention(zr, zg, ret_gn, dec):
    B, S, _ = zr.shape
    ts = S
    inner_decay, q_decay, k_decay, chunk_decay = dec
    head_tab = lambda shape: pl.BlockSpec((None,) + shape, lambda b, h, s: (h, 0, 0))
    return pl.pallas_call(
        functools.partial(_retention_kernel, n_chunks=ts // RET_CHUNK),
        grid=(B, RET_HEADS, S // ts),
        in_specs=[
            pl.BlockSpec((None, ts, RET_DK), lambda b, h, s: (b, s, h)),
            pl.BlockSpec((None, ts, RET_DK), lambda b, h, s: (b, s, RET_HEADS + h)),
            pl.BlockSpec((None, ts, RET_DV), lambda b, h, s: (b, s, 2 * RET_QK // RET_DV + h)),
            pl.BlockSpec((None, ts, RET_DV), lambda b, h, s: (b, s, h)),
            pl.BlockSpec((1, RET_DV), lambda b, h, s: (0, h)),
            head_tab((RET_CHUNK, RET_CHUNK)),
            head_tab((RET_CHUNK, RET_DK)),
            head_tab((RET_CHUNK, RET_DK)),
            head_tab((RET_DK, RET_DV)),
        ],
        out_specs=pl.BlockSpec((None, ts, RET_DV), lambda b, h, s: (b, s, h)),
        out_shape=jax.ShapeDtypeStruct((B, S, RET_V), BF16),
        compiler_params=pltpu.CompilerParams(
            dimension_semantics=("parallel", "parallel", "arbitrary"), vmem_limit_bytes=VMEM_LIMIT),
        name="retention",
    )(zr, zr, zr, zg, ret_gn, inner_decay, q_decay, k_decay, chunk_decay)


def _attn_unit(q, parts):
    nt = (((1,), (1,)), ((), ()))
    s = [lax.dot_general(q, k, nt, preferred_element_type=F32) + bias for k, _, bias in parts]
    s = s[0] if len(s) == 1 else jnp.concatenate(s, axis=1)
    m = jnp.max(s, axis=-1, keepdims=True)
    e = jnp.exp2(s - m).astype(BF16)
    ol, off = None, 0
    for _, v, _ in parts:
        n = v.shape[0]
        t = jnp.dot(e[:, off:off + n], jnp.concatenate([v, jnp.ones_like(v)], axis=1),
                    preferred_element_type=F32)
        ol = t if ol is None else ol + t
        off += n
    return ol[:, :ATT_HD], m, ol[:, ATT_HD:]


def _attention_kernel(q_ref, k_ref, v_ref, *rest, n_res, n_blk, has_prev):
    if has_prev:
        kp_ref, vp_ref, o_ref, st_ref = rest
    else:
        o_ref, st_ref = rest
    W = ATT_BLK
    neg = jnp.float32(-jnp.inf)
    row2 = lax.broadcasted_iota(jnp.int32, (W, 2 * W), 0)
    col2 = lax.broadcasted_iota(jnp.int32, (W, 2 * W), 1)
    band = jnp.where((col2 >= row2) & (col2 <= row2 + W), 0.0, neg)
    first = jnp.where(col2 <= row2, 0.0, neg)
    lane_half = lax.broadcasted_iota(jnp.int32, (W, LANES), 1) // STAT_LANES
    if has_prev:
        no_prev = W * (1 - jnp.minimum(pl.program_id(1), 1))
        prev_own = jnp.where((col2 >= row2) & (col2 <= row2 + W) & (col2 >= no_prev), 0.0, neg)

    for r in range(n_res):
        for b in range(n_blk):
            rows = slice(b * W, (b + 1) * W)
            st_tile = jnp.zeros((W, LANES), F32)
            for hh in range(ATT_HEADS):
                cols = slice(hh * ATT_HD, (hh + 1) * ATT_HD)
                q = q_ref[r, rows, cols]
                if b > 0:
                    keys = slice((b - 1) * W, (b + 1) * W)
                    parts = [(k_ref[r, keys, cols], v_ref[r, keys, cols], band)]
                elif has_prev:
                    parts = [(jnp.concatenate([kp_ref[0, :, cols], k_ref[r, rows, cols]], axis=0),
                              jnp.concatenate([vp_ref[0, :, cols], v_ref[r, rows, cols]], axis=0),
                              prev_own)]
                else:
                    keys = slice(0, 2 * W)
                    parts = [(k_ref[r, keys, cols], v_ref[r, keys, cols], first)]
                o, m, l = _attn_unit(q, parts)
                o_ref[r, rows, cols] = o.astype(BF16)
                st_tile = jnp.where(lane_half == 2 * hh, m, st_tile)
                st_tile = jnp.where(lane_half == 2 * hh + 1, l, st_tile)
            st_ref[r, rows, :] = st_tile


def _attention(za, g, rows_per_step=4096):
    window, dil = ATT_PATTERNS[g]
    assert window // dil == ATT_BLK
    batch, _, L, _ = za.shape
    has_prev = L > rows_per_step
    lc = min(L, rows_per_step)
    n_res = rows_per_step // lc
    n_blk = lc // ATT_BLK
    assert n_blk >= 2
    grid = (batch, L // lc if has_prev else dil // n_res)
    idx = (lambda b, j, seg: (b, 0, j, seg)) if has_prev else (lambda b, j, seg: (b, j, 0, seg))
    in_specs = [pl.BlockSpec((None, n_res, lc, ATT_W), functools.partial(idx, seg=seg))
                for seg in range(3)]
    operands = [za, za, za]
    if has_prev:
        in_specs += [pl.BlockSpec((None, 1, ATT_BLK, ATT_W),
                                  lambda b, j, seg=seg: (b, 0, jnp.maximum(j * n_blk - 1, 0), seg))
                     for seg in (1, 2)]
        operands += [za, za]
    out_specs = [pl.BlockSpec((None, n_res, lc, ATT_W), functools.partial(idx, seg=0)),
                 pl.BlockSpec((None, n_res, lc, LANES), functools.partial(idx, seg=0))]
    out_shape = [jax.ShapeDtypeStruct((batch, dil, L, ATT_W), BF16),
                 jax.ShapeDtypeStruct((batch, dil, L, LANES), F32)]
    return pl.pallas_call(
        functools.partial(_attention_kernel, n_res=n_res, n_blk=n_blk, has_prev=has_prev),
        grid=grid, in_specs=in_specs, out_specs=out_specs, out_shape=out_shape,
        compiler_params=pltpu.CompilerParams(
            dimension_semantics=("parallel", "parallel"), vmem_limit_bytes=VMEM_LIMIT),
        name=f"attention_g{g}",
    )(*operands)


def _merge_mlp_kernel(x_ref, y_ref, o0_ref, o1_ref, o2_ref, s0_ref, s1_ref, s2_ref, gr_ref, ga_ref,
                      bg_ref, wr_ref, wa_ref, wo_ref, p_ref, gm_ref, wu_ref, wd_ref, gp_ref, wpg_ref,
                      wpp_ref, out_ref, *perm_refs, ff_chunk):
    tm = x_ref.shape[0]

    def token_order(ref, scratch):
        dil = ref.shape[0]
        if dil == 1:
            return ref[0].astype(F32)
        n_slab = ref.shape[2] // LANES
        for r in range(dil):
            for c in range(n_slab):
                scratch[c, pl.ds(r, tm // dil, stride=dil), :] = (
                    ref[r, :, c * LANES:(c + 1) * LANES].astype(F32))
        return jnp.concatenate([scratch[c] for c in range(n_slab)], axis=1)

    o_tok = [token_order(o0_ref, None), token_order(o1_ref, perm_refs[0]),
             token_order(o2_ref, perm_refs[1])]
    stats = [token_order(s0_ref, None), token_order(s1_ref, perm_refs[2]),
             token_order(s2_ref, perm_refs[3])]
    sum_lane = (lax.broadcasted_iota(jnp.int32, (tm, LANES), 1) // STAT_LANES) % 2 == 1
    ms = stats
    lses = [st + pltpu.roll(jnp.log2(jnp.where(sum_lane, st, 1.0)), LANES - STAT_LANES, axis=1)
            for st in stats]
    top = jnp.maximum(jnp.maximum(lses[0], lses[1]), lses[2])
    den = jnp.exp2(lses[0] - top) + jnp.exp2(lses[1] - top) + jnp.exp2(lses[2] - top)
    wts = [jnp.exp2(m - top) / den for m in ms]
    per_head = LANES // ATT_HEADS
    parts = []
    for hh in range(ATT_HEADS):
        cols = slice(hh * ATT_HD, (hh + 1) * ATT_HD)
        acc = None
        for w, o_g in zip(wts, o_tok):
            t = w[:, hh * per_head:hh * per_head + 1] * o_g[:, cols]
            acc = t if acc is None else acc + t
        parts.append(acc)
    o = jnp.concatenate(parts, axis=1).astype(BF16)
    ret_branch = jnp.dot(y_ref[...], wr_ref[...], preferred_element_type=F32)
    att_branch = jnp.dot(o, wa_ref[...], preferred_element_type=F32)
    bg = bg_ref[...]
    gate_r = _sigmoid(gr_ref[...].astype(F32) + bg[:, :D_MODEL])
    gate_a = _sigmoid(ga_ref[...].astype(F32) + bg[:, D_MODEL:])
    mix = (gate_r * ret_branch + gate_a * att_branch).astype(BF16)
    x1 = x_ref[...] + jnp.dot(mix, wo_ref[...], preferred_element_type=F32)

    hn = (_rms(x1) * gm_ref[...]).astype(BF16)
    acc = x1
    for c in range(D_FF // ff_chunk):
        cols = slice(c * ff_chunk, (c + 1) * ff_chunk)
        u = jnp.dot(hn, wu_ref[:, cols], preferred_element_type=F32)
        a = jnp.square(jnp.maximum(u, 0.0)).astype(BF16)
        acc = acc + jnp.dot(a, wd_ref[cols, :], preferred_element_type=F32)
    hp = (_rms(acc) * gp_ref[...]).astype(BF16)
    gate_p = _sigmoid(jnp.dot(hp, wpg_ref[...], preferred_element_type=F32))
    proj = jnp.dot(p_ref[...].astype(BF16), wpp_ref[...], preferred_element_type=F32)
    out_ref[...] = acc + gate_p * proj


def _merge_mlp(x2, yg, os_, ss_, zg, b_gate, w_ret_out, w_att_out, w_o, p2, g_mlp, w_up, w_down,
               g_ple, w_pg, w_pp, tm=512, ff_chunk=1024):
    T = x2.shape[0]
    npos = os_[0].shape[1] * os_[0].shape[2] // tm
    row = lambda w: pl.BlockSpec((tm, w), lambda i: (i, 0))
    res_major = lambda a: pl.BlockSpec((None, a.shape[1], tm // a.shape[1], a.shape[3]),
                                       lambda i: (i // npos, 0, i % npos, 0))
    dilated = [a for a in os_ + ss_ if a.shape[1] > 1]
    return pl.pallas_call(
        functools.partial(_merge_mlp_kernel, ff_chunk=ff_chunk),
        grid=(T // tm,),
        in_specs=[
            row(D_MODEL), row(RET_V), *[res_major(a) for a in os_ + ss_],
            pl.BlockSpec((tm, D_MODEL), lambda i: (i, RET_V // D_MODEL)),
            pl.BlockSpec((tm, D_MODEL), lambda i: (i, RET_V // D_MODEL + 1)),
            _resident(b_gate.shape), _resident(w_ret_out.shape), _resident(w_att_out.shape),
            _resident(w_o.shape),
            row(PLE_DIM), _resident((1, D_MODEL)), _resident(w_up.shape), _resident(w_down.shape),
            _resident((1, D_MODEL)), _resident(w_pg.shape), _resident(w_pp.shape),
        ],
        out_specs=row(D_MODEL),
        out_shape=jax.ShapeDtypeStruct((T, D_MODEL), F32),
        scratch_shapes=[pltpu.VMEM((a.shape[3] // LANES, tm, LANES), F32) for a in dilated],
        compiler_params=pltpu.CompilerParams(
            dimension_semantics=("parallel",), vmem_limit_bytes=VMEM_LIMIT),
        name="merge_mlp",
    )(x2, yg, *os_, *ss_, zg, zg, b_gate, w_ret_out, w_att_out, w_o, p2, g_mlp, w_up, w_down, g_ple,
      w_pg, w_pp)


def _rot_tables(inv_freq, seq):
    ang = np.arange(seq, dtype=np.float64)[:, None] * inv_freq[None, :]
    cos, sin = np.cos(ang), np.sin(ang)
    return (np.concatenate([cos, cos], axis=1).astype(np.float32),
            np.concatenate([-sin, sin], axis=1).astype(np.float32))


def _decay_tables():
    H, C = RET_HEADS, RET_CHUNK
    log_g = np.log1p(-np.exp2(-5.0 - np.arange(H, dtype=np.float64)))
    idx = np.arange(C, dtype=np.float64)
    diff = idx[:, None] - idx[None, :]
    inner = np.where(diff >= 0, np.exp(log_g[:, None, None] * np.maximum(diff, 0.0)), 0.0)
    q_decay = np.exp(log_g[:, None] * (idx[None, :] + 1.0))
    k_decay = np.exp(log_g[:, None] * (C - 1.0 - idx[None, :]))
    chunk_decay = np.exp(log_g * C)
    f32 = lambda a: np.ascontiguousarray(a, dtype=np.float32)
    return (f32(inner),
            f32(np.broadcast_to(q_decay[:, :, None], (H, C, RET_DK))),
            f32(np.broadcast_to(k_decay[:, :, None], (H, C, RET_DK))),
            f32(np.broadcast_to(chunk_decay[:, None, None], (H, RET_DK, RET_DV))))


def _layer(x, p_i, w_in, b_gate, g_mix, q_gain, k_gain, ret_gn, w_ret_out, w_att_out, w_o,
           g_mlp, w_up, w_down, g_ple, w_ple_proj, w_ple_gate):
    B, S, D = x.shape
    T = B * S
    x2 = x.reshape(T, D)
    ret_freq = 1.0 / (10000.0 ** np.linspace(0.0, 1.0, RET_DK // 2))
    rope_freq = ROPE_THETA ** (-np.arange(0, ATT_HD, 2, dtype=np.float64) / ATT_HD)
    tabs = _rot_tables(ret_freq, S) + _rot_tables(rope_freq, S)

    col_scale = jnp.ones((w_in.shape[1],), F32).at[W_RET:W_RET + RET_V].set(0.5)
    zr, zg, za0, za1, za2, w_ret_out, w_att_out, w_o, w_up, w_down, w_ple_gate, w_ple_proj = _in_proj(
        x2, g_mix.reshape(1, D), (w_in * col_scale).astype(BF16), tabs, q_gain, k_gain, S,
        [w_ret_out, w_att_out, w_o, w_up, w_down, w_ple_gate, w_ple_proj])
    zas = (za0, za1, za2)
    yg = _retention(zr.reshape(B, S, W_RET), zg.reshape(B, S, W_GATE), ret_gn.reshape(1, RET_V),
                    _decay_tables())
    att = [_attention(zas[g], g) for g in range(N_GROUPS)]
    out = _merge_mlp(x2, yg.reshape(T, RET_V), [o for o, _ in att], [s for _, s in att], zg,
                     b_gate.reshape(1, -1), w_ret_out, w_att_out, w_o, p_i.reshape(T, PLE_DIM),
                     g_mlp.reshape(1, D), w_up, w_down, g_ple.reshape(1, D), w_ple_gate, w_ple_proj)
    return out.reshape(B, S, D)


def kernel(x, p, w_in, b_gate, g_mix, q_gain, k_gain, ret_gn, w_ret_out, w_att_out, w_o, g_mlp, w_up,
           w_down, g_ple, w_ple_proj, w_ple_gate):
    for i in range(p.shape[0]):
        x = _layer(x, p[i], w_in[i], b_gate[i], g_mix[i], q_gain[i], k_gain[i], ret_gn[i],
                   w_ret_out[i], w_att_out[i], w_o[i], g_mlp[i], w_up[i], w_down[i], g_ple[i],
                   w_ple_proj[i], w_ple_gate[i])
    return x
```

```python
import functools

import jax
import jax.numpy as jnp
import numpy as np
from jax import lax
from jax.experimental import pallas as pl
from jax.experimental.pallas import tpu as pltpu

F32 = jnp.float32
BF16 = jnp.bfloat16

D_MODEL = 1024
PLE_DIM = 256
D_FF = 4 * D_MODEL
EPS = 1e-6
RET_HEADS = 4
RET_DK = 128
RET_DV = 256
RET_CHUNK = 128
RET_QK = RET_HEADS * RET_DK
RET_V = RET_HEADS * RET_DV
ATT_PATTERNS = ((128, 1), (512, 4), (2048, 16))
N_GROUPS = len(ATT_PATTERNS)
ATT_HEADS = 4
ATT_HD = 128
ATT_W = ATT_HEADS * ATT_HD
ATT_BLK = 128
ROPE_THETA = 10000.0
LOG2E = 1.4426950408889634
ATT_QSCALE = ATT_HD ** -0.5 * LOG2E

LANES = 128
STAT_LANES = LANES // (2 * ATT_HEADS)
SEG = 512
W_RET = 2 * RET_QK + RET_V
W_GATE = RET_V + 2 * D_MODEL
W_ATT = 3 * N_GROUPS * ATT_W
VMEM_LIMIT = 62 * 1024 * 1024
W_IN_STAGE_ROWS = 32
BF16_ROWS = 16


def _resident(shape):
    return pl.BlockSpec(shape, lambda *_: (0,) * len(shape), pipeline_mode=pl.Buffered(1))


def _rms(x):
    return x * lax.rsqrt(jnp.mean(x * x, axis=-1, keepdims=True) + EPS)


def _sigmoid(x):
    return 0.5 * jnp.tanh(0.5 * x) + 0.5


def _rotate(a, cos, sin_signed):
    return a * cos + pltpu.roll(a, LANES // 2, axis=1) * sin_signed


def _load_w_in(w_hbm_ref, w_ref, stage_ref, sem):
    rows = stage_ref.shape[1]
    n_blk = w_hbm_ref.shape[0] // rows
    rg = slice(W_RET, W_RET + RET_V)

    def copy(k):
        return pltpu.make_async_copy(w_hbm_ref.at[pl.ds(k * rows, rows), :], stage_ref.at[k % 2],
                                     sem.at[k % 2])

    copy(0).start()
    for k in range(n_blk):
        if k + 1 < n_blk:
            copy(k + 1).start()
        copy(k).wait()
        dst = slice(k * rows, (k + 1) * rows)
        w_ref[dst, :rg.start] = stage_ref[k % 2, :, :rg.start].astype(BF16)
        w_ref[dst, rg] = (stage_ref[k % 2, :, rg] * 0.5).astype(BF16)
        w_ref[dst, rg.stop:] = stage_ref[k % 2, :, rg.stop:].astype(BF16)


def _inproj_kernel(x_ref, g_ref, w_hbm_ref, cr_ref, sr_ref, ca_ref, sa_ref, qg_ref, kg_ref, *rest):
    n_w = (len(rest) - 9) // 2
    zr_ref, zg_ref, *za_refs = rest[n_w:n_w + 5]
    perm_ref, w_ref, stage_ref, sem = rest[-4:]
    for w_f32_ref, w_bf16_ref in zip(rest[:n_w], rest[n_w + 5:-4]):
        w_bf16_ref[...] = w_f32_ref[...].astype(BF16)

    @pl.when(pl.program_id(0) == 0)
    def _():
        _load_w_in(w_hbm_ref, w_ref, stage_ref, sem)

    tm = x_ref.shape[0]
    n_slab = D_MODEL // LANES

    def residue_major(load, dil):
        return jnp.concatenate([load(pl.ds(r, tm // dil, stride=dil)) for r in range(dil)], axis=0)

    xn = _rms(x_ref[...]) * g_ref[...]
    for c in range(n_slab):
        perm_ref[c] = xn[:, c * LANES:(c + 1) * LANES]
    lhs, cos_a, sin_a = {}, {}, {}
    for dil in sorted({d for _, d in ATT_PATTERNS}):
        if dil == 1:
            lhs[dil], cos_a[dil], sin_a[dil] = xn.astype(BF16), ca_ref[...], sa_ref[...]
            continue
        lhs[dil] = jnp.concatenate(
            [residue_major(lambda rows, c=c: perm_ref[c, rows, :], dil) for c in range(n_slab)],
            axis=1).astype(BF16)
        cos_a[dil] = residue_major(lambda rows: ca_ref[rows, :], dil)
        sin_a[dil] = residue_major(lambda rows: sa_ref[rows, :], dil)
    h = lhs[1]

    def store_att(g, seg, val):
        dil = ATT_PATTERNS[g][1]
        za_refs[g][:, :, seg * ATT_W:(seg + 1) * ATT_W] = (
            val.astype(BF16).reshape(dil, tm // dil, ATT_W))

    n_ret, n_rg, n_att = W_RET // SEG, RET_V // SEG, W_ATT // SEG
    n_all = (W_RET + W_GATE + W_ATT) // SEG
    heavy = [0, 1] + list(range(n_ret + n_rg, n_ret + n_rg + n_att))
    for j in heavy + [j for j in range(n_all) if j not in heavy]:
        w_chunk = w_ref[:, j * SEG:(j + 1) * SEG]
        if n_ret + n_rg <= j < n_ret + n_rg + n_att:
            jj = j - n_ret - n_rg
            seg, g = jj // N_GROUPS, jj % N_GROUPS
            dil = ATT_PATTERNS[g][1]
            acc = jnp.dot(lhs[dil], w_chunk, preferred_element_type=F32)
            if seg < 2:
                gain = (qg_ref if seg == 0 else kg_ref)[g:g + 1, :]
                if seg == 0:
                    gain = gain * ATT_QSCALE
                parts = []
                for hh in range(SEG // LANES):
                    a = _rms(acc[:, hh * LANES:(hh + 1) * LANES]) * gain
                    parts.append(_rotate(a, cos_a[dil], sin_a[dil]))
                store_att(g, seg, jnp.concatenate(parts, axis=1))
            else:
                store_att(g, seg, acc)
            continue
        acc = jnp.dot(h, w_chunk, preferred_element_type=F32)
        if j < 2:
            cos, sin = cr_ref[...], sr_ref[...]
            for hh in range(SEG // LANES):
                r = _rotate(acc[:, hh * LANES:(hh + 1) * LANES], cos, sin)
                if j == 1:
                    r = r * (RET_DK ** -0.5)
                zr_ref[:, j * SEG + hh * LANES:j * SEG + (hh + 1) * LANES] = r.astype(BF16)
        elif j < n_ret:
            zr_ref[:, j * SEG:(j + 1) * SEG] = acc.astype(BF16)
        elif j < n_ret + n_rg:
            jj = j - n_ret
            zg_ref[:, jj * SEG:(jj + 1) * SEG] = acc.astype(BF16)
        else:
            jj = j - n_ret - n_att
            zg_ref[:, jj * SEG:(jj + 1) * SEG] = acc.astype(BF16)


def _in_proj(x2, g_mix, w_in, tabs, q_gain, k_gain, seq, weights, tm=512):
    T = x2.shape[0]
    npos = seq // tm
    tab_spec = pl.BlockSpec((tm, LANES), lambda i: (i % npos, 0))

    def slice_spec(w):
        rows = max(w.shape[0] // (T // tm), BF16_ROWS)
        reps = rows * (T // tm) // w.shape[0]
        return pl.BlockSpec((rows, w.shape[1]), lambda i: (i // reps, 0))

    w_specs = [slice_spec(w) for w in weights]
    return pl.pallas_call(
        _inproj_kernel,
        grid=(T // tm,),
        in_specs=[
            pl.BlockSpec((tm, D_MODEL), lambda i: (i, 0)),
            _resident((1, D_MODEL)),
            pl.BlockSpec(memory_space=pl.ANY),
            tab_spec, tab_spec, tab_spec, tab_spec,
            _resident(q_gain.shape), _resident(k_gain.shape),
        ] + w_specs,
        out_specs=[
            pl.BlockSpec((tm, W_RET), lambda i: (i, 0)),
            pl.BlockSpec((tm, W_GATE), lambda i: (i, 0)),
        ] + [
            pl.BlockSpec((None, dil, tm // dil, 3 * ATT_W), lambda i: (i // npos, 0, i % npos, 0))
            for _, dil in ATT_PATTERNS
        ] + w_specs,
        out_shape=[
            jax.ShapeDtypeStruct((T, W_RET), BF16),
            jax.ShapeDtypeStruct((T, W_GATE), BF16),
        ] + [
            jax.ShapeDtypeStruct((T // seq, dil, seq // dil, 3 * ATT_W), BF16)
            for _, dil in ATT_PATTERNS
        ] + [jax.ShapeDtypeStruct(w.shape, BF16) for w in weights],
        scratch_shapes=[pltpu.VMEM((D_MODEL // LANES, tm, LANES), F32),
                        pltpu.VMEM(w_in.shape, BF16),
                        pltpu.VMEM((2, W_IN_STAGE_ROWS, w_in.shape[1]), F32),
                        pltpu.SemaphoreType.DMA((2,))],
        compiler_params=pltpu.CompilerParams(
            dimension_semantics=("arbitrary",), vmem_limit_bytes=VMEM_LIMIT),
        name="in_proj",
    )(x2, g_mix, w_in, *tabs, q_gain, k_gain, *weights)


def _retention_kernel(q_ref, k_ref, v_ref, rg_ref, gn_ref, inner_ref, qd_ref, kd_ref, cd_ref,
                      y_ref, *, n_chunks):
    C = RET_CHUNK
    inner_decay, q_decay, k_decay, chunk_decay = inner_ref[...], qd_ref[...], kd_ref[...], cd_ref[...]
    gn = gn_ref[...]
    lhs, kvs = [], []
    for c in range(n_chunks):
        rows = slice(c * C, (c + 1) * C)
        q, k, v = q_ref[rows, :], k_ref[rows, :], v_ref[rows, :]
        scores = lax.dot_general(q, k, (((1,), (1,)), ((), ())), preferred_element_type=F32) * inner_decay
        qd = (q.astype(F32) * q_decay).astype(BF16)
        lhs.append(jnp.concatenate([scores.astype(BF16), qd], axis=1))
        kd = (k.astype(F32) * k_decay).astype(BF16)
        kvs.append(lax.dot_general(kd, v, (((0,), (0,)), ((), ())), preferred_element_type=F32))
    state = jnp.zeros((RET_DK, RET_DV), F32)
    for c in range(n_chunks):
        rows = slice(c * C, (c + 1) * C)
        rhs = jnp.concatenate([v_ref[rows, :], state.astype(BF16)], axis=0)
        y = jnp.dot(lhs[c], rhs, preferred_element_type=F32)
        state = state * chunk_decay + kvs[c]
        half_rg = rg_ref[rows, :].astype(F32)
        silu = half_rg * jnp.tanh(half_rg) + half_rg
        y_ref[rows, :] = (silu * (_rms(y) * gn)).astype(BF16)


def _retention(zr, zg, ret_gn, dec):
    B, S, _ = zr.shape
    ts = S
    inner_decay, q_decay, k_decay, chunk_decay = dec
    head_tab = lambda shape: pl.BlockSpec((None,) + shape, lambda b, h, s: (h, 0, 0))
    return pl.pallas_call(
        functools.partial(_retention_kernel, n_chunks=ts // RET_CHUNK),
        grid=(B, RET_HEADS, S // ts),
        in_specs=[
            pl.BlockSpec((None, ts, RET_DK), lambda b, h, s: (b, s, h)),
            pl.BlockSpec((None, ts, RET_DK), lambda b, h, s: (b, s, RET_HEADS + h)),
            pl.BlockSpec((None, ts, RET_DV), lambda b, h, s: (b, s, 2 * RET_QK // RET_DV + h)),
            pl.BlockSpec((None, ts, RET_DV), lambda b, h, s: (b, s, h)),
            pl.BlockSpec((1, RET_DV), lambda b, h, s: (0, h)),
            head_tab((RET_CHUNK, RET_CHUNK)),
            head_tab((RET_CHUNK, RET_DK)),
            head_tab((RET_CHUNK, RET_DK)),
            head_tab((RET_DK, RET_DV)),
        ],
        out_specs=pl.BlockSpec((None, ts, RET_DV), lambda b, h, s: (b, s, h)),
        out_shape=jax.ShapeDtypeStruct((B, S, RET_V), BF16),
        compiler_params=pltpu.CompilerParams(
            dimension_semantics=("parallel", "parallel", "arbitrary"), vmem_limit_bytes=VMEM_LIMIT),
        name="retention",
    )(zr, zr, zr, zg, ret_gn, inner_decay, q_decay, k_decay, chunk_decay)


def _attn_unit(q, parts):
    nt = (((1,), (1,)), ((), ()))
    s = [lax.dot_general(q, k, nt, preferred_element_type=F32) + bias for k, _, bias in parts]
    s = s[0] if len(s) == 1 else jnp.concatenate(s, axis=1)
    m = jnp.max(s, axis=-1, keepdims=True)
    e = jnp.exp2(s - m).astype(BF16)
    ol, off = None, 0
    for _, v, _ in parts:
        n = v.shape[0]
        t = jnp.dot(e[:, off:off + n], jnp.concatenate([v, jnp.ones_like(v)], axis=1),
                    preferred_element_type=F32)
        ol = t if ol is None else ol + t
        off += n
    return ol[:, :ATT_HD], m, ol[:, ATT_HD:]


def _attention_kernel(q_ref, k_ref, v_ref, *rest, n_res, n_blk, has_prev):
    if has_prev:
        kp_ref, vp_ref, o_ref, st_ref = rest
    else:
        o_ref, st_ref = rest
    W = ATT_BLK
    neg = jnp.float32(-jnp.inf)
    row2 = lax.broadcasted_iota(jnp.int32, (W, 2 * W), 0)
    col2 = lax.broadcasted_iota(jnp.int32, (W, 2 * W), 1)
    band = jnp.where((col2 >= row2) & (col2 <= row2 + W), 0.0, neg)
    first = jnp.where(col2 <= row2, 0.0, neg)
    lane_half = lax.broadcasted_iota(jnp.int32, (W, LANES), 1) // STAT_LANES
    if has_prev:
        no_prev = W * (1 - jnp.minimum(pl.program_id(1), 1))
        prev_own = jnp.where((col2 >= row2) & (col2 <= row2 + W) & (col2 >= no_prev), 0.0, neg)

    for r in range(n_res):
        for b in range(n_blk):
            rows = slice(b * W, (b + 1) * W)
            st_tile = jnp.zeros((W, LANES), F32)
            for hh in range(ATT_HEADS):
                cols = slice(hh * ATT_HD, (hh + 1) * ATT_HD)
                q = q_ref[r, rows, cols]
                if b > 0:
                    keys = slice((b - 1) * W, (b + 1) * W)
                    parts = [(k_ref[r, keys, cols], v_ref[r, keys, cols], band)]
                elif has_prev:
                    parts = [(jnp.concatenate([kp_ref[0, :, cols], k_ref[r, rows, cols]], axis=0),
                              jnp.concatenate([vp_ref[0, :, cols], v_ref[r, rows, cols]], axis=0),
                              prev_own)]
                else:
                    keys = slice(0, 2 * W)
                    parts = [(k_ref[r, keys, cols], v_ref[r, keys, cols], first)]
                o, m, l = _attn_unit(q, parts)
                o_ref[r, rows, cols] = o.astype(BF16)
                st_tile = jnp.where(lane_half == 2 * hh, m, st_tile)
                st_tile = jnp.where(lane_half == 2 * hh + 1, l, st_tile)
            st_ref[r, rows, :] = st_tile


def _attention(za, g, rows_per_step=4096):
    window, dil = ATT_PATTERNS[g]
    assert window // dil == ATT_BLK
    batch, _, L, _ = za.shape
    has_prev = L > rows_per_step
    lc = min(L, rows_per_step)
    n_res = rows_per_step // lc
    n_blk = lc // ATT_BLK
    assert n_blk >= 2
    grid = (batch, L // lc if has_prev else dil // n_res)
    idx = (lambda b, j, seg: (b, 0, j, seg)) if has_prev else (lambda b, j, seg: (b, j, 0, seg))
    in_specs = [pl.BlockSpec((None, n_res, lc, ATT_W), functools.partial(idx, seg=seg))
                for seg in range(3)]
    operands = [za, za, za]
    if has_prev:
        in_specs += [pl.BlockSpec((None, 1, ATT_BLK, ATT_W),
                                  lambda b, j, seg=seg: (b, 0, jnp.maximum(j * n_blk - 1, 0), seg))
                     for seg in (1, 2)]
        operands += [za, za]
    out_specs = [pl.BlockSpec((None, n_res, lc, ATT_W), functools.partial(idx, seg=0)),
                 pl.BlockSpec((None, n_res, lc, LANES), functools.partial(idx, seg=0))]
    out_shape = [jax.ShapeDtypeStruct((batch, dil, L, ATT_W), BF16),
                 jax.ShapeDtypeStruct((batch, dil, L, LANES), F32)]
    return pl.pallas_call(
        functools.partial(_attention_kernel, n_res=n_res, n_blk=n_blk, has_prev=has_prev),
        grid=grid, in_specs=in_specs, out_specs=out_specs, out_shape=out_shape,
        compiler_params=pltpu.CompilerParams(
            dimension_semantics=("parallel", "parallel"), vmem_limit_bytes=VMEM_LIMIT),
        name=f"attention_g{g}",
    )(*operands)


def _merge_mlp_kernel(x_ref, y_ref, o0_ref, o1_ref, o2_ref, s0_ref, s1_ref, s2_ref, gr_ref, ga_ref,
                      bg_ref, wr_ref, wa_ref, wo_ref, p_ref, gm_ref, wu_ref, wd_ref, gp_ref, wpg_ref,
                      wpp_ref, out_ref, *perm_refs, ff_chunk):
    tm = x_ref.shape[0]

    def token_order(ref, scratch):
        dil = ref.shape[0]
        if dil == 1:
            return ref[0].astype(F32)
        n_slab = ref.shape[2] // LANES
        for r in range(dil):
            for c in range(n_slab):
                scratch[c, pl.ds(r, tm // dil, stride=dil), :] = (
                    ref[r, :, c * LANES:(c + 1) * LANES].astype(F32))
        return jnp.concatenate([scratch[c] for c in range(n_slab)], axis=1)

    o_tok = [token_order(o0_ref, None), token_order(o1_ref, perm_refs[0]),
             token_order(o2_ref, perm_refs[1])]
    stats = [token_order(s0_ref, None), token_order(s1_ref, perm_refs[2]),
             token_order(s2_ref, perm_refs[3])]
    sum_lane = (lax.broadcasted_iota(jnp.int32, (tm, LANES), 1) // STAT_LANES) % 2 == 1
    ms = stats
    lses = [st + pltpu.roll(jnp.log2(jnp.where(sum_lane, st, 1.0)), LANES - STAT_LANES, axis=1)
            for st in stats]
    top = jnp.maximum(jnp.maximum(lses[0], lses[1]), lses[2])
    den = jnp.exp2(lses[0] - top) + jnp.exp2(lses[1] - top) + jnp.exp2(lses[2] - top)
    wts = [jnp.exp2(m - top) / den for m in ms]
    per_head = LANES // ATT_HEADS
    parts = []
    for hh in range(ATT_HEADS):
        cols = slice(hh * ATT_HD, (hh + 1) * ATT_HD)
        acc = None
        for w, o_g in zip(wts, o_tok):
            t = w[:, hh * per_head:hh * per_head + 1] * o_g[:, cols]
            acc = t if acc is None else acc + t
        parts.append(acc)
    o = jnp.concatenate(parts, axis=1).astype(BF16)
    ret_branch = jnp.dot(y_ref[...], wr_ref[...], preferred_element_type=F32)
    att_branch = jnp.dot(o, wa_ref[...], preferred_element_type=F32)
    bg = bg_ref[...]
    gate_r = _sigmoid(gr_ref[...].astype(F32) + bg[:, :D_MODEL])
    gate_a = _sigmoid(ga_ref[...].astype(F32) + bg[:, D_MODEL:])
    mix = (gate_r * ret_branch + gate_a * att_branch).astype(BF16)
    x1 = x_ref[...] + jnp.dot(mix, wo_ref[...], preferred_element_type=F32)

    hn = (_rms(x1) * gm_ref[...]).astype(BF16)
    acc = x1
    for c in range(D_FF // ff_chunk):
        cols = slice(c * ff_chunk, (c + 1) * ff_chunk)
        u = jnp.dot(hn, wu_ref[:, cols], preferred_element_type=F32)
        a = jnp.square(jnp.maximum(u, 0.0)).astype(BF16)
        acc = acc + jnp.dot(a, wd_ref[cols, :], preferred_element_type=F32)
    hp = (_rms(acc) * gp_ref[...]).astype(BF16)
    gate_p = _sigmoid(jnp.dot(hp, wpg_ref[...], preferred_element_type=F32))
    proj = jnp.dot(p_ref[...].astype(BF16), wpp_ref[...], preferred_element_type=F32)
    out_ref[...] = acc + gate_p * proj


def _merge_mlp(x2, yg, os_, ss_, zg, b_gate, w_ret_out, w_att_out, w_o, p2, g_mlp, w_up, w_down,
               g_ple, w_pg, w_pp, tm=512, ff_chunk=1024):
    T = x2.shape[0]
    npos = os_[0].shape[1] * os_[0].shape[2] // tm
    row = lambda w: pl.BlockSpec((tm, w), lambda i: (i, 0))
    res_major = lambda a: pl.BlockSpec((None, a.shape[1], tm // a.shape[1], a.shape[3]),
                                       lambda i: (i // npos, 0, i % npos, 0))
    dilated = [a for a in os_ + ss_ if a.shape[1] > 1]
    return pl.pallas_call(
        functools.partial(_merge_mlp_kernel, ff_chunk=ff_chunk),
        grid=(T // tm,),
        in_specs=[
            row(D_MODEL), row(RET_V), *[res_major(a) for a in os_ + ss_],
            pl.BlockSpec((tm, D_MODEL), lambda i: (i, RET_V // D_MODEL)),
            pl.BlockSpec((tm, D_MODEL), lambda i: (i, RET_V // D_MODEL + 1)),
            _resident(b_gate.shape), _resident(w_ret_out.shape), _resident(w_att_out.shape),
            _resident(w_o.shape),
            row(PLE_DIM), _resident((1, D_MODEL)), _resident(w_up.shape), _resident(w_down.shape),
            _resident((1, D_MODEL)), _resident(w_pg.shape), _resident(w_pp.shape),
        ],
        out_specs=row(D_MODEL),
        out_shape=jax.ShapeDtypeStruct((T, D_MODEL), F32),
        scratch_shapes=[pltpu.VMEM((a.shape[3] // LANES, tm, LANES), F32) for a in dilated],
        compiler_params=pltpu.CompilerParams(
            dimension_semantics=("parallel",), vmem_limit_bytes=VMEM_LIMIT),
        name="merge_mlp",
    )(x2, yg, *os_, *ss_, zg, zg, b_gate, w_ret_out, w_att_out, w_o, p2, g_mlp, w_up, w_down, g_ple,
      w_pg, w_pp)


def _rot_tables(inv_freq, seq):
    ang = np.arange(seq, dtype=np.float64)[:, None] * inv_freq[None, :]
    cos, sin = np.cos(ang), np.sin(ang)
    return (np.concatenate([cos, cos], axis=1).astype(np.float32),
            np.concatenate([-sin, sin], axis=1).astype(np.float32))


def _decay_tables():
    H, C = RET_HEADS, RET_CHUNK
    log_g = np.log1p(-np.exp2(-5.0 - np.arange(H, dtype=np.float64)))
    idx = np.arange(C, dtype=np.float64)
    diff = idx[:, None] - idx[None, :]
    inner = np.where(diff >= 0, np.exp(log_g[:, None, None] * np.maximum(diff, 0.0)), 0.0)
    q_decay = np.exp(log_g[:, None] * (idx[None, :] + 1.0))
    k_decay = np.exp(log_g[:, None] * (C - 1.0 - idx[None, :]))
    chunk_decay = np.exp(log_g * C)
    f32 = lambda a: np.ascontiguousarray(a, dtype=np.float32)
    return (f32(inner),
            f32(np.broadcast_to(q_decay[:, :, None], (H, C, RET_DK))),
            f32(np.broadcast_to(k_decay[:, :, None], (H, C, RET_DK))),
            f32(np.broadcast_to(chunk_decay[:, None, None], (H, RET_DK, RET_DV))))


def _layer(x, p_i, w_in, b_gate, g_mix, q_gain, k_gain, ret_gn, w_ret_out, w_att_out, w_o,
           g_mlp, w_up, w_down, g_ple, w_ple_proj, w_ple_gate):
    B, S, D = x.shape
    T = B * S
    x2 = x.reshape(T, D)
    ret_freq = 1.0 / (10000.0 ** np.linspace(0.0, 1.0, RET_DK // 2))
    rope_freq = ROPE_THETA ** (-np.arange(0, ATT_HD, 2, dtype=np.float64) / ATT_HD)
    tabs = _rot_tables(ret_freq, S) + _rot_tables(rope_freq, S)

    zr, zg, za0, za1, za2, w_ret_out, w_att_out, w_o, w_up, w_down, w_ple_gate, w_ple_proj = _in_proj(
        x2, g_mix.reshape(1, D), w_in, tabs, q_gain, k_gain, S,
        [w_ret_out, w_att_out, w_o, w_up, w_down, w_ple_gate, w_ple_proj])
    zas = (za0, za1, za2)
    yg = _retention(zr.reshape(B, S, W_RET), zg.reshape(B, S, W_GATE), ret_gn.reshape(1, RET_V),
                    _decay_tables())
    att = [_attention(zas[g], g) for g in range(N_GROUPS)]
    out = _merge_mlp(x2, yg.reshape(T, RET_V), [o for o, _ in att], [s for _, s in att], zg,
                     b_gate.reshape(1, -1), w_ret_out, w_att_out, w_o, p_i.reshape(T, PLE_DIM),
                     g_mlp.reshape(1, D), w_up, w_down, g_ple.reshape(1, D), w_ple_gate, w_ple_proj)
    return out.reshape(B, S, D)


def kernel(x, p, w_in, b_gate, g_mix, q_gain, k_gain, ret_gn, w_ret_out, w_att_out, w_o, g_mlp, w_up,
           w_down, g_ple, w_ple_proj, w_ple_gate):
    for i in range(p.shape[0]):
        x = _layer(x, p[i], w_in[i], b_gate[i], g_mix[i], q_gain[i], k_gain[i], ret_gn[i],
                   w_ret_out[i], w_att_out[i], w_o[i], g_mlp[i], w_up[i], w_down[i], g_ple[i],
                   w_ple_proj[i], w_ple_gate[i])
    return x
```

```python
import functools

import jax
import jax.numpy as jnp
import numpy as np
from jax import lax
from jax.experimental import pallas as pl
from jax.experimental.pallas import tpu as pltpu

F32 = jnp.float32
BF16 = jnp.bfloat16

D_MODEL = 1024
PLE_DIM = 256
D_FF = 4 * D_MODEL
EPS = 1e-6
RET_HEADS = 4
RET_DK = 128
RET_DV = 256
RET_CHUNK = 128
RET_QK = RET_HEADS * RET_DK
RET_V = RET_HEADS * RET_DV
ATT_PATTERNS = ((128, 1), (512, 4), (2048, 16))
N_GROUPS = len(ATT_PATTERNS)
ATT_HEADS = 4
ATT_HD = 128
ATT_W = ATT_HEADS * ATT_HD
ATT_BLK = 128
ROPE_THETA = 10000.0
LOG2E = 1.4426950408889634
ATT_QSCALE = ATT_HD ** -0.5 * LOG2E

LANES = 128
STAT_LANES = LANES // (2 * ATT_HEADS)
SEG = 512
W_RET = 2 * RET_QK + RET_V
W_GATE = RET_V + 2 * D_MODEL
W_ATT = 3 * N_GROUPS * ATT_W
VMEM_LIMIT = 62 * 1024 * 1024
W_IN_STAGE_ROWS = 16
W_IN_STAGE_SLOTS = 4
BF16_ROWS = 16


def _resident(shape):
    return pl.BlockSpec(shape, lambda *_: (0,) * len(shape), pipeline_mode=pl.Buffered(1))


def _rms(x):
    return x * lax.rsqrt(jnp.mean(x * x, axis=-1, keepdims=True) + EPS)


def _sigmoid(x):
    return 0.5 * jnp.tanh(0.5 * x) + 0.5


def _rotate(a, cos, sin_signed):
    return a * cos + pltpu.roll(a, LANES // 2, axis=1) * sin_signed


def _load_w_in(w_hbm_ref, w_ref, stage_ref, sem):
    n_slots, rows, _ = stage_ref.shape
    n_blk = w_hbm_ref.shape[0] // rows
    rg = slice(W_RET, W_RET + RET_V)

    def copy(k):
        slot = k % n_slots
        return pltpu.make_async_copy(w_hbm_ref.at[pl.ds(k * rows, rows), :], stage_ref.at[slot],
                                     sem.at[slot])

    for k in range(n_slots - 1):
        copy(k).start()
    for k in range(n_blk):
        if k + n_slots - 1 < n_blk:
            copy(k + n_slots - 1).start()
        copy(k).wait()
        slot = k % n_slots
        dst = slice(k * rows, (k + 1) * rows)
        w_ref[dst, :rg.start] = stage_ref[slot, :, :rg.start].astype(BF16)
        w_ref[dst, rg] = (stage_ref[slot, :, rg] * 0.5).astype(BF16)
        w_ref[dst, rg.stop:] = stage_ref[slot, :, rg.stop:].astype(BF16)


def _inproj_kernel(x_ref, g_ref, w_hbm_ref, cr_ref, sr_ref, ca_ref, sa_ref, qg_ref, kg_ref, *rest):
    n_w = (len(rest) - 9) // 2
    zr_ref, zg_ref, *za_refs = rest[n_w:n_w + 5]
    perm_ref, w_ref, stage_ref, sem = rest[-4:]
    for w_f32_ref, w_bf16_ref in zip(rest[:n_w], rest[n_w + 5:-4]):
        w_bf16_ref[...] = w_f32_ref[...].astype(BF16)

    @pl.when(pl.program_id(0) == 0)
    def _():
        _load_w_in(w_hbm_ref, w_ref, stage_ref, sem)

    tm = x_ref.shape[0]
    n_slab = D_MODEL // LANES

    def residue_major(load, dil):
        return jnp.concatenate([load(pl.ds(r, tm // dil, stride=dil)) for r in range(dil)], axis=0)

    xn = _rms(x_ref[...]) * g_ref[...]
    for c in range(n_slab):
        perm_ref[c] = xn[:, c * LANES:(c + 1) * LANES]
    lhs, cos_a, sin_a = {}, {}, {}
    for dil in sorted({d for _, d in ATT_PATTERNS}):
        if dil == 1:
            lhs[dil], cos_a[dil], sin_a[dil] = xn.astype(BF16), ca_ref[...], sa_ref[...]
            continue
        lhs[dil] = jnp.concatenate(
            [residue_major(lambda rows, c=c: perm_ref[c, rows, :], dil) for c in range(n_slab)],
            axis=1).astype(BF16)
        cos_a[dil] = residue_major(lambda rows: ca_ref[rows, :], dil)
        sin_a[dil] = residue_major(lambda rows: sa_ref[rows, :], dil)
    h = lhs[1]

    def store_att(g, seg, val):
        dil = ATT_PATTERNS[g][1]
        za_refs[g][:, :, seg * ATT_W:(seg + 1) * ATT_W] = (
            val.astype(BF16).reshape(dil, tm // dil, ATT_W))

    n_ret, n_rg, n_att = W_RET // SEG, RET_V // SEG, W_ATT // SEG
    n_all = (W_RET + W_GATE + W_ATT) // SEG
    heavy = [0, 1] + list(range(n_ret + n_rg, n_ret + n_rg + n_att))
    for j in heavy + [j for j in range(n_all) if j not in heavy]:
        w_chunk = w_ref[:, j * SEG:(j + 1) * SEG]
        if n_ret + n_rg <= j < n_ret + n_rg + n_att:
            jj = j - n_ret - n_rg
            seg, g = jj // N_GROUPS, jj % N_GROUPS
            dil = ATT_PATTERNS[g][1]
            acc = jnp.dot(lhs[dil], w_chunk, preferred_element_type=F32)
            if seg < 2:
                gain = (qg_ref if seg == 0 else kg_ref)[g:g + 1, :]
                if seg == 0:
                    gain = gain * ATT_QSCALE
                parts = []
                for hh in range(SEG // LANES):
                    a = _rms(acc[:, hh * LANES:(hh + 1) * LANES]) * gain
                    parts.append(_rotate(a, cos_a[dil], sin_a[dil]))
                store_att(g, seg, jnp.concatenate(parts, axis=1))
            else:
                store_att(g, seg, acc)
            continue
        acc = jnp.dot(h, w_chunk, preferred_element_type=F32)
        if j < 2:
            cos, sin = cr_ref[...], sr_ref[...]
            for hh in range(SEG // LANES):
                r = _rotate(acc[:, hh * LANES:(hh + 1) * LANES], cos, sin)
                if j == 1:
                    r = r * (RET_DK ** -0.5)
                zr_ref[:, j * SEG + hh * LANES:j * SEG + (hh + 1) * LANES] = r.astype(BF16)
        elif j < n_ret:
            zr_ref[:, j * SEG:(j + 1) * SEG] = acc.astype(BF16)
        elif j < n_ret + n_rg:
            jj = j - n_ret
            zg_ref[:, jj * SEG:(jj + 1) * SEG] = acc.astype(BF16)
        else:
            jj = j - n_ret - n_att
            zg_ref[:, jj * SEG:(jj + 1) * SEG] = acc.astype(BF16)


def _in_proj(x2, g_mix, w_in, tabs, q_gain, k_gain, seq, weights, tm=512):
    T = x2.shape[0]
    npos = seq // tm
    tab_spec = pl.BlockSpec((tm, LANES), lambda i: (i % npos, 0))

    def slice_spec(w):
        rows = max(w.shape[0] // (T // tm), BF16_ROWS)
        reps = rows * (T // tm) // w.shape[0]
        return pl.BlockSpec((rows, w.shape[1]), lambda i: (i // reps, 0))

    w_specs = [slice_spec(w) for w in weights]
    return pl.pallas_call(
        _inproj_kernel,
        grid=(T // tm,),
        in_specs=[
            pl.BlockSpec((tm, D_MODEL), lambda i: (i, 0)),
            _resident((1, D_MODEL)),
            pl.BlockSpec(memory_space=pl.ANY),
            tab_spec, tab_spec, tab_spec, tab_spec,
            _resident(q_gain.shape), _resident(k_gain.shape),
        ] + w_specs,
        out_specs=[
            pl.BlockSpec((tm, W_RET), lambda i: (i, 0)),
            pl.BlockSpec((tm, W_GATE), lambda i: (i, 0)),
        ] + [
            pl.BlockSpec((None, dil, tm // dil, 3 * ATT_W), lambda i: (i // npos, 0, i % npos, 0))
            for _, dil in ATT_PATTERNS
        ] + w_specs,
        out_shape=[
            jax.ShapeDtypeStruct((T, W_RET), BF16),
            jax.ShapeDtypeStruct((T, W_GATE), BF16),
        ] + [
            jax.ShapeDtypeStruct((T // seq, dil, seq // dil, 3 * ATT_W), BF16)
            for _, dil in ATT_PATTERNS
        ] + [jax.ShapeDtypeStruct(w.shape, BF16) for w in weights],
        scratch_shapes=[pltpu.VMEM((D_MODEL // LANES, tm, LANES), F32),
                        pltpu.VMEM(w_in.shape, BF16),
                        pltpu.VMEM((W_IN_STAGE_SLOTS, W_IN_STAGE_ROWS, w_in.shape[1]), F32),
                        pltpu.SemaphoreType.DMA((W_IN_STAGE_SLOTS,))],
        compiler_params=pltpu.CompilerParams(
            dimension_semantics=("arbitrary",), vmem_limit_bytes=VMEM_LIMIT),
        name="in_proj",
    )(x2, g_mix, w_in, *tabs, q_gain, k_gain, *weights)


def _retention_kernel(q_ref, k_ref, v_ref, rg_ref, gn_ref, inner_ref, qd_ref, kd_ref, cd_ref,
                      y_ref, *, n_chunks):
    C = RET_CHUNK
    inner_decay, q_decay, k_decay, chunk_decay = inner_ref[...], qd_ref[...], kd_ref[...], cd_ref[...]
    gn = gn_ref[...]
    lhs, kvs = [], []
    for c in range(n_chunks):
        rows = slice(c * C, (c + 1) * C)
        q, k, v = q_ref[rows, :], k_ref[rows, :], v_ref[rows, :]
        scores = lax.dot_general(q, k, (((1,), (1,)), ((), ())), preferred_element_type=F32) * inner_decay
        qd = (q.astype(F32) * q_decay).astype(BF16)
        lhs.append(jnp.concatenate([scores.astype(BF16), qd], axis=1))
        kd = (k.astype(F32) * k_decay).astype(BF16)
        kvs.append(lax.dot_general(kd, v, (((0,), (0,)), ((), ())), preferred_element_type=F32))
    state = jnp.zeros((RET_DK, RET_DV), F32)
    for c in range(n_chunks):
        rows = slice(c * C, (c + 1) * C)
        rhs = jnp.concatenate([v_ref[rows, :], state.astype(BF16)], axis=0)
        y = jnp.dot(lhs[c], rhs, preferred_element_type=F32)
        state = state * chunk_decay + kvs[c]
        half_rg = rg_ref[rows, :].astype(F32)
        silu = half_rg * jnp.tanh(half_rg) + half_rg
        y_ref[rows, :] = (silu * (_rms(y) * gn)).astype(BF16)


def _retention(zr, zg, ret_gn, dec):
    B, S, _ = zr.shape
    ts = S
    inner_decay, q_decay, k_decay, chunk_decay = dec
    head_tab = lambda shape: pl.BlockSpec((None,) + shape, lambda b, h, s: (h, 0, 0))
    return pl.pallas_call(
        functools.partial(_retention_kernel, n_chunks=ts // RET_CHUNK),
        grid=(B, RET_HEADS, S // ts),
        in_specs=[
            pl.BlockSpec((None, ts, RET_DK), lambda b, h, s: (b, s, h)),
            pl.BlockSpec((None, ts, RET_DK), lambda b, h, s: (b, s, RET_HEADS + h)),
            pl.BlockSpec((None, ts, RET_DV), lambda b, h, s: (b, s, 2 * RET_QK // RET_DV + h)),
            pl.BlockSpec((None, ts, RET_DV), lambda b, h, s: (b, s, h)),
            pl.BlockSpec((1, RET_DV), lambda b, h, s: (0, h)),
            head_tab((RET_CHUNK, RET_CHUNK)),
            head_tab((RET_CHUNK, RET_DK)),
            head_tab((RET_CHUNK, RET_DK)),
            head_tab((RET_DK, RET_DV)),
        ],
        out_specs=pl.BlockSpec((None, ts, RET_DV), lambda b, h, s: (b, s, h)),
        out_shape=jax.ShapeDtypeStruct((B, S, RET_V), BF16),
        compiler_params=pltpu.CompilerParams(
            dimension_semantics=("parallel", "parallel", "arbitrary"), vmem_limit_bytes=VMEM_LIMIT),
        name="retention",
    )(zr, zr, zr, zg, ret_gn, inner_decay, q_decay, k_decay, chunk_decay)


def _attn_unit(q, parts):
    nt = (((1,), (1,)), ((), ()))
    s = [lax.dot_general(q, k, nt, preferred_element_type=F32) + bias for k, _, bias in parts]
    s = s[0] if len(s) == 1 else jnp.concatenate(s, axis=1)
    m = jnp.max(s, axis=-1, keepdims=True)
    e = jnp.exp2(s - m).astype(BF16)
    ol, off = None, 0
    for _, v, _ in parts:
        n = v.shape[0]
        t = jnp.dot(e[:, off:off + n], jnp.concatenate([v, jnp.ones_like(v)], axis=1),
                    preferred_element_type=F32)
        ol = t if ol is None else ol + t
        off += n
    return ol[:, :ATT_HD], m, ol[:, ATT_HD:]


def _attention_kernel(q_ref, k_ref, v_ref, *rest, n_res, n_blk, has_prev):
    if has_prev:
        kp_ref, vp_ref, o_ref, st_ref = rest
    else:
        o_ref, st_ref = rest
    W = ATT_BLK
    neg = jnp.float32(-jnp.inf)
    row2 = lax.broadcasted_iota(jnp.int32, (W, 2 * W), 0)
    col2 = lax.broadcasted_iota(jnp.int32, (W, 2 * W), 1)
    band = jnp.where((col2 >= row2) & (col2 <= row2 + W), 0.0, neg)
    first = jnp.where(col2 <= row2, 0.0, neg)
    lane_half = lax.broadcasted_iota(jnp.int32, (W, LANES), 1) // STAT_LANES
    if has_prev:
        no_prev = W * (1 - jnp.minimum(pl.program_id(1), 1))
        prev_own = jnp.where((col2 >= row2) & (col2 <= row2 + W) & (col2 >= no_prev), 0.0, neg)

    for r in range(n_res):
        for b in range(n_blk):
            rows = slice(b * W, (b + 1) * W)
            st_tile = jnp.zeros((W, LANES), F32)
            for hh in range(ATT_HEADS):
                cols = slice(hh * ATT_HD, (hh + 1) * ATT_HD)
                q = q_ref[r, rows, cols]
                if b > 0:
                    keys = slice((b - 1) * W, (b + 1) * W)
                    parts = [(k_ref[r, keys, cols], v_ref[r, keys, cols], band)]
                elif has_prev:
                    parts = [(jnp.concatenate([kp_ref[0, :, cols], k_ref[r, rows, cols]], axis=0),
                              jnp.concatenate([vp_ref[0, :, cols], v_ref[r, rows, cols]], axis=0),
                              prev_own)]
                else:
                    keys = slice(0, 2 * W)
                    parts = [(k_ref[r, keys, cols], v_ref[r, keys, cols], first)]
                o, m, l = _attn_unit(q, parts)
                o_ref[r, rows, cols] = o.astype(BF16)
                st_tile = jnp.where(lane_half == 2 * hh, m, st_tile)
                st_tile = jnp.where(lane_half == 2 * hh + 1, l, st_tile)
            st_ref[r, rows, :] = st_tile


def _attention(za, g, rows_per_step=4096):
    window, dil = ATT_PATTERNS[g]
    assert window // dil == ATT_BLK
    batch, _, L, _ = za.shape
    has_prev = L > rows_per_step
    lc = min(L, rows_per_step)
    n_res = rows_per_step // lc
    n_blk = lc // ATT_BLK
    assert n_blk >= 2
    grid = (batch, L // lc if has_prev else dil // n_res)
    idx = (lambda b, j, seg: (b, 0, j, seg)) if has_prev else (lambda b, j, seg: (b, j, 0, seg))
    in_specs = [pl.BlockSpec((None, n_res, lc, ATT_W), functools.partial(idx, seg=seg))
                for seg in range(3)]
    operands = [za, za, za]
    if has_prev:
        in_specs += [pl.BlockSpec((None, 1, ATT_BLK, ATT_W),
                                  lambda b, j, seg=seg: (b, 0, jnp.maximum(j * n_blk - 1, 0), seg))
                     for seg in (1, 2)]
        operands += [za, za]
    out_specs = [pl.BlockSpec((None, n_res, lc, ATT_W), functools.partial(idx, seg=0)),
                 pl.BlockSpec((None, n_res, lc, LANES), functools.partial(idx, seg=0))]
    out_shape = [jax.ShapeDtypeStruct((batch, dil, L, ATT_W), BF16),
                 jax.ShapeDtypeStruct((batch, dil, L, LANES), F32)]
    return pl.pallas_call(
        functools.partial(_attention_kernel, n_res=n_res, n_blk=n_blk, has_prev=has_prev),
        grid=grid, in_specs=in_specs, out_specs=out_specs, out_shape=out_shape,
        compiler_params=pltpu.CompilerParams(
            dimension_semantics=("parallel", "parallel"), vmem_limit_bytes=VMEM_LIMIT),
        name=f"attention_g{g}",
    )(*operands)


def _merge_mlp_kernel(x_ref, y_ref, o0_ref, o1_ref, o2_ref, s0_ref, s1_ref, s2_ref, gr_ref, ga_ref,
                      bg_ref, wr_ref, wa_ref, wo_ref, p_ref, gm_ref, wu_ref, wd_ref, gp_ref, wpg_ref,
                      wpp_ref, out_ref, *perm_refs, ff_chunk):
    tm = x_ref.shape[0]

    def token_order(ref, scratch):
        dil = ref.shape[0]
        if dil == 1:
            return ref[0].astype(F32)
        n_slab = ref.shape[2] // LANES
        for r in range(dil):
            for c in range(n_slab):
                scratch[c, pl.ds(r, tm // dil, stride=dil), :] = (
                    ref[r, :, c * LANES:(c + 1) * LANES].astype(F32))
        return jnp.concatenate([scratch[c] for c in range(n_slab)], axis=1)

    o_tok = [token_order(o0_ref, None), token_order(o1_ref, perm_refs[0]),
             token_order(o2_ref, perm_refs[1])]
    stats = [token_order(s0_ref, None), token_order(s1_ref, perm_refs[2]),
             token_order(s2_ref, perm_refs[3])]
    sum_lane = (lax.broadcasted_iota(jnp.int32, (tm, LANES), 1) // STAT_LANES) % 2 == 1
    ms = stats
    lses = [st + pltpu.roll(jnp.log2(jnp.where(sum_lane, st, 1.0)), LANES - STAT_LANES, axis=1)
            for st in stats]
    top = jnp.maximum(jnp.maximum(lses[0], lses[1]), lses[2])
    den = jnp.exp2(lses[0] - top) + jnp.exp2(lses[1] - top) + jnp.exp2(lses[2] - top)
    wts = [jnp.exp2(m - top) / den for m in ms]
    per_head = LANES // ATT_HEADS
    parts = []
    for hh in range(ATT_HEADS):
        cols = slice(hh * ATT_HD, (hh + 1) * ATT_HD)
        acc = None
        for w, o_g in zip(wts, o_tok):
            t = w[:, hh * per_head:hh * per_head + 1] * o_g[:, cols]
            acc = t if acc is None else acc + t
        parts.append(acc)
    o = jnp.concatenate(parts, axis=1).astype(BF16)
    ret_branch = jnp.dot(y_ref[...], wr_ref[...], preferred_element_type=F32)
    att_branch = jnp.dot(o, wa_ref[...], preferred_element_type=F32)
    bg = bg_ref[...]
    gate_r = _sigmoid(gr_ref[...].astype(F32) + bg[:, :D_MODEL])
    gate_a = _sigmoid(ga_ref[...].astype(F32) + bg[:, D_MODEL:])
    mix = (gate_r * ret_branch + gate_a * att_branch).astype(BF16)
    x1 = x_ref[...] + jnp.dot(mix, wo_ref[...], preferred_element_type=F32)

    hn = (_rms(x1) * gm_ref[...]).astype(BF16)
    acc = x1
    for c in range(D_FF // ff_chunk):
        cols = slice(c * ff_chunk, (c + 1) * ff_chunk)
        u = jnp.dot(hn, wu_ref[:, cols], preferred_element_type=F32)
        a = jnp.square(jnp.maximum(u, 0.0)).astype(BF16)
        acc = acc + jnp.dot(a, wd_ref[cols, :], preferred_element_type=F32)
    hp = (_rms(acc) * gp_ref[...]).astype(BF16)
    gate_p = _sigmoid(jnp.dot(hp, wpg_ref[...], preferred_element_type=F32))
    proj = jnp.dot(p_ref[...].astype(BF16), wpp_ref[...], preferred_element_type=F32)
    out_ref[...] = acc + gate_p * proj


def _merge_mlp(x2, yg, os_, ss_, zg, b_gate, w_ret_out, w_att_out, w_o, p2, g_mlp, w_up, w_down,
               g_ple, w_pg, w_pp, tm=512, ff_chunk=1024):
    T = x2.shape[0]
    npos = os_[0].shape[1] * os_[0].shape[2] // tm
    row = lambda w: pl.BlockSpec((tm, w), lambda i: (i, 0))
    res_major = lambda a: pl.BlockSpec((None, a.shape[1], tm // a.shape[1], a.shape[3]),
                                       lambda i: (i // npos, 0, i % npos, 0))
    dilated = [a for a in os_ + ss_ if a.shape[1] > 1]
    return pl.pallas_call(
        functools.partial(_merge_mlp_kernel, ff_chunk=ff_chunk),
        grid=(T // tm,),
        in_specs=[
            row(D_MODEL), row(RET_V), *[res_major(a) for a in os_ + ss_],
            pl.BlockSpec((tm, D_MODEL), lambda i: (i, RET_V // D_MODEL)),
            pl.BlockSpec((tm, D_MODEL), lambda i: (i, RET_V // D_MODEL + 1)),
            _resident(b_gate.shape), _resident(w_ret_out.shape), _resident(w_att_out.shape),
            _resident(w_o.shape),
            row(PLE_DIM), _resident((1, D_MODEL)), _resident(w_up.shape), _resident(w_down.shape),
            _resident((1, D_MODEL)), _resident(w_pg.shape), _resident(w_pp.shape),
        ],
        out_specs=row(D_MODEL),
        out_shape=jax.ShapeDtypeStruct((T, D_MODEL), F32),
        scratch_shapes=[pltpu.VMEM((a.shape[3] // LANES, tm, LANES), F32) for a in dilated],
        compiler_params=pltpu.CompilerParams(
            dimension_semantics=("parallel",), vmem_limit_bytes=VMEM_LIMIT),
        name="merge_mlp",
    )(x2, yg, *os_, *ss_, zg, zg, b_gate, w_ret_out, w_att_out, w_o, p2, g_mlp, w_up, w_down, g_ple,
      w_pg, w_pp)


def _rot_tables(inv_freq, seq):
    ang = np.arange(seq, dtype=np.float64)[:, None] * inv_freq[None, :]
    cos, sin = np.cos(ang), np.sin(ang)
    return (np.concatenate([cos, cos], axis=1).astype(np.float32),
            np.concatenate([-sin, sin], axis=1).astype(np.float32))


def _decay_tables():
    H, C = RET_HEADS, RET_CHUNK
    log_g = np.log1p(-np.exp2(-5.0 - np.arange(H, dtype=np.float64)))
    idx = np.arange(C, dtype=np.float64)
    diff = idx[:, None] - idx[None, :]
    inner = np.where(diff >= 0, np.exp(log_g[:, None, None] * np.maximum(diff, 0.0)), 0.0)
    q_decay = np.exp(log_g[:, None] * (idx[None, :] + 1.0))
    k_decay = np.exp(log_g[:, None] * (C - 1.0 - idx[None, :]))
    chunk_decay = np.exp(log_g * C)
    f32 = lambda a: np.ascontiguousarray(a, dtype=np.float32)
    return (f32(inner),
            f32(np.broadcast_to(q_decay[:, :, None], (H, C, RET_DK))),
            f32(np.broadcast_to(k_decay[:, :, None], (H, C, RET_DK))),
            f32(np.broadcast_to(chunk_decay[:, None, None], (H, RET_DK, RET_DV))))


def _layer(x, p_i, w_in, b_gate, g_mix, q_gain, k_gain, ret_gn, w_ret_out, w_att_out, w_o,
           g_mlp, w_up, w_down, g_ple, w_ple_proj, w_ple_gate):
    B, S, D = x.shape
    T = B * S
    x2 = x.reshape(T, D)
    ret_freq = 1.0 / (10000.0 ** np.linspace(0.0, 1.0, RET_DK // 2))
    rope_freq = ROPE_THETA ** (-np.arange(0, ATT_HD, 2, dtype=np.float64) / ATT_HD)
    tabs = _rot_tables(ret_freq, S) + _rot_tables(rope_freq, S)

    zr, zg, za0, za1, za2, w_ret_out, w_att_out, w_o, w_up, w_down, w_ple_gate, w_ple_proj = _in_proj(
        x2, g_mix.reshape(1, D), w_in, tabs, q_gain, k_gain, S,
        [w_ret_out, w_att_out, w_o, w_up, w_down, w_ple_gate, w_ple_proj])
    zas = (za0, za1, za2)
    yg = _retention(zr.reshape(B, S, W_RET), zg.reshape(B, S, W_GATE), ret_gn.reshape(1, RET_V),
                    _decay_tables())
    att = [_attention(zas[g], g) for g in range(N_GROUPS)]
    out = _merge_mlp(x2, yg.reshape(T, RET_V), [o for o, _ in att], [s for _, s in att], zg,
                     b_gate.reshape(1, -1), w_ret_out, w_att_out, w_o, p_i.reshape(T, PLE_DIM),
                     g_mlp.reshape(1, D), w_up, w_down, g_ple.reshape(1, D), w_ple_gate, w_ple_proj)
    return out.reshape(B, S, D)


def kernel(x, p, w_in, b_gate, g_mix, q_gain, k_gain, ret_gn, w_ret_out, w_att_out, w_o, g_mlp, w_up,
           w_down, g_ple, w_ple_proj, w_ple_gate):
    for i in range(p.shape[0]):
        x = _layer(x, p[i], w_in[i], b_gate[i], g_mix[i], q_gain[i], k_gain[i], ret_gn[i],
                   w_ret_out[i], w_att_out[i], w_o[i], g_mlp[i], w_up[i], w_down[i], g_ple[i],
                   w_ple_proj[i], w_ple_gate[i])
    return x
```

```python
import functools

import jax
import jax.numpy as jnp
import numpy as np
from jax import lax
from jax.experimental import pallas as pl
from jax.experimental.pallas import tpu as pltpu

F32 = jnp.float32
BF16 = jnp.bfloat16

D_MODEL = 1024
PLE_DIM = 256
D_FF = 4 * D_MODEL
EPS = 1e-6
RET_HEADS = 4
RET_DK = 128
RET_DV = 256
RET_CHUNK = 128
RET_QK = RET_HEADS * RET_DK
RET_V = RET_HEADS * RET_DV
ATT_PATTERNS = ((128, 1), (512, 4), (2048, 16))
N_GROUPS = len(ATT_PATTERNS)
ATT_HEADS = 4
ATT_HD = 128
ATT_W = ATT_HEADS * ATT_HD
ATT_BLK = 128
ROPE_THETA = 10000.0
LOG2E = 1.4426950408889634
ATT_QSCALE = ATT_HD ** -0.5 * LOG2E

LANES = 128
SUBLANES = 8
STAT_LANES = LANES // (2 * ATT_HEADS)
SEG = 512
W_RET = 2 * RET_QK + RET_V
W_GATE = RET_V + 2 * D_MODEL
W_ATT = 3 * N_GROUPS * ATT_W
VMEM_LIMIT = 62 * 1024 * 1024
W_IN_STAGE_ROWS = 16
W_IN_STAGE_SLOTS = 4
BF16_ROWS = 16


def _resident(shape):
    return pl.BlockSpec(shape, lambda *_: (0,) * len(shape), pipeline_mode=pl.Buffered(1))


def _row_pitch(dil):
    return dil + SUBLANES if dil % (2 * SUBLANES) == 0 else dil


def _rms(x):
    return x * lax.rsqrt(jnp.mean(x * x, axis=-1, keepdims=True) + EPS)


def _sigmoid(x):
    return 0.5 * jnp.tanh(0.5 * x) + 0.5


def _rotate(a, cos, sin_signed):
    return a * cos + pltpu.roll(a, LANES // 2, axis=1) * sin_signed


def _load_w_in(w_hbm_ref, w_ref, stage_ref, sem):
    n_slots, rows, _ = stage_ref.shape
    n_blk = w_hbm_ref.shape[0] // rows
    rg = slice(W_RET, W_RET + RET_V)

    def copy(k):
        slot = k % n_slots
        return pltpu.make_async_copy(w_hbm_ref.at[pl.ds(k * rows, rows), :], stage_ref.at[slot],
                                     sem.at[slot])

    for k in range(n_slots - 1):
        copy(k).start()
    for k in range(n_blk):
        if k + n_slots - 1 < n_blk:
            copy(k + n_slots - 1).start()
        copy(k).wait()
        slot = k % n_slots
        dst = slice(k * rows, (k + 1) * rows)
        w_ref[dst, :rg.start] = stage_ref[slot, :, :rg.start].astype(BF16)
        w_ref[dst, rg] = (stage_ref[slot, :, rg] * 0.5).astype(BF16)
        w_ref[dst, rg.stop:] = stage_ref[slot, :, rg.stop:].astype(BF16)


def _inproj_kernel(x_ref, g_ref, w_hbm_ref, cr_ref, sr_ref, ca_ref, sa_ref, qg_ref, kg_ref, *rest):
    n_w = (len(rest) - 9) // 2
    zr_ref, zg_ref, *za_refs = rest[n_w:n_w + 5]
    perm_ref, w_ref, stage_ref, sem = rest[-4:]
    for w_f32_ref, w_bf16_ref in zip(rest[:n_w], rest[n_w + 5:-4]):
        w_bf16_ref[...] = w_f32_ref[...].astype(BF16)

    @pl.when(pl.program_id(0) == 0)
    def _():
        _load_w_in(w_hbm_ref, w_ref, stage_ref, sem)

    tm = x_ref.shape[0]
    n_slab = D_MODEL // LANES

    def residue_major(load, dil):
        return jnp.concatenate([load(pl.ds(r, tm // dil, stride=dil)) for r in range(dil)], axis=0)

    xn = _rms(x_ref[...]) * g_ref[...]
    for c in range(n_slab):
        perm_ref[c] = xn[:, c * LANES:(c + 1) * LANES]
    lhs, cos_a, sin_a = {}, {}, {}
    for dil in sorted({d for _, d in ATT_PATTERNS}):
        if dil == 1:
            lhs[dil], cos_a[dil], sin_a[dil] = xn.astype(BF16), ca_ref[...], sa_ref[...]
            continue
        lhs[dil] = jnp.concatenate(
            [residue_major(lambda rows, c=c: perm_ref[c, rows, :], dil) for c in range(n_slab)],
            axis=1).astype(BF16)
        cos_a[dil] = residue_major(lambda rows: ca_ref[rows, :], dil)
        sin_a[dil] = residue_major(lambda rows: sa_ref[rows, :], dil)
    h = lhs[1]

    def store_att(g, seg, val):
        dil = ATT_PATTERNS[g][1]
        za_refs[g][:, :, seg * ATT_W:(seg + 1) * ATT_W] = (
            val.astype(BF16).reshape(dil, tm // dil, ATT_W))

    n_ret, n_rg, n_att = W_RET // SEG, RET_V // SEG, W_ATT // SEG
    n_all = (W_RET + W_GATE + W_ATT) // SEG
    heavy = [0, 1] + list(range(n_ret + n_rg, n_ret + n_rg + n_att))
    for j in heavy + [j for j in range(n_all) if j not in heavy]:
        w_chunk = w_ref[:, j * SEG:(j + 1) * SEG]
        if n_ret + n_rg <= j < n_ret + n_rg + n_att:
            jj = j - n_ret - n_rg
            seg, g = jj // N_GROUPS, jj % N_GROUPS
            dil = ATT_PATTERNS[g][1]
            acc = jnp.dot(lhs[dil], w_chunk, preferred_element_type=F32)
            if seg < 2:
                gain = (qg_ref if seg == 0 else kg_ref)[g:g + 1, :]
                if seg == 0:
                    gain = gain * ATT_QSCALE
                parts = []
                for hh in range(SEG // LANES):
                    a = _rms(acc[:, hh * LANES:(hh + 1) * LANES]) * gain
                    parts.append(_rotate(a, cos_a[dil], sin_a[dil]))
                store_att(g, seg, jnp.concatenate(parts, axis=1))
            else:
                store_att(g, seg, acc)
            continue
        acc = jnp.dot(h, w_chunk, preferred_element_type=F32)
        if j < 2:
            cos, sin = cr_ref[...], sr_ref[...]
            for hh in range(SEG // LANES):
                r = _rotate(acc[:, hh * LANES:(hh + 1) * LANES], cos, sin)
                if j == 1:
                    r = r * (RET_DK ** -0.5)
                zr_ref[:, j * SEG + hh * LANES:j * SEG + (hh + 1) * LANES] = r.astype(BF16)
        elif j < n_ret:
            zr_ref[:, j * SEG:(j + 1) * SEG] = acc.astype(BF16)
        elif j < n_ret + n_rg:
            jj = j - n_ret
            zg_ref[:, jj * SEG:(jj + 1) * SEG] = acc.astype(BF16)
        else:
            jj = j - n_ret - n_att
            zg_ref[:, jj * SEG:(jj + 1) * SEG] = acc.astype(BF16)


def _in_proj(x2, g_mix, w_in, tabs, q_gain, k_gain, seq, weights, tm=512):
    T = x2.shape[0]
    npos = seq // tm
    tab_spec = pl.BlockSpec((tm, LANES), lambda i: (i % npos, 0))

    def slice_spec(w):
        rows = max(w.shape[0] // (T // tm), BF16_ROWS)
        reps = rows * (T // tm) // w.shape[0]
        return pl.BlockSpec((rows, w.shape[1]), lambda i: (i // reps, 0))

    w_specs = [slice_spec(w) for w in weights]
    return pl.pallas_call(
        _inproj_kernel,
        grid=(T // tm,),
        in_specs=[
            pl.BlockSpec((tm, D_MODEL), lambda i: (i, 0)),
            _resident((1, D_MODEL)),
            pl.BlockSpec(memory_space=pl.ANY),
            tab_spec, tab_spec, tab_spec, tab_spec,
            _resident(q_gain.shape), _resident(k_gain.shape),
        ] + w_specs,
        out_specs=[
            pl.BlockSpec((tm, W_RET), lambda i: (i, 0)),
            pl.BlockSpec((tm, W_GATE), lambda i: (i, 0)),
        ] + [
            pl.BlockSpec((None, dil, tm // dil, 3 * ATT_W), lambda i: (i // npos, 0, i % npos, 0))
            for _, dil in ATT_PATTERNS
        ] + w_specs,
        out_shape=[
            jax.ShapeDtypeStruct((T, W_RET), BF16),
            jax.ShapeDtypeStruct((T, W_GATE), BF16),
        ] + [
            jax.ShapeDtypeStruct((T // seq, dil, seq // dil, 3 * ATT_W), BF16)
            for _, dil in ATT_PATTERNS
        ] + [jax.ShapeDtypeStruct(w.shape, BF16) for w in weights],
        scratch_shapes=[pltpu.VMEM((D_MODEL // LANES, tm, LANES), F32),
                        pltpu.VMEM(w_in.shape, BF16),
                        pltpu.VMEM((W_IN_STAGE_SLOTS, W_IN_STAGE_ROWS, w_in.shape[1]), F32),
                        pltpu.SemaphoreType.DMA((W_IN_STAGE_SLOTS,))],
        compiler_params=pltpu.CompilerParams(
            dimension_semantics=("arbitrary",), vmem_limit_bytes=VMEM_LIMIT),
        name="in_proj",
    )(x2, g_mix, w_in, *tabs, q_gain, k_gain, *weights)


def _retention_kernel(q_ref, k_ref, v_ref, rg_ref, gn_ref, inner_ref, qd_ref, kd_ref, cd_ref,
                      y_ref, *, n_chunks):
    C = RET_CHUNK
    inner_decay, q_decay, k_decay, chunk_decay = inner_ref[...], qd_ref[...], kd_ref[...], cd_ref[...]
    gn = gn_ref[...]
    lhs, kvs = [], []
    for c in range(n_chunks):
        rows = slice(c * C, (c + 1) * C)
        q, k, v = q_ref[rows, :], k_ref[rows, :], v_ref[rows, :]
        scores = lax.dot_general(q, k, (((1,), (1,)), ((), ())), preferred_element_type=F32) * inner_decay
        qd = (q.astype(F32) * q_decay).astype(BF16)
        lhs.append(jnp.concatenate([scores.astype(BF16), qd], axis=1))
        kd = (k.astype(F32) * k_decay).astype(BF16)
        kvs.append(lax.dot_general(kd, v, (((0,), (0,)), ((), ())), preferred_element_type=F32))
    state = jnp.zeros((RET_DK, RET_DV), F32)
    for c in range(n_chunks):
        rows = slice(c * C, (c + 1) * C)
        rhs = jnp.concatenate([v_ref[rows, :], state.astype(BF16)], axis=0)
        y = jnp.dot(lhs[c], rhs, preferred_element_type=F32)
        state = state * chunk_decay + kvs[c]
        half_rg = rg_ref[rows, :].astype(F32)
        silu = half_rg * jnp.tanh(half_rg) + half_rg
        y_ref[rows, :] = (silu * (_rms(y) * gn)).astype(BF16)


def _retention(zr, zg, ret_gn, dec):
    B, S, _ = zr.shape
    ts = S
    inner_decay, q_decay, k_decay, chunk_decay = dec
    head_tab = lambda shape: pl.BlockSpec((None,) + shape, lambda b, h, s: (h, 0, 0))
    return pl.pallas_call(
        functools.partial(_retention_kernel, n_chunks=ts // RET_CHUNK),
        grid=(B, RET_HEADS, S // ts),
        in_specs=[
            pl.BlockSpec((None, ts, RET_DK), lambda b, h, s: (b, s, h)),
            pl.BlockSpec((None, ts, RET_DK), lambda b, h, s: (b, s, RET_HEADS + h)),
            pl.BlockSpec((None, ts, RET_DV), lambda b, h, s: (b, s, 2 * RET_QK // RET_DV + h)),
            pl.BlockSpec((None, ts, RET_DV), lambda b, h, s: (b, s, h)),
            pl.BlockSpec((1, RET_DV), lambda b, h, s: (0, h)),
            head_tab((RET_CHUNK, RET_CHUNK)),
            head_tab((RET_CHUNK, RET_DK)),
            head_tab((RET_CHUNK, RET_DK)),
            head_tab((RET_DK, RET_DV)),
        ],
        out_specs=pl.BlockSpec((None, ts, RET_DV), lambda b, h, s: (b, s, h)),
        out_shape=jax.ShapeDtypeStruct((B, S, RET_V), BF16),
        compiler_params=pltpu.CompilerParams(
            dimension_semantics=("parallel", "parallel", "arbitrary"), vmem_limit_bytes=VMEM_LIMIT),
        name="retention",
    )(zr, zr, zr, zg, ret_gn, inner_decay, q_decay, k_decay, chunk_decay)


def _attn_unit(q, parts):
    nt = (((1,), (1,)), ((), ()))
    s = [lax.dot_general(q, k, nt, preferred_element_type=F32) + bias for k, _, bias in parts]
    s = s[0] if len(s) == 1 else jnp.concatenate(s, axis=1)
    m = jnp.max(s, axis=-1, keepdims=True)
    e = jnp.exp2(s - m).astype(BF16)
    ol, off = None, 0
    for _, v, _ in parts:
        n = v.shape[0]
        t = jnp.dot(e[:, off:off + n], jnp.concatenate([v, jnp.ones_like(v)], axis=1),
                    preferred_element_type=F32)
        ol = t if ol is None else ol + t
        off += n
    return ol[:, :ATT_HD], m, ol[:, ATT_HD:]


def _attention_kernel(q_ref, k_ref, v_ref, *rest, n_res, n_blk, has_prev):
    if has_prev:
        kp_ref, vp_ref, o_ref, st_ref = rest
    else:
        o_ref, st_ref = rest
    W = ATT_BLK
    neg = jnp.float32(-jnp.inf)
    row2 = lax.broadcasted_iota(jnp.int32, (W, 2 * W), 0)
    col2 = lax.broadcasted_iota(jnp.int32, (W, 2 * W), 1)
    band = jnp.where((col2 >= row2) & (col2 <= row2 + W), 0.0, neg)
    first = jnp.where(col2 <= row2, 0.0, neg)
    lane_half = lax.broadcasted_iota(jnp.int32, (W, LANES), 1) // STAT_LANES
    if has_prev:
        no_prev = W * (1 - jnp.minimum(pl.program_id(1), 1))
        prev_own = jnp.where((col2 >= row2) & (col2 <= row2 + W) & (col2 >= no_prev), 0.0, neg)

    for r in range(n_res):
        for b in range(n_blk):
            rows = slice(b * W, (b + 1) * W)
            st_tile = jnp.zeros((W, LANES), F32)
            for hh in range(ATT_HEADS):
                cols = slice(hh * ATT_HD, (hh + 1) * ATT_HD)
                q = q_ref[r, rows, cols]
                if b > 0:
                    keys = slice((b - 1) * W, (b + 1) * W)
                    parts = [(k_ref[r, keys, cols], v_ref[r, keys, cols], band)]
                elif has_prev:
                    parts = [(jnp.concatenate([kp_ref[0, :, cols], k_ref[r, rows, cols]], axis=0),
                              jnp.concatenate([vp_ref[0, :, cols], v_ref[r, rows, cols]], axis=0),
                              prev_own)]
                else:
                    keys = slice(0, 2 * W)
                    parts = [(k_ref[r, keys, cols], v_ref[r, keys, cols], first)]
                o, m, l = _attn_unit(q, parts)
                o_ref[r, rows, cols] = o.astype(BF16)
                st_tile = jnp.where(lane_half == 2 * hh, m, st_tile)
                st_tile = jnp.where(lane_half == 2 * hh + 1, l, st_tile)
            st_ref[r, rows, :] = st_tile


def _attention(za, g, rows_per_step=4096):
    window, dil = ATT_PATTERNS[g]
    assert window // dil == ATT_BLK
    batch, _, L, _ = za.shape
    has_prev = L > rows_per_step
    lc = min(L, rows_per_step)
    n_res = rows_per_step // lc
    n_blk = lc // ATT_BLK
    assert n_blk >= 2
    grid = (batch, L // lc if has_prev else dil // n_res)
    idx = (lambda b, j, seg: (b, 0, j, seg)) if has_prev else (lambda b, j, seg: (b, j, 0, seg))
    in_specs = [pl.BlockSpec((None, n_res, lc, ATT_W), functools.partial(idx, seg=seg))
                for seg in range(3)]
    operands = [za, za, za]
    if has_prev:
        in_specs += [pl.BlockSpec((None, 1, ATT_BLK, ATT_W),
                                  lambda b, j, seg=seg: (b, 0, jnp.maximum(j * n_blk - 1, 0), seg))
                     for seg in (1, 2)]
        operands += [za, za]
    out_specs = [pl.BlockSpec((None, n_res, lc, ATT_W), functools.partial(idx, seg=0)),
                 pl.BlockSpec((None, n_res, lc, LANES), functools.partial(idx, seg=0))]
    out_shape = [jax.ShapeDtypeStruct((batch, dil, L, ATT_W), BF16),
                 jax.ShapeDtypeStruct((batch, dil, L, LANES), F32)]
    return pl.pallas_call(
        functools.partial(_attention_kernel, n_res=n_res, n_blk=n_blk, has_prev=has_prev),
        grid=grid, in_specs=in_specs, out_specs=out_specs, out_shape=out_shape,
        compiler_params=pltpu.CompilerParams(
            dimension_semantics=("parallel", "parallel"), vmem_limit_bytes=VMEM_LIMIT),
        name=f"attention_g{g}",
    )(*operands)


def _merge_mlp_kernel(x_ref, y_ref, o0_ref, o1_ref, o2_ref, s0_ref, s1_ref, s2_ref, gr_ref, ga_ref,
                      bg_ref, wr_ref, wa_ref, wo_ref, p_ref, gm_ref, wu_ref, wd_ref, gp_ref, wpg_ref,
                      wpp_ref, out_ref, *perm_refs, ff_chunk):
    tm = x_ref.shape[0]

    def token_order(ref, scratch):
        dil = ref.shape[0]
        if dil == 1:
            return ref[0].astype(F32)
        n_slab = ref.shape[2] // LANES
        pitch = _row_pitch(dil)
        for r in range(dil):
            for c in range(n_slab):
                scratch[c, pl.ds(r, tm // dil, stride=pitch), :] = (
                    ref[r, :, c * LANES:(c + 1) * LANES].astype(F32))
        if pitch == dil:
            return jnp.concatenate([scratch[c] for c in range(n_slab)], axis=1)
        return jnp.concatenate(
            [jnp.concatenate([scratch[c, l * pitch:l * pitch + dil, :] for c in range(n_slab)], axis=1)
             for l in range(tm // dil)], axis=0)

    proj = jnp.dot(p_ref[...].astype(BF16), wpp_ref[...], preferred_element_type=F32)
    ret_branch = jnp.dot(y_ref[...], wr_ref[...], preferred_element_type=F32)
    bg = bg_ref[...]
    gate_r = _sigmoid(gr_ref[...].astype(F32) + bg[:, :D_MODEL])
    gate_a = _sigmoid(ga_ref[...].astype(F32) + bg[:, D_MODEL:])
    o_tok = [token_order(o0_ref, None), token_order(o1_ref, perm_refs[0]),
             token_order(o2_ref, perm_refs[1])]
    stats = [token_order(s0_ref, None), token_order(s1_ref, perm_refs[2]),
             token_order(s2_ref, perm_refs[3])]
    sum_lane = (lax.broadcasted_iota(jnp.int32, (tm, LANES), 1) // STAT_LANES) % 2 == 1
    ms = stats
    lses = [st + pltpu.roll(jnp.log2(jnp.where(sum_lane, st, 1.0)), LANES - STAT_LANES, axis=1)
            for st in stats]
    top = jnp.maximum(jnp.maximum(lses[0], lses[1]), lses[2])
    den = jnp.exp2(lses[0] - top) + jnp.exp2(lses[1] - top) + jnp.exp2(lses[2] - top)
    wts = [jnp.exp2(m - top) / den for m in ms]
    per_head = LANES // ATT_HEADS
    parts = []
    for hh in range(ATT_HEADS):
        cols = slice(hh * ATT_HD, (hh + 1) * ATT_HD)
        acc = None
        for w, o_g in zip(wts, o_tok):
            t = w[:, hh * per_head:hh * per_head + 1] * o_g[:, cols]
            acc = t if acc is None else acc + t
        parts.append(acc)
    o = jnp.concatenate(parts, axis=1).astype(BF16)
    att_branch = jnp.dot(o, wa_ref[...], preferred_element_type=F32)
    mix = (gate_r * ret_branch + gate_a * att_branch).astype(BF16)
    x1 = x_ref[...] + jnp.dot(mix, wo_ref[...], preferred_element_type=F32)

    hn = (_rms(x1) * gm_ref[...]).astype(BF16)
    acc = x1
    for c in range(D_FF // ff_chunk):
        cols = slice(c * ff_chunk, (c + 1) * ff_chunk)
        u = jnp.dot(hn, wu_ref[:, cols], preferred_element_type=F32)
        a = jnp.square(jnp.maximum(u, 0.0)).astype(BF16)
        acc = acc + jnp.dot(a, wd_ref[cols, :], preferred_element_type=F32)
    hp = (_rms(acc) * gp_ref[...]).astype(BF16)
    gate_p = _sigmoid(jnp.dot(hp, wpg_ref[...], preferred_element_type=F32))
    out_ref[...] = acc + gate_p * proj


def _merge_mlp(x2, yg, os_, ss_, zg, b_gate, w_ret_out, w_att_out, w_o, p2, g_mlp, w_up, w_down,
               g_ple, w_pg, w_pp, tm=512, ff_chunk=1024):
    T = x2.shape[0]
    npos = os_[0].shape[1] * os_[0].shape[2] // tm
    row = lambda w: pl.BlockSpec((tm, w), lambda i: (i, 0))
    res_major = lambda a: pl.BlockSpec((None, a.shape[1], tm // a.shape[1], a.shape[3]),
                                       lambda i: (i // npos, 0, i % npos, 0))
    dilated = [a for a in os_ + ss_ if a.shape[1] > 1]
    return pl.pallas_call(
        functools.partial(_merge_mlp_kernel, ff_chunk=ff_chunk),
        grid=(T // tm,),
        in_specs=[
            row(D_MODEL), row(RET_V), *[res_major(a) for a in os_ + ss_],
            pl.BlockSpec((tm, D_MODEL), lambda i: (i, RET_V // D_MODEL)),
            pl.BlockSpec((tm, D_MODEL), lambda i: (i, RET_V // D_MODEL + 1)),
            _resident(b_gate.shape), _resident(w_ret_out.shape), _resident(w_att_out.shape),
            _resident(w_o.shape),
            row(PLE_DIM), _resident((1, D_MODEL)), _resident(w_up.shape), _resident(w_down.shape),
            _resident((1, D_MODEL)), _resident(w_pg.shape), _resident(w_pp.shape),
        ],
        out_specs=row(D_MODEL),
        out_shape=jax.ShapeDtypeStruct((T, D_MODEL), F32),
        scratch_shapes=[pltpu.VMEM((a.shape[3] // LANES, tm // a.shape[1] * _row_pitch(a.shape[1]), LANES),
                                   F32) for a in dilated],
        compiler_params=pltpu.CompilerParams(
            dimension_semantics=("parallel",), vmem_limit_bytes=VMEM_LIMIT),
        name="merge_mlp",
    )(x2, yg, *os_, *ss_, zg, zg, b_gate, w_ret_out, w_att_out, w_o, p2, g_mlp, w_up, w_down, g_ple,
      w_pg, w_pp)


def _rot_tables(inv_freq, seq):
    ang = np.arange(seq, dtype=np.float64)[:, None] * inv_freq[None, :]
    cos, sin = np.cos(ang), np.sin(ang)
    return (np.concatenate([cos, cos], axis=1).astype(np.float32),
            np.concatenate([-sin, sin], axis=1).astype(np.float32))


def _decay_tables():
    H, C = RET_HEADS, RET_CHUNK
    log_g = np.log1p(-np.exp2(-5.0 - np.arange(H, dtype=np.float64)))
    idx = np.arange(C, dtype=np.float64)
    diff = idx[:, None] - idx[None, :]
    inner = np.where(diff >= 0, np.exp(log_g[:, None, None] * np.maximum(diff, 0.0)), 0.0)
    q_decay = np.exp(log_g[:, None] * (idx[None, :] + 1.0))
    k_decay = np.exp(log_g[:, None] * (C - 1.0 - idx[None, :]))
    chunk_decay = np.exp(log_g * C)
    f32 = lambda a: np.ascontiguousarray(a, dtype=np.float32)
    return (f32(inner),
            f32(np.broadcast_to(q_decay[:, :, None], (H, C, RET_DK))),
            f32(np.broadcast_to(k_decay[:, :, None], (H, C, RET_DK))),
            f32(np.broadcast_to(chunk_decay[:, None, None], (H, RET_DK, RET_DV))))


def _layer(x, p_i, w_in, b_gate, g_mix, q_gain, k_gain, ret_gn, w_ret_out, w_att_out, w_o,
           g_mlp, w_up, w_down, g_ple, w_ple_proj, w_ple_gate):
    B, S, D = x.shape
    T = B * S
    x2 = x.reshape(T, D)
    ret_freq = 1.0 / (10000.0 ** np.linspace(0.0, 1.0, RET_DK // 2))
    rope_freq = ROPE_THETA ** (-np.arange(0, ATT_HD, 2, dtype=np.float64) / ATT_HD)
    tabs = _rot_tables(ret_freq, S) + _rot_tables(rope_freq, S)

    zr, zg, za0, za1, za2, w_ret_out, w_att_out, w_o, w_up, w_down, w_ple_gate, w_ple_proj = _in_proj(
        x2, g_mix.reshape(1, D), w_in, tabs, q_gain, k_gain, S,
        [w_ret_out, w_att_out, w_o, w_up, w_down, w_ple_gate, w_ple_proj])
    zas = (za0, za1, za2)
    yg = _retention(zr.reshape(B, S, W_RET), zg.reshape(B, S, W_GATE), ret_gn.reshape(1, RET_V),
                    _decay_tables())
    att = [_attention(zas[g], g) for g in range(N_GROUPS)]
    out = _merge_mlp(x2, yg.reshape(T, RET_V), [o for o, _ in att], [s for _, s in att], zg,
                     b_gate.reshape(1, -1), w_ret_out, w_att_out, w_o, p_i.reshape(T, PLE_DIM),
                     g_mlp.reshape(1, D), w_up, w_down, g_ple.reshape(1, D), w_ple_gate, w_ple_proj)
    return out.reshape(B, S, D)


def kernel(x, p, w_in, b_gate, g_mix, q_gain, k_gain, ret_gn, w_ret_out, w_att_out, w_o, g_mlp, w_up,
           w_down, g_ple, w_ple_proj, w_ple_gate):
    for i in range(p.shape[0]):
        x = _layer(x, p[i], w_in[i], b_gate[i], g_mix[i], q_gain[i], k_gain[i], ret_gn[i],
                   w_ret_out[i], w_att_out[i], w_o[i], g_mlp[i], w_up[i], w_down[i], g_ple[i],
                   w_ple_proj[i], w_ple_gate[i])
    return x
```

```python
import functools

import jax
import jax.numpy as jnp
import numpy as np
from jax import lax
from jax.experimental import pallas as pl
from jax.experimental.pallas import tpu as pltpu

F32 = jnp.float32
BF16 = jnp.bfloat16

D_MODEL = 1024
PLE_DIM = 256
D_FF = 4 * D_MODEL
EPS = 1e-6
RET_HEADS = 4
RET_DK = 128
RET_DV = 256
RET_CHUNK = 128
RET_QK = RET_HEADS * RET_DK
RET_V = RET_HEADS * RET_DV
ATT_PATTERNS = ((128, 1), (512, 4), (2048, 16))
N_GROUPS = len(ATT_PATTERNS)
ATT_HEADS = 4
ATT_HD = 128
ATT_W = ATT_HEADS * ATT_HD
ATT_BLK = 128
ROPE_THETA = 10000.0
LOG2E = 1.4426950408889634
ATT_QSCALE = ATT_HD ** -0.5 * LOG2E

LANES = 128
SUBLANES = 8
STAT_LANES = LANES // (2 * ATT_HEADS)
SEG = 512
W_RET = 2 * RET_QK + RET_V
W_GATE = RET_V + 2 * D_MODEL
W_ATT = 3 * N_GROUPS * ATT_W
VMEM_LIMIT = 62 * 1024 * 1024
W_IN_STAGE_ROWS = 16
W_IN_STAGE_SLOTS = 4
ATT_IN_BUFFERS = 3
BF16_ROWS = 16


def _resident(shape):
    return pl.BlockSpec(shape, lambda *_: (0,) * len(shape), pipeline_mode=pl.Buffered(1))


def _row_pitch(dil):
    return dil + SUBLANES if dil % (2 * SUBLANES) == 0 else dil


def _rms(x):
    return x * lax.rsqrt(jnp.mean(x * x, axis=-1, keepdims=True) + EPS)


def _sigmoid(x):
    return 0.5 * jnp.tanh(0.5 * x) + 0.5


def _rotate(a, cos, sin_signed):
    return a * cos + pltpu.roll(a, LANES // 2, axis=1) * sin_signed


def _load_w_in(w_hbm_ref, w_ref, stage_ref, sem):
    n_slots, rows, _ = stage_ref.shape
    n_blk = w_hbm_ref.shape[0] // rows
    rg = slice(W_RET, W_RET + RET_V)

    def copy(k):
        slot = k % n_slots
        return pltpu.make_async_copy(w_hbm_ref.at[pl.ds(k * rows, rows), :], stage_ref.at[slot],
                                     sem.at[slot])

    for k in range(n_slots - 1):
        copy(k).start()
    for k in range(n_blk):
        if k + n_slots - 1 < n_blk:
            copy(k + n_slots - 1).start()
        copy(k).wait()
        slot = k % n_slots
        dst = slice(k * rows, (k + 1) * rows)
        w_ref[dst, :rg.start] = stage_ref[slot, :, :rg.start].astype(BF16)
        w_ref[dst, rg] = (stage_ref[slot, :, rg] * 0.5).astype(BF16)
        w_ref[dst, rg.stop:] = stage_ref[slot, :, rg.stop:].astype(BF16)


def _inproj_kernel(x_ref, g_ref, w_hbm_ref, cr_ref, sr_ref, ca_ref, sa_ref, qg_ref, kg_ref, *rest):
    n_w = (len(rest) - 9) // 2
    zr_ref, zg_ref, *za_refs = rest[n_w:n_w + 5]
    perm_ref, w_ref, stage_ref, sem = rest[-4:]
    for w_f32_ref, w_bf16_ref in zip(rest[:n_w], rest[n_w + 5:-4]):
        w_bf16_ref[...] = w_f32_ref[...].astype(BF16)

    @pl.when(pl.program_id(0) == 0)
    def _():
        _load_w_in(w_hbm_ref, w_ref, stage_ref, sem)

    tm = x_ref.shape[0]
    n_slab = D_MODEL // LANES

    def residue_major(load, dil):
        return jnp.concatenate([load(pl.ds(r, tm // dil, stride=dil)) for r in range(dil)], axis=0)

    xn = _rms(x_ref[...]) * g_ref[...]
    for c in range(n_slab):
        perm_ref[c] = xn[:, c * LANES:(c + 1) * LANES]
    lhs, cos_a, sin_a = {}, {}, {}
    for dil in sorted({d for _, d in ATT_PATTERNS}):
        if dil == 1:
            lhs[dil], cos_a[dil], sin_a[dil] = xn.astype(BF16), ca_ref[...], sa_ref[...]
            continue
        lhs[dil] = jnp.concatenate(
            [residue_major(lambda rows, c=c: perm_ref[c, rows, :], dil) for c in range(n_slab)],
            axis=1).astype(BF16)
        cos_a[dil] = residue_major(lambda rows: ca_ref[rows, :], dil)
        sin_a[dil] = residue_major(lambda rows: sa_ref[rows, :], dil)
    h = lhs[1]

    def store_att(g, seg, val):
        dil = ATT_PATTERNS[g][1]
        za_refs[g][:, :, seg * ATT_W:(seg + 1) * ATT_W] = (
            val.astype(BF16).reshape(dil, tm // dil, ATT_W))

    n_ret, n_rg, n_att = W_RET // SEG, RET_V // SEG, W_ATT // SEG
    n_all = (W_RET + W_GATE + W_ATT) // SEG
    heavy = [0, 1] + list(range(n_ret + n_rg, n_ret + n_rg + n_att))
    for j in heavy + [j for j in range(n_all) if j not in heavy]:
        w_chunk = w_ref[:, j * SEG:(j + 1) * SEG]
        if n_ret + n_rg <= j < n_ret + n_rg + n_att:
            jj = j - n_ret - n_rg
            seg, g = jj // N_GROUPS, jj % N_GROUPS
            dil = ATT_PATTERNS[g][1]
            acc = jnp.dot(lhs[dil], w_chunk, preferred_element_type=F32)
            if seg < 2:
                gain = (qg_ref if seg == 0 else kg_ref)[g:g + 1, :]
                if seg == 0:
                    gain = gain * ATT_QSCALE
                parts = []
                for hh in range(SEG // LANES):
                    a = _rms(acc[:, hh * LANES:(hh + 1) * LANES]) * gain
                    parts.append(_rotate(a, cos_a[dil], sin_a[dil]))
                store_att(g, seg, jnp.concatenate(parts, axis=1))
            else:
                store_att(g, seg, acc)
            continue
        acc = jnp.dot(h, w_chunk, preferred_element_type=F32)
        if j < 2:
            cos, sin = cr_ref[...], sr_ref[...]
            for hh in range(SEG // LANES):
                r = _rotate(acc[:, hh * LANES:(hh + 1) * LANES], cos, sin)
                if j == 1:
                    r = r * (RET_DK ** -0.5)
                zr_ref[:, j * SEG + hh * LANES:j * SEG + (hh + 1) * LANES] = r.astype(BF16)
        elif j < n_ret:
            zr_ref[:, j * SEG:(j + 1) * SEG] = acc.astype(BF16)
        elif j < n_ret + n_rg:
            jj = j - n_ret
            zg_ref[:, jj * SEG:(jj + 1) * SEG] = acc.astype(BF16)
        else:
            jj = j - n_ret - n_att
            zg_ref[:, jj * SEG:(jj + 1) * SEG] = acc.astype(BF16)


def _in_proj(x2, g_mix, w_in, tabs, q_gain, k_gain, seq, weights, tm=512):
    T = x2.shape[0]
    npos = seq // tm
    tab_spec = pl.BlockSpec((tm, LANES), lambda i: (i % npos, 0))

    def slice_spec(w):
        rows = max(w.shape[0] // (T // tm), BF16_ROWS)
        reps = rows * (T // tm) // w.shape[0]
        return pl.BlockSpec((rows, w.shape[1]), lambda i: (i // reps, 0))

    w_specs = [slice_spec(w) for w in weights]
    return pl.pallas_call(
        _inproj_kernel,
        grid=(T // tm,),
        in_specs=[
            pl.BlockSpec((tm, D_MODEL), lambda i: (i, 0)),
            _resident((1, D_MODEL)),
            pl.BlockSpec(memory_space=pl.ANY),
            tab_spec, tab_spec, tab_spec, tab_spec,
            _resident(q_gain.shape), _resident(k_gain.shape),
        ] + w_specs,
        out_specs=[
            pl.BlockSpec((tm, W_RET), lambda i: (i, 0)),
            pl.BlockSpec((tm, W_GATE), lambda i: (i, 0)),
        ] + [
            pl.BlockSpec((None, dil, tm // dil, 3 * ATT_W), lambda i: (i // npos, 0, i % npos, 0))
            for _, dil in ATT_PATTERNS
        ] + w_specs,
        out_shape=[
            jax.ShapeDtypeStruct((T, W_RET), BF16),
            jax.ShapeDtypeStruct((T, W_GATE), BF16),
        ] + [
            jax.ShapeDtypeStruct((T // seq, dil, seq // dil, 3 * ATT_W), BF16)
            for _, dil in ATT_PATTERNS
        ] + [jax.ShapeDtypeStruct(w.shape, BF16) for w in weights],
        scratch_shapes=[pltpu.VMEM((D_MODEL // LANES, tm, LANES), F32),
                        pltpu.VMEM(w_in.shape, BF16),
                        pltpu.VMEM((W_IN_STAGE_SLOTS, W_IN_STAGE_ROWS, w_in.shape[1]), F32),
                        pltpu.SemaphoreType.DMA((W_IN_STAGE_SLOTS,))],
        compiler_params=pltpu.CompilerParams(
            dimension_semantics=("arbitrary",), vmem_limit_bytes=VMEM_LIMIT),
        name="in_proj",
    )(x2, g_mix, w_in, *tabs, q_gain, k_gain, *weights)


def _retention_kernel(q_ref, k_ref, v_ref, rg_ref, gn_ref, inner_ref, qd_ref, kd_ref, cd_ref,
                      y_ref, *, n_chunks):
    C = RET_CHUNK
    inner_decay, q_decay, k_decay, chunk_decay = inner_ref[...], qd_ref[...], kd_ref[...], cd_ref[...]
    gn = gn_ref[...]
    lhs, kvs = [], []
    for c in range(n_chunks):
        rows = slice(c * C, (c + 1) * C)
        q, k, v = q_ref[rows, :], k_ref[rows, :], v_ref[rows, :]
        scores = lax.dot_general(q, k, (((1,), (1,)), ((), ())), preferred_element_type=F32) * inner_decay
        qd = (q.astype(F32) * q_decay).astype(BF16)
        lhs.append(jnp.concatenate([scores.astype(BF16), qd], axis=1))
        kd = (k.astype(F32) * k_decay).astype(BF16)
        kvs.append(lax.dot_general(kd, v, (((0,), (0,)), ((), ())), preferred_element_type=F32))
    state = jnp.zeros((RET_DK, RET_DV), F32)
    for c in range(n_chunks):
        rows = slice(c * C, (c + 1) * C)
        rhs = jnp.concatenate([v_ref[rows, :], state.astype(BF16)], axis=0)
        y = jnp.dot(lhs[c], rhs, preferred_element_type=F32)
        state = state * chunk_decay + kvs[c]
        half_rg = rg_ref[rows, :].astype(F32)
        silu = half_rg * jnp.tanh(half_rg) + half_rg
        y_ref[rows, :] = (silu * (_rms(y) * gn)).astype(BF16)


def _retention(zr, zg, ret_gn, dec):
    B, S, _ = zr.shape
    ts = S
    inner_decay, q_decay, k_decay, chunk_decay = dec
    head_tab = lambda shape: pl.BlockSpec((None,) + shape, lambda b, h, s: (h, 0, 0))
    return pl.pallas_call(
        functools.partial(_retention_kernel, n_chunks=ts // RET_CHUNK),
        grid=(B, RET_HEADS, S // ts),
        in_specs=[
            pl.BlockSpec((None, ts, RET_DK), lambda b, h, s: (b, s, h)),
            pl.BlockSpec((None, ts, RET_DK), lambda b, h, s: (b, s, RET_HEADS + h)),
            pl.BlockSpec((None, ts, RET_DV), lambda b, h, s: (b, s, 2 * RET_QK // RET_DV + h)),
            pl.BlockSpec((None, ts, RET_DV), lambda b, h, s: (b, s, h)),
            pl.BlockSpec((1, RET_DV), lambda b, h, s: (0, h)),
            head_tab((RET_CHUNK, RET_CHUNK)),
            head_tab((RET_CHUNK, RET_DK)),
            head_tab((RET_CHUNK, RET_DK)),
            head_tab((RET_DK, RET_DV)),
        ],
        out_specs=pl.BlockSpec((None, ts, RET_DV), lambda b, h, s: (b, s, h)),
        out_shape=jax.ShapeDtypeStruct((B, S, RET_V), BF16),
        compiler_params=pltpu.CompilerParams(
            dimension_semantics=("parallel", "parallel", "arbitrary"), vmem_limit_bytes=VMEM_LIMIT),
        name="retention",
    )(zr, zr, zr, zg, ret_gn, inner_decay, q_decay, k_decay, chunk_decay)


def _attn_unit(q, parts):
    nt = (((1,), (1,)), ((), ()))
    s = [lax.dot_general(q, k, nt, preferred_element_type=F32) + bias for k, _, bias in parts]
    s = s[0] if len(s) == 1 else jnp.concatenate(s, axis=1)
    m = jnp.max(s, axis=-1, keepdims=True)
    e = jnp.exp2(s - m).astype(BF16)
    ol, off = None, 0
    for _, v, _ in parts:
        n = v.shape[0]
        t = jnp.dot(e[:, off:off + n], jnp.concatenate([v, jnp.ones_like(v)], axis=1),
                    preferred_element_type=F32)
        ol = t if ol is None else ol + t
        off += n
    return ol[:, :ATT_HD], m, ol[:, ATT_HD:]


def _attention_kernel(q_ref, k_ref, v_ref, *rest, n_res, n_blk, has_prev):
    if has_prev:
        kp_ref, vp_ref, o_ref, st_ref = rest
    else:
        o_ref, st_ref = rest
    W = ATT_BLK
    neg = jnp.float32(-jnp.inf)
    row2 = lax.broadcasted_iota(jnp.int32, (W, 2 * W), 0)
    col2 = lax.broadcasted_iota(jnp.int32, (W, 2 * W), 1)
    band = jnp.where((col2 >= row2) & (col2 <= row2 + W), 0.0, neg)
    first = jnp.where(col2 <= row2, 0.0, neg)
    lane_half = lax.broadcasted_iota(jnp.int32, (W, LANES), 1) // STAT_LANES
    if has_prev:
        no_prev = W * (1 - jnp.minimum(pl.program_id(1), 1))
        prev_own = jnp.where((col2 >= row2) & (col2 <= row2 + W) & (col2 >= no_prev), 0.0, neg)

    for r in range(n_res):
        for b in range(n_blk):
            rows = slice(b * W, (b + 1) * W)
            st_tile = jnp.zeros((W, LANES), F32)
            for hh in range(ATT_HEADS):
                cols = slice(hh * ATT_HD, (hh + 1) * ATT_HD)
                q = q_ref[r, rows, cols]
                if b > 0:
                    keys = slice((b - 1) * W, (b + 1) * W)
                    parts = [(k_ref[r, keys, cols], v_ref[r, keys, cols], band)]
                elif has_prev:
                    parts = [(jnp.concatenate([kp_ref[0, :, cols], k_ref[r, rows, cols]], axis=0),
                              jnp.concatenate([vp_ref[0, :, cols], v_ref[r, rows, cols]], axis=0),
                              prev_own)]
                else:
                    keys = slice(0, 2 * W)
                    parts = [(k_ref[r, keys, cols], v_ref[r, keys, cols], first)]
                o, m, l = _attn_unit(q, parts)
                o_ref[r, rows, cols] = o.astype(BF16)
                st_tile = jnp.where(lane_half == 2 * hh, m, st_tile)
                st_tile = jnp.where(lane_half == 2 * hh + 1, l, st_tile)
            st_ref[r, rows, :] = st_tile


def _attention(za, g, rows_per_step=4096):
    window, dil = ATT_PATTERNS[g]
    assert window // dil == ATT_BLK
    batch, _, L, _ = za.shape
    has_prev = L > rows_per_step
    lc = min(L, rows_per_step)
    n_res = rows_per_step // lc
    n_blk = lc // ATT_BLK
    assert n_blk >= 2
    grid = (batch, L // lc if has_prev else dil // n_res)
    idx = (lambda b, j, seg: (b, 0, j, seg)) if has_prev else (lambda b, j, seg: (b, j, 0, seg))
    assert not has_prev
    in_specs = [pl.BlockSpec((None, n_res, lc, ATT_W), functools.partial(idx, seg=seg),
                             pipeline_mode=pl.Buffered(ATT_IN_BUFFERS))
                for seg in range(3)]
    out_specs = [pl.BlockSpec((None, n_res, lc, ATT_W), functools.partial(idx, seg=0)),
                 pl.BlockSpec((None, n_res, lc, LANES), functools.partial(idx, seg=0))]
    out_shape = [jax.ShapeDtypeStruct((batch, dil, L, ATT_W), BF16),
                 jax.ShapeDtypeStruct((batch, dil, L, LANES), F32)]
    body = functools.partial(_attention_kernel, n_res=n_res, n_blk=n_blk, has_prev=False)

    def outer(q_hbm, k_hbm, v_hbm, o_hbm, st_hbm):
        pltpu.emit_pipeline(body, grid=grid, in_specs=in_specs, out_specs=out_specs)(
            q_hbm, k_hbm, v_hbm, o_hbm, st_hbm)

    any_spec = pl.BlockSpec(memory_space=pl.ANY)
    return pl.pallas_call(
        outer, in_specs=[any_spec] * 3, out_specs=[any_spec] * 2, out_shape=out_shape,
        compiler_params=pltpu.CompilerParams(vmem_limit_bytes=VMEM_LIMIT),
        name=f"attention_g{g}",
    )(za, za, za)


def _merge_mlp_kernel(x_ref, y_ref, o0_ref, o1_ref, o2_ref, s0_ref, s1_ref, s2_ref, gr_ref, ga_ref,
                      bg_ref, wr_ref, wa_ref, wo_ref, p_ref, gm_ref, wu_ref, wd_ref, gp_ref, wpg_ref,
                      wpp_ref, out_ref, *perm_refs, ff_chunk):
    tm = x_ref.shape[0]

    def token_order(ref, scratch):
        dil = ref.shape[0]
        if dil == 1:
            return ref[0].astype(F32)
        n_slab = ref.shape[2] // LANES
        pitch = _row_pitch(dil)
        for r in range(dil):
            for c in range(n_slab):
                scratch[c, pl.ds(r, tm // dil, stride=pitch), :] = (
                    ref[r, :, c * LANES:(c + 1) * LANES].astype(F32))
        if pitch == dil:
            return jnp.concatenate([scratch[c] for c in range(n_slab)], axis=1)
        return jnp.concatenate(
            [jnp.concatenate([scratch[c, l * pitch:l * pitch + dil, :] for c in range(n_slab)], axis=1)
             for l in range(tm // dil)], axis=0)

    proj = jnp.dot(p_ref[...].astype(BF16), wpp_ref[...], preferred_element_type=F32)
    ret_branch = jnp.dot(y_ref[...], wr_ref[...], preferred_element_type=F32)
    bg = bg_ref[...]
    gate_r = _sigmoid(gr_ref[...].astype(F32) + bg[:, :D_MODEL])
    gate_a = _sigmoid(ga_ref[...].astype(F32) + bg[:, D_MODEL:])
    o_tok = [token_order(o0_ref, None), token_order(o1_ref, perm_refs[0]),
             token_order(o2_ref, perm_refs[1])]
    stats = [token_order(s0_ref, None), token_order(s1_ref, perm_refs[2]),
             token_order(s2_ref, perm_refs[3])]
    sum_lane = (lax.broadcasted_iota(jnp.int32, (tm, LANES), 1) // STAT_LANES) % 2 == 1
    ms = stats
    lses = [st + pltpu.roll(jnp.log2(jnp.where(sum_lane, st, 1.0)), LANES - STAT_LANES, axis=1)
            for st in stats]
    top = jnp.maximum(jnp.maximum(lses[0], lses[1]), lses[2])
    den = jnp.exp2(lses[0] - top) + jnp.exp2(lses[1] - top) + jnp.exp2(lses[2] - top)
    wts = [jnp.exp2(m - top) / den for m in ms]
    per_head = LANES // ATT_HEADS
    parts = []
    for hh in range(ATT_HEADS):
        cols = slice(hh * ATT_HD, (hh + 1) * ATT_HD)
        acc = None
        for w, o_g in zip(wts, o_tok):
            t = w[:, hh * per_head:hh * per_head + 1] * o_g[:, cols]
            acc = t if acc is None else acc + t
        parts.append(acc)
    o = jnp.concatenate(parts, axis=1).astype(BF16)
    att_branch = jnp.dot(o, wa_ref[...], preferred_element_type=F32)
    mix = (gate_r * ret_branch + gate_a * att_branch).astype(BF16)
    x1 = x_ref[...] + jnp.dot(mix, wo_ref[...], preferred_element_type=F32)

    hn = (_rms(x1) * gm_ref[...]).astype(BF16)
    acc = x1
    for c in range(D_FF // ff_chunk):
        cols = slice(c * ff_chunk, (c + 1) * ff_chunk)
        u = jnp.dot(hn, wu_ref[:, cols], preferred_element_type=F32)
        a = jnp.square(jnp.maximum(u, 0.0)).astype(BF16)
        acc = acc + jnp.dot(a, wd_ref[cols, :], preferred_element_type=F32)
    hp = (_rms(acc) * gp_ref[...]).astype(BF16)
    gate_p = _sigmoid(jnp.dot(hp, wpg_ref[...], preferred_element_type=F32))
    out_ref[...] = acc + gate_p * proj


def _merge_mlp(x2, yg, os_, ss_, zg, b_gate, w_ret_out, w_att_out, w_o, p2, g_mlp, w_up, w_down,
               g_ple, w_pg, w_pp, tm=512, ff_chunk=1024):
    T = x2.shape[0]
    npos = os_[0].shape[1] * os_[0].shape[2] // tm
    row = lambda w: pl.BlockSpec((tm, w), lambda i: (i, 0))
    res_major = lambda a: pl.BlockSpec((None, a.shape[1], tm // a.shape[1], a.shape[3]),
                                       lambda i: (i // npos, 0, i % npos, 0))
    dilated = [a for a in os_ + ss_ if a.shape[1] > 1]
    return pl.pallas_call(
        functools.partial(_merge_mlp_kernel, ff_chunk=ff_chunk),
        grid=(T // tm,),
        in_specs=[
            row(D_MODEL), row(RET_V), *[res_major(a) for a in os_ + ss_],
            pl.BlockSpec((tm, D_MODEL), lambda i: (i, RET_V // D_MODEL)),
            pl.BlockSpec((tm, D_MODEL), lambda i: (i, RET_V // D_MODEL + 1)),
            _resident(b_gate.shape), _resident(w_ret_out.shape), _resident(w_att_out.shape),
            _resident(w_o.shape),
            row(PLE_DIM), _resident((1, D_MODEL)), _resident(w_up.shape), _resident(w_down.shape),
            _resident((1, D_MODEL)), _resident(w_pg.shape), _resident(w_pp.shape),
        ],
        out_specs=row(D_MODEL),
        out_shape=jax.ShapeDtypeStruct((T, D_MODEL), F32),
        scratch_shapes=[pltpu.VMEM((a.shape[3] // LANES, tm // a.shape[1] * _row_pitch(a.shape[1]), LANES),
                                   F32) for a in dilated],
        compiler_params=pltpu.CompilerParams(
            dimension_semantics=("parallel",), vmem_limit_bytes=VMEM_LIMIT),
        name="merge_mlp",
    )(x2, yg, *os_, *ss_, zg, zg, b_gate, w_ret_out, w_att_out, w_o, p2, g_mlp, w_up, w_down, g_ple,
      w_pg, w_pp)


def _rot_tables(inv_freq, seq):
    ang = np.arange(seq, dtype=np.float64)[:, None] * inv_freq[None, :]
    cos, sin = np.cos(ang), np.sin(ang)
    return (np.concatenate([cos, cos], axis=1).astype(np.float32),
            np.concatenate([-sin, sin], axis=1).astype(np.float32))


def _decay_tables():
    H, C = RET_HEADS, RET_CHUNK
    log_g = np.log1p(-np.exp2(-5.0 - np.arange(H, dtype=np.float64)))
    idx = np.arange(C, dtype=np.float64)
    diff = idx[:, None] - idx[None, :]
    inner = np.where(diff >= 0, np.exp(log_g[:, None, None] * np.maximum(diff, 0.0)), 0.0)
    q_decay = np.exp(log_g[:, None] * (idx[None, :] + 1.0))
    k_decay = np.exp(log_g[:, None] * (C - 1.0 - idx[None, :]))
    chunk_decay = np.exp(log_g * C)
    f32 = lambda a: np.ascontiguousarray(a, dtype=np.float32)
    return (f32(inner),
            f32(np.broadcast_to(q_decay[:, :, None], (H, C, RET_DK))),
            f32(np.broadcast_to(k_decay[:, :, None], (H, C, RET_DK))),
            f32(np.broadcast_to(chunk_decay[:, None, None], (H, RET_DK, RET_DV))))


def _layer(x, p_i, w_in, b_gate, g_mix, q_gain, k_gain, ret_gn, w_ret_out, w_att_out, w_o,
           g_mlp, w_up, w_down, g_ple, w_ple_proj, w_ple_gate):
    B, S, D = x.shape
    T = B * S
    x2 = x.reshape(T, D)
    ret_freq = 1.0 / (10000.0 ** np.linspace(0.0, 1.0, RET_DK // 2))
    rope_freq = ROPE_THETA ** (-np.arange(0, ATT_HD, 2, dtype=np.float64) / ATT_HD)
    tabs = _rot_tables(ret_freq, S) + _rot_tables(rope_freq, S)

    zr, zg, za0, za1, za2, w_ret_out, w_att_out, w_o, w_up, w_down, w_ple_gate, w_ple_proj = _in_proj(
        x2, g_mix.reshape(1, D), w_in, tabs, q_gain, k_gain, S,
        [w_ret_out, w_att_out, w_o, w_up, w_down, w_ple_gate, w_ple_proj])
    zas = (za0, za1, za2)
    yg = _retention(zr.reshape(B, S, W_RET), zg.reshape(B, S, W_GATE), ret_gn.reshape(1, RET_V),
                    _decay_tables())
    att = [_attention(zas[g], g) for g in range(N_GROUPS)]
    out = _merge_mlp(x2, yg.reshape(T, RET_V), [o for o, _ in att], [s for _, s in att], zg,
                     b_gate.reshape(1, -1), w_ret_out, w_att_out, w_o, p_i.reshape(T, PLE_DIM),
                     g_mlp.reshape(1, D), w_up, w_down, g_ple.reshape(1, D), w_ple_gate, w_ple_proj)
    return out.reshape(B, S, D)


def kernel(x, p, w_in, b_gate, g_mix, q_gain, k_gain, ret_gn, w_ret_out, w_att_out, w_o, g_mlp, w_up,
           w_down, g_ple, w_ple_proj, w_ple_gate):
    for i in range(p.shape[0]):
        x = _layer(x, p[i], w_in[i], b_gate[i], g_mix[i], q_gain[i], k_gain[i], ret_gn[i],
                   w_ret_out[i], w_att_out[i], w_o[i], g_mlp[i], w_up[i], w_down[i], g_ple[i],
                   w_ple_proj[i], w_ple_gate[i])
    return x
```

```python
import functools

import jax
import jax.numpy as jnp
import numpy as np
from jax import lax
from jax.experimental import pallas as pl
from jax.experimental.pallas import tpu as pltpu

F32 = jnp.float32
BF16 = jnp.bfloat16

D_MODEL = 1024
PLE_DIM = 256
D_FF = 4 * D_MODEL
EPS = 1e-6
RET_HEADS = 4
RET_DK = 128
RET_DV = 256
RET_CHUNK = 128
RET_QK = RET_HEADS * RET_DK
RET_V = RET_HEADS * RET_DV
ATT_PATTERNS = ((128, 1), (512, 4), (2048, 16))
N_GROUPS = len(ATT_PATTERNS)
ATT_HEADS = 4
ATT_HD = 128
ATT_W = ATT_HEADS * ATT_HD
ATT_BLK = 128
ROPE_THETA = 10000.0
LOG2E = 1.4426950408889634
ATT_QSCALE = ATT_HD ** -0.5 * LOG2E

LANES = 128
SUBLANES = 8
STAT_LANES = LANES // (2 * ATT_HEADS)
SEG = 512
W_RET = 2 * RET_QK + RET_V
W_GATE = RET_V + 2 * D_MODEL
W_ATT = 3 * N_GROUPS * ATT_W
VMEM_LIMIT = 62 * 1024 * 1024
W_IN_STAGE_ROWS = 16
W_IN_STAGE_SLOTS = 4
ATT_IN_BUFFERS = 3
BF16_ROWS = 16


def _resident(shape):
    return pl.BlockSpec(shape, lambda *_: (0,) * len(shape), pipeline_mode=pl.Buffered(1))


def _row_pitch(dil):
    return dil + SUBLANES if dil % (2 * SUBLANES) == 0 else dil


def _rms(x):
    return x * lax.rsqrt(jnp.mean(x * x, axis=-1, keepdims=True) + EPS)


def _sigmoid(x):
    return 0.5 * jnp.tanh(0.5 * x) + 0.5


def _rotate(a, cos, sin_signed):
    return a * cos + pltpu.roll(a, LANES // 2, axis=1) * sin_signed


def _load_w_in(w_hbm_ref, w_ref, stage_ref, sem):
    n_slots, rows, _ = stage_ref.shape
    n_blk = w_hbm_ref.shape[0] // rows
    rg = slice(W_RET, W_RET + RET_V)

    def copy(k):
        slot = k % n_slots
        return pltpu.make_async_copy(w_hbm_ref.at[pl.ds(k * rows, rows), :], stage_ref.at[slot],
                                     sem.at[slot])

    for k in range(n_slots - 1):
        copy(k).start()
    for k in range(n_blk):
        if k + n_slots - 1 < n_blk:
            copy(k + n_slots - 1).start()
        copy(k).wait()
        slot = k % n_slots
        dst = slice(k * rows, (k + 1) * rows)
        w_ref[dst, :rg.start] = stage_ref[slot, :, :rg.start].astype(BF16)
        w_ref[dst, rg] = (stage_ref[slot, :, rg] * 0.5).astype(BF16)
        w_ref[dst, rg.stop:] = stage_ref[slot, :, rg.stop:].astype(BF16)


def _inproj_kernel(x_ref, g_ref, w_hbm_ref, cr_ref, sr_ref, ca_ref, sa_ref, qg_ref, kg_ref, *rest):
    n_w = (len(rest) - 9) // 2
    zr_ref, zg_ref, *za_refs = rest[n_w:n_w + 5]
    perm_ref, w_ref, stage_ref, sem = rest[-4:]
    for w_f32_ref, w_bf16_ref in zip(rest[:n_w], rest[n_w + 5:-4]):
        w_bf16_ref[...] = w_f32_ref[...].astype(BF16)

    @pl.when(pl.program_id(0) == 0)
    def _():
        _load_w_in(w_hbm_ref, w_ref, stage_ref, sem)

    tm = x_ref.shape[0]
    n_slab = D_MODEL // LANES

    def residue_major(load, dil):
        return jnp.concatenate([load(pl.ds(r, tm // dil, stride=dil)) for r in range(dil)], axis=0)

    xn = _rms(x_ref[...]) * g_ref[...]
    for c in range(n_slab):
        perm_ref[c] = xn[:, c * LANES:(c + 1) * LANES]
    lhs, cos_a, sin_a = {}, {}, {}
    for dil in sorted({d for _, d in ATT_PATTERNS}):
        if dil == 1:
            lhs[dil], cos_a[dil], sin_a[dil] = xn.astype(BF16), ca_ref[...], sa_ref[...]
            continue
        lhs[dil] = jnp.concatenate(
            [residue_major(lambda rows, c=c: perm_ref[c, rows, :], dil) for c in range(n_slab)],
            axis=1).astype(BF16)
        cos_a[dil] = residue_major(lambda rows: ca_ref[rows, :], dil)
        sin_a[dil] = residue_major(lambda rows: sa_ref[rows, :], dil)
    h = lhs[1]

    def store_att(g, seg, val):
        dil = ATT_PATTERNS[g][1]
        za_refs[g][:, :, seg * ATT_W:(seg + 1) * ATT_W] = (
            val.astype(BF16).reshape(dil, tm // dil, ATT_W))

    n_ret, n_rg, n_att = W_RET // SEG, RET_V // SEG, W_ATT // SEG
    n_all = (W_RET + W_GATE + W_ATT) // SEG
    heavy = [0, 1] + list(range(n_ret + n_rg, n_ret + n_rg + n_att))
    for j in heavy + [j for j in range(n_all) if j not in heavy]:
        w_chunk = w_ref[:, j * SEG:(j + 1) * SEG]
        if n_ret + n_rg <= j < n_ret + n_rg + n_att:
            jj = j - n_ret - n_rg
            seg, g = jj // N_GROUPS, jj % N_GROUPS
            dil = ATT_PATTERNS[g][1]
            acc = jnp.dot(lhs[dil], w_chunk, preferred_element_type=F32)
            if seg < 2:
                gain = (qg_ref if seg == 0 else kg_ref)[g:g + 1, :]
                if seg == 0:
                    gain = gain * ATT_QSCALE
                parts = []
                for hh in range(SEG // LANES):
                    a = _rms(acc[:, hh * LANES:(hh + 1) * LANES]) * gain
                    parts.append(_rotate(a, cos_a[dil], sin_a[dil]))
                store_att(g, seg, jnp.concatenate(parts, axis=1))
            else:
                store_att(g, seg, acc)
            continue
        acc = jnp.dot(h, w_chunk, preferred_element_type=F32)
        if j < 2:
            cos, sin = cr_ref[...], sr_ref[...]
            for hh in range(SEG // LANES):
                r = _rotate(acc[:, hh * LANES:(hh + 1) * LANES], cos, sin)
                if j == 1:
                    r = r * (RET_DK ** -0.5)
                zr_ref[:, j * SEG + hh * LANES:j * SEG + (hh + 1) * LANES] = r.astype(BF16)
        elif j < n_ret:
            zr_ref[:, j * SEG:(j + 1) * SEG] = acc.astype(BF16)
        elif j < n_ret + n_rg:
            jj = j - n_ret
            zg_ref[:, jj * SEG:(jj + 1) * SEG] = acc.astype(BF16)
        else:
            jj = j - n_ret - n_att
            zg_ref[:, jj * SEG:(jj + 1) * SEG] = acc.astype(BF16)


def _in_proj(x2, g_mix, w_in, tabs, q_gain, k_gain, seq, weights, tm=512):
    T = x2.shape[0]
    npos = seq // tm
    tab_spec = pl.BlockSpec((tm, LANES), lambda i: (i % npos, 0))

    def slice_spec(w):
        rows = max(w.shape[0] // (T // tm), BF16_ROWS)
        reps = rows * (T // tm) // w.shape[0]
        return pl.BlockSpec((rows, w.shape[1]), lambda i: (i // reps, 0))

    w_specs = [slice_spec(w) for w in weights]
    return pl.pallas_call(
        _inproj_kernel,
        grid=(T // tm,),
        in_specs=[
            pl.BlockSpec((tm, D_MODEL), lambda i: (i, 0)),
            _resident((1, D_MODEL)),
            pl.BlockSpec(memory_space=pl.ANY),
            tab_spec, tab_spec, tab_spec, tab_spec,
            _resident(q_gain.shape), _resident(k_gain.shape),
        ] + w_specs,
        out_specs=[
            pl.BlockSpec((tm, W_RET), lambda i: (i, 0)),
            pl.BlockSpec((tm, W_GATE), lambda i: (i, 0)),
        ] + [
            pl.BlockSpec((None, dil, tm // dil, 3 * ATT_W), lambda i: (i // npos, 0, i % npos, 0))
            for _, dil in ATT_PATTERNS
        ] + w_specs,
        out_shape=[
            jax.ShapeDtypeStruct((T, W_RET), BF16),
            jax.ShapeDtypeStruct((T, W_GATE), BF16),
        ] + [
            jax.ShapeDtypeStruct((T // seq, dil, seq // dil, 3 * ATT_W), BF16)
            for _, dil in ATT_PATTERNS
        ] + [jax.ShapeDtypeStruct(w.shape, BF16) for w in weights],
        scratch_shapes=[pltpu.VMEM((D_MODEL // LANES, tm, LANES), F32),
                        pltpu.VMEM(w_in.shape, BF16),
                        pltpu.VMEM((W_IN_STAGE_SLOTS, W_IN_STAGE_ROWS, w_in.shape[1]), F32),
                        pltpu.SemaphoreType.DMA((W_IN_STAGE_SLOTS,))],
        compiler_params=pltpu.CompilerParams(
            dimension_semantics=("arbitrary",), vmem_limit_bytes=VMEM_LIMIT),
        name="in_proj",
    )(x2, g_mix, w_in, *tabs, q_gain, k_gain, *weights)


def _retention_kernel(q_ref, k_ref, v_ref, rg_ref, gn_ref, inner_ref, qd_ref, kd_ref, cd_ref,
                      y_ref, *, n_chunks):
    C = RET_CHUNK
    inner_decay, q_decay, k_decay, chunk_decay = inner_ref[...], qd_ref[...], kd_ref[...], cd_ref[...]
    gn = gn_ref[...]
    lhs, kvs = [], []
    for c in range(n_chunks):
        rows = slice(c * C, (c + 1) * C)
        q, k, v = q_ref[rows, :], k_ref[rows, :], v_ref[rows, :]
        scores = lax.dot_general(q, k, (((1,), (1,)), ((), ())), preferred_element_type=F32) * inner_decay
        qd = (q.astype(F32) * q_decay).astype(BF16)
        lhs.append(jnp.concatenate([scores.astype(BF16), qd], axis=1))
        kd = (k.astype(F32) * k_decay).astype(BF16)
        kvs.append(lax.dot_general(kd, v, (((0,), (0,)), ((), ())), preferred_element_type=F32))
    state = jnp.zeros((RET_DK, RET_DV), F32)
    for c in range(n_chunks):
        rows = slice(c * C, (c + 1) * C)
        rhs = jnp.concatenate([v_ref[rows, :], state.astype(BF16)], axis=0)
        y = jnp.dot(lhs[c], rhs, preferred_element_type=F32)
        state = state * chunk_decay + kvs[c]
        half_rg = rg_ref[rows, :].astype(F32)
        silu = half_rg * jnp.tanh(half_rg) + half_rg
        y_ref[rows, :] = (silu * (_rms(y) * gn)).astype(BF16)


def _retention(zr, zg, ret_gn, dec):
    B, S, _ = zr.shape
    ts = S
    inner_decay, q_decay, k_decay, chunk_decay = dec
    head_tab = lambda shape: pl.BlockSpec((None,) + shape, lambda b, h, s: (h, 0, 0))
    return pl.pallas_call(
        functools.partial(_retention_kernel, n_chunks=ts // RET_CHUNK),
        grid=(B, RET_HEADS, S // ts),
        in_specs=[
            pl.BlockSpec((None, ts, RET_DK), lambda b, h, s: (b, s, h)),
            pl.BlockSpec((None, ts, RET_DK), lambda b, h, s: (b, s, RET_HEADS + h)),
            pl.BlockSpec((None, ts, RET_DV), lambda b, h, s: (b, s, 2 * RET_QK // RET_DV + h)),
            pl.BlockSpec((None, ts, RET_DV), lambda b, h, s: (b, s, h)),
            pl.BlockSpec((1, RET_DV), lambda b, h, s: (0, h)),
            head_tab((RET_CHUNK, RET_CHUNK)),
            head_tab((RET_CHUNK, RET_DK)),
            head_tab((RET_CHUNK, RET_DK)),
            head_tab((RET_DK, RET_DV)),
        ],
        out_specs=pl.BlockSpec((None, ts, RET_DV), lambda b, h, s: (b, s, h)),
        out_shape=jax.ShapeDtypeStruct((B, S, RET_V), BF16),
        compiler_params=pltpu.CompilerParams(
            dimension_semantics=("parallel", "parallel", "arbitrary"), vmem_limit_bytes=VMEM_LIMIT),
        name="retention",
    )(zr, zr, zr, zg, ret_gn, inner_decay, q_decay, k_decay, chunk_decay)


def _attn_unit(q, parts):
    nt = (((1,), (1,)), ((), ()))
    s = [lax.dot_general(q, k, nt, preferred_element_type=F32) + bias for k, _, bias in parts]
    s = s[0] if len(s) == 1 else jnp.concatenate(s, axis=1)
    m = jnp.max(s, axis=-1, keepdims=True)
    e = jnp.exp2(s - m).astype(BF16)
    ol, off = None, 0
    for _, v, _ in parts:
        n = v.shape[0]
        t = jnp.dot(e[:, off:off + n], jnp.concatenate([v, jnp.ones_like(v)], axis=1),
                    preferred_element_type=F32)
        ol = t if ol is None else ol + t
        off += n
    return ol[:, :ATT_HD], m, ol[:, ATT_HD:]


def _attention_kernel(q_ref, k_ref, v_ref, *rest, n_res, n_blk, has_prev):
    if has_prev:
        kp_ref, vp_ref, o_ref, st_ref = rest
    else:
        o_ref, st_ref = rest
    W = ATT_BLK
    neg = jnp.float32(-jnp.inf)
    row2 = lax.broadcasted_iota(jnp.int32, (W, 2 * W), 0)
    col2 = lax.broadcasted_iota(jnp.int32, (W, 2 * W), 1)
    band = jnp.where((col2 >= row2) & (col2 <= row2 + W), 0.0, neg)
    first = jnp.where(col2 <= row2, 0.0, neg)
    lane_half = lax.broadcasted_iota(jnp.int32, (W, LANES), 1) // STAT_LANES
    if has_prev:
        no_prev = W * (1 - jnp.minimum(pl.program_id(1), 1))
        prev_own = jnp.where((col2 >= row2) & (col2 <= row2 + W) & (col2 >= no_prev), 0.0, neg)

    for r in range(n_res):
        for b in range(n_blk):
            rows = slice(b * W, (b + 1) * W)
            st_tile = jnp.zeros((W, LANES), F32)
            for hh in range(ATT_HEADS):
                cols = slice(hh * ATT_HD, (hh + 1) * ATT_HD)
                q = q_ref[r, rows, cols]
                if b > 0:
                    keys = slice((b - 1) * W, (b + 1) * W)
                    parts = [(k_ref[r, keys, cols], v_ref[r, keys, cols], band)]
                elif has_prev:
                    parts = [(jnp.concatenate([kp_ref[0, :, cols], k_ref[r, rows, cols]], axis=0),
                              jnp.concatenate([vp_ref[0, :, cols], v_ref[r, rows, cols]], axis=0),
                              prev_own)]
                else:
                    keys = slice(0, 2 * W)
                    parts = [(k_ref[r, keys, cols], v_ref[r, keys, cols], first)]
                o, m, l = _attn_unit(q, parts)
                o_ref[r, rows, cols] = o.astype(BF16)
                st_tile = jnp.where(lane_half == 2 * hh, m, st_tile)
                st_tile = jnp.where(lane_half == 2 * hh + 1, l, st_tile)
            st_ref[r, rows, :] = st_tile


def _attention(za, g, rows_per_step=4096):
    window, dil = ATT_PATTERNS[g]
    assert window // dil == ATT_BLK
    batch, _, L, _ = za.shape
    has_prev = L > rows_per_step
    lc = min(L, rows_per_step)
    n_res = rows_per_step // lc
    n_blk = lc // ATT_BLK
    assert n_blk >= 2
    grid = (batch, L // lc if has_prev else dil // n_res)
    idx = (lambda b, j, seg: (b, 0, j, seg)) if has_prev else (lambda b, j, seg: (b, j, 0, seg))
    assert not has_prev
    in_specs = [pl.BlockSpec((None, n_res, lc, ATT_W), functools.partial(idx, seg=seg),
                             pipeline_mode=pl.Buffered(ATT_IN_BUFFERS))
                for seg in range(3)]
    out_specs = [pl.BlockSpec((None, n_res, lc, ATT_W), functools.partial(idx, seg=0)),
                 pl.BlockSpec((None, n_res, lc, LANES), functools.partial(idx, seg=0))]
    out_shape = [jax.ShapeDtypeStruct((batch, dil, L, ATT_W), BF16),
                 jax.ShapeDtypeStruct((batch, dil, L, LANES), F32)]
    body = functools.partial(_attention_kernel, n_res=n_res, n_blk=n_blk, has_prev=False)

    def outer(q_hbm, k_hbm, v_hbm, o_hbm, st_hbm):
        pltpu.emit_pipeline(body, grid=grid, in_specs=in_specs, out_specs=out_specs)(
            q_hbm, k_hbm, v_hbm, o_hbm, st_hbm)

    any_spec = pl.BlockSpec(memory_space=pl.ANY)
    return pl.pallas_call(
        outer, in_specs=[any_spec] * 3, out_specs=[any_spec] * 2, out_shape=out_shape,
        compiler_params=pltpu.CompilerParams(vmem_limit_bytes=VMEM_LIMIT),
        name=f"attention_g{g}",
    )(za, za, za)


def _merge_mlp_kernel(x_ref, y_ref, o0_ref, o1_ref, o2_ref, s0_ref, s1_ref, s2_ref, gr_ref, ga_ref,
                      bg_ref, wr_ref, wa_ref, wo_ref, p_ref, gm_ref, wu_ref, wd_ref, gp_ref, wpg_ref,
                      wpp_ref, out_ref, *perm_refs, ff_chunk):
    tm = x_ref.shape[0]

    def token_order(ref, scratch):
        dil = ref.shape[0]
        if dil == 1:
            return ref[0].astype(F32)
        n_slab = ref.shape[2] // LANES
        pitch = _row_pitch(dil)
        for r in range(dil):
            for c in range(n_slab):
                scratch[c, pl.ds(r, tm // dil, stride=pitch), :] = (
                    ref[r, :, c * LANES:(c + 1) * LANES].astype(F32))
        if pitch == dil:
            return jnp.concatenate([scratch[c] for c in range(n_slab)], axis=1)
        return jnp.concatenate(
            [jnp.concatenate([scratch[c, l * pitch:l * pitch + dil, :] for c in range(n_slab)], axis=1)
             for l in range(tm // dil)], axis=0)

    proj = jnp.dot(p_ref[...].astype(BF16), wpp_ref[...], preferred_element_type=F32)
    ret_branch = jnp.dot(y_ref[...], wr_ref[...], preferred_element_type=F32)
    bg = bg_ref[...]
    gate_r = _sigmoid(gr_ref[...].astype(F32) + bg[:, :D_MODEL])
    gate_a = _sigmoid(ga_ref[...].astype(F32) + bg[:, D_MODEL:])
    o_tok = [token_order(o0_ref, None), token_order(o1_ref, perm_refs[0]),
             token_order(o2_ref, perm_refs[1])]
    stats = [token_order(s0_ref, None), token_order(s1_ref, perm_refs[2]),
             token_order(s2_ref, perm_refs[3])]
    sum_lane = (lax.broadcasted_iota(jnp.int32, (tm, LANES), 1) // STAT_LANES) % 2 == 1
    ms = stats
    lses = [st + pltpu.roll(jnp.log2(jnp.where(sum_lane, st, 1.0)), LANES - STAT_LANES, axis=1)
            for st in stats]
    top = jnp.maximum(jnp.maximum(lses[0], lses[1]), lses[2])
    den = jnp.exp2(lses[0] - top) + jnp.exp2(lses[1] - top) + jnp.exp2(lses[2] - top)
    wts = [jnp.exp2(m - top) / den for m in ms]
    per_head = LANES // ATT_HEADS
    parts = []
    for hh in range(ATT_HEADS):
        cols = slice(hh * ATT_HD, (hh + 1) * ATT_HD)
        acc = None
        for w, o_g in zip(wts, o_tok):
            t = w[:, hh * per_head:hh * per_head + 1] * o_g[:, cols]
            acc = t if acc is None else acc + t
        parts.append(acc)
    o = jnp.concatenate(parts, axis=1).astype(BF16)
    att_branch = jnp.dot(o, wa_ref[...], preferred_element_type=F32)
    mix = (gate_r * ret_branch + gate_a * att_branch).astype(BF16)
    x1 = x_ref[...] + jnp.dot(mix, wo_ref[...], preferred_element_type=F32)

    hn = (_rms(x1) * gm_ref[...]).astype(BF16)
    acc = x1
    for c in range(D_FF // ff_chunk):
        cols = slice(c * ff_chunk, (c + 1) * ff_chunk)
        u = jnp.dot(hn, wu_ref[:, cols], preferred_element_type=F32)
        a = jnp.square(jnp.maximum(u, 0.0)).astype(BF16)
        acc = acc + jnp.dot(a, wd_ref[cols, :], preferred_element_type=F32)
    hp = (_rms(acc) * gp_ref[...]).astype(BF16)
    gate_p = _sigmoid(jnp.dot(hp, wpg_ref[...], preferred_element_type=F32))
    out_ref[...] = acc + gate_p * proj


def _merge_mlp(x2, yg, os_, ss_, zg, b_gate, w_ret_out, w_att_out, w_o, p2, g_mlp, w_up, w_down,
               g_ple, w_pg, w_pp, tm=512, ff_chunk=1024):
    T = x2.shape[0]
    npos = os_[0].shape[1] * os_[0].shape[2] // tm
    row = lambda w: pl.BlockSpec((tm, w), lambda i: (i, 0))
    res_major = lambda a: pl.BlockSpec((None, a.shape[1], tm // a.shape[1], a.shape[3]),
                                       lambda i: (i // npos, 0, i % npos, 0))
    dilated = [a for a in os_ + ss_ if a.shape[1] > 1]
    tiled_specs = [
        row(D_MODEL), row(RET_V), *[res_major(a) for a in os_ + ss_],
        pl.BlockSpec((tm, D_MODEL), lambda i: (i, RET_V // D_MODEL)),
        pl.BlockSpec((tm, D_MODEL), lambda i: (i, RET_V // D_MODEL + 1)),
        row(PLE_DIM),
    ]
    tiled = [x2, yg, *os_, *ss_, zg, zg, p2]
    resident = [b_gate, w_ret_out, w_att_out, w_o, g_mlp, w_up, w_down, g_ple, w_pg, w_pp]
    n_t, n_r = len(tiled), len(resident)

    def outer(*refs):
        t_refs, r_refs = refs[:n_t], refs[n_t:n_t + n_r]
        out_hbm, perm = refs[n_t + n_r], refs[n_t + n_r + 1:]
        bg, wr, wa, wo, gm, wu, wd, gp, wpg, wpp = r_refs

        def body(x_r, y_r, o0, o1, o2, s0, s1, s2, gr, ga, p_r, out_r):
            _merge_mlp_kernel(x_r, y_r, o0, o1, o2, s0, s1, s2, gr, ga, bg, wr, wa, wo, p_r, gm, wu, wd,
                              gp, wpg, wpp, out_r, *perm, ff_chunk=ff_chunk)

        pltpu.emit_pipeline(body, grid=(T // tm,), in_specs=tiled_specs, out_specs=[row(D_MODEL)])(
            *t_refs, out_hbm)

    any_spec = pl.BlockSpec(memory_space=pl.ANY)
    vmem_spec = pl.BlockSpec(memory_space=pltpu.VMEM)
    return pl.pallas_call(
        outer,
        in_specs=[any_spec] * n_t + [vmem_spec] * n_r,
        out_specs=any_spec,
        out_shape=jax.ShapeDtypeStruct((T, D_MODEL), F32),
        scratch_shapes=[pltpu.VMEM((a.shape[3] // LANES, tm // a.shape[1] * _row_pitch(a.shape[1]), LANES),
                                   F32) for a in dilated],
        compiler_params=pltpu.CompilerParams(vmem_limit_bytes=VMEM_LIMIT),
        name="merge_mlp",
    )(*tiled, *resident)


def _rot_tables(inv_freq, seq):
    ang = np.arange(seq, dtype=np.float64)[:, None] * inv_freq[None, :]
    cos, sin = np.cos(ang), np.sin(ang)
    return (np.concatenate([cos, cos], axis=1).astype(np.float32),
            np.concatenate([-sin, sin], axis=1).astype(np.float32))


def _decay_tables():
    H, C = RET_HEADS, RET_CHUNK
    log_g = np.log1p(-np.exp2(-5.0 - np.arange(H, dtype=np.float64)))
    idx = np.arange(C, dtype=np.float64)
    diff = idx[:, None] - idx[None, :]
    inner = np.where(diff >= 0, np.exp(log_g[:, None, None] * np.maximum(diff, 0.0)), 0.0)
    q_decay = np.exp(log_g[:, None] * (idx[None, :] + 1.0))
    k_decay = np.exp(log_g[:, None] * (C - 1.0 - idx[None, :]))
    chunk_decay = np.exp(log_g * C)
    f32 = lambda a: np.ascontiguousarray(a, dtype=np.float32)
    return (f32(inner),
            f32(np.broadcast_to(q_decay[:, :, None], (H, C, RET_DK))),
            f32(np.broadcast_to(k_decay[:, :, None], (H, C, RET_DK))),
            f32(np.broadcast_to(chunk_decay[:, None, None], (H, RET_DK, RET_DV))))


def _layer(x, p_i, w_in, b_gate, g_mix, q_gain, k_gain, ret_gn, w_ret_out, w_att_out, w_o,
           g_mlp, w_up, w_down, g_ple, w_ple_proj, w_ple_gate):
    B, S, D = x.shape
    T = B * S
    x2 = x.reshape(T, D)
    ret_freq = 1.0 / (10000.0 ** np.linspace(0.0, 1.0, RET_DK // 2))
    rope_freq = ROPE_THETA ** (-np.arange(0, ATT_HD, 2, dtype=np.float64) / ATT_HD)
    tabs = _rot_tables(ret_freq, S) + _rot_tables(rope_freq, S)

    zr, zg, za0, za1, za2, w_ret_out, w_att_out, w_o, w_up, w_down, w_ple_gate, w_ple_proj = _in_proj(
        x2, g_mix.reshape(1, D), w_in, tabs, q_gain, k_gain, S,
        [w_ret_out, w_att_out, w_o, w_up, w_down, w_ple_gate, w_ple_proj])
    zas = (za0, za1, za2)
    yg = _retention(zr.reshape(B, S, W_RET), zg.reshape(B, S, W_GATE), ret_gn.reshape(1, RET_V),
                    _decay_tables())
    att = [_attention(zas[g], g) for g in range(N_GROUPS)]
    out = _merge_mlp(x2, yg.reshape(T, RET_V), [o for o, _ in att], [s for _, s in att], zg,
                     b_gate.reshape(1, -1), w_ret_out, w_att_out, w_o, p_i.reshape(T, PLE_DIM),
                     g_mlp.reshape(1, D), w_up, w_down, g_ple.reshape(1, D), w_ple_gate, w_ple_proj)
    return out.reshape(B, S, D)


def kernel(x, p, w_in, b_gate, g_mix, q_gain, k_gain, ret_gn, w_ret_out, w_att_out, w_o, g_mlp, w_up,
           w_down, g_ple, w_ple_proj, w_ple_gate):
    for i in range(p.shape[0]):
        x = _layer(x, p[i], w_in[i], b_gate[i], g_mix[i], q_gain[i], k_gain[i], ret_gn[i],
                   w_ret_out[i], w_att_out[i], w_o[i], g_mlp[i], w_up[i], w_down[i], g_ple[i],
                   w_ple_proj[i], w_ple_gate[i])
    return x
```

```python
import functools

import jax
import jax.numpy as jnp
import numpy as np
from jax import lax
from jax.experimental import pallas as pl
from jax.experimental.pallas import tpu as pltpu

F32 = jnp.float32
BF16 = jnp.bfloat16

D_MODEL = 1024
PLE_DIM = 256
D_FF = 4 * D_MODEL
EPS = 1e-6
RET_HEADS = 4
RET_DK = 128
RET_DV = 256
RET_CHUNK = 128
RET_QK = RET_HEADS * RET_DK
RET_V = RET_HEADS * RET_DV
ATT_PATTERNS = ((128, 1), (512, 4), (2048, 16))
N_GROUPS = len(ATT_PATTERNS)
ATT_HEADS = 4
ATT_HD = 128
ATT_W = ATT_HEADS * ATT_HD
ATT_BLK = 128
ROPE_THETA = 10000.0
LOG2E = 1.4426950408889634
ATT_QSCALE = ATT_HD ** -0.5 * LOG2E

LANES = 128
SUBLANES = 8
STAT_LANES = LANES // (2 * ATT_HEADS)
SEG = 512
W_RET = 2 * RET_QK + RET_V
W_GATE = RET_V + 2 * D_MODEL
W_ATT = 3 * N_GROUPS * ATT_W
VMEM_LIMIT = 62 * 1024 * 1024
W_IN_STAGE_ROWS = 16
W_IN_STAGE_SLOTS = 4
ATT_IN_BUFFERS = 3
XN_ROW_BLOCKS = 4
BF16_ROWS = 16


def _resident(shape):
    return pl.BlockSpec(shape, lambda *_: (0,) * len(shape), pipeline_mode=pl.Buffered(1))


def _row_pitch(dil):
    return dil + SUBLANES if dil % (2 * SUBLANES) == 0 else dil


def _rms(x):
    return x * lax.rsqrt(jnp.mean(x * x, axis=-1, keepdims=True) + EPS)


def _sigmoid(x):
    return 0.5 * jnp.tanh(0.5 * x) + 0.5


def _rotate(a, cos, sin_signed):
    return a * cos + pltpu.roll(a, LANES // 2, axis=1) * sin_signed


def _load_w_in(w_hbm_ref, w_ref, stage_ref, sem):
    n_slots, rows, _ = stage_ref.shape
    n_blk = w_hbm_ref.shape[0] // rows
    rg = slice(W_RET, W_RET + RET_V)

    def copy(k):
        slot = k % n_slots
        return pltpu.make_async_copy(w_hbm_ref.at[pl.ds(k * rows, rows), :], stage_ref.at[slot],
                                     sem.at[slot])

    for k in range(n_slots - 1):
        copy(k).start()
    for k in range(n_blk):
        if k + n_slots - 1 < n_blk:
            copy(k + n_slots - 1).start()
        copy(k).wait()
        slot = k % n_slots
        dst = slice(k * rows, (k + 1) * rows)
        w_ref[dst, :rg.start] = stage_ref[slot, :, :rg.start].astype(BF16)
        w_ref[dst, rg] = (stage_ref[slot, :, rg] * 0.5).astype(BF16)
        w_ref[dst, rg.stop:] = stage_ref[slot, :, rg.stop:].astype(BF16)


def _inproj_kernel(x_ref, g_ref, w_hbm_ref, cr_ref, sr_ref, ca_ref, sa_ref, qg_ref, kg_ref, *rest):
    n_w = (len(rest) - 9) // 2
    zr_ref, zg_ref, *za_refs = rest[n_w:n_w + 5]
    perm_ref, w_ref, stage_ref, sem = rest[-4:]
    for w_f32_ref, w_bf16_ref in zip(rest[:n_w], rest[n_w + 5:-4]):
        w_bf16_ref[...] = w_f32_ref[...].astype(BF16)

    @pl.when(pl.program_id(0) == 0)
    def _():
        _load_w_in(w_hbm_ref, w_ref, stage_ref, sem)

    tm = x_ref.shape[0]
    n_slab = D_MODEL // LANES

    def residue_major(load, dil):
        return jnp.concatenate([load(pl.ds(r, tm // dil, stride=dil)) for r in range(dil)], axis=0)

    rb = tm // XN_ROW_BLOCKS
    xn_blocks = []
    for b_ in range(XN_ROW_BLOCKS):
        xb = _rms(x_ref[b_ * rb:(b_ + 1) * rb, :]) * g_ref[...]
        for c in range(n_slab):
            perm_ref[c, b_ * rb:(b_ + 1) * rb, :] = xb[:, c * LANES:(c + 1) * LANES]
        xn_blocks.append(xb.astype(BF16))
    lhs, cos_a, sin_a = {}, {}, {}
    for dil in sorted({d for _, d in ATT_PATTERNS}):
        if dil == 1:
            lhs[dil], cos_a[dil], sin_a[dil] = jnp.concatenate(xn_blocks, axis=0), ca_ref[...], sa_ref[...]
            continue
        lhs[dil] = jnp.concatenate(
            [residue_major(lambda rows, c=c: perm_ref[c, rows, :], dil) for c in range(n_slab)],
            axis=1).astype(BF16)
        cos_a[dil] = residue_major(lambda rows: ca_ref[rows, :], dil)
        sin_a[dil] = residue_major(lambda rows: sa_ref[rows, :], dil)
    h = lhs[1]

    def store_att(g, seg, val):
        dil = ATT_PATTERNS[g][1]
        za_refs[g][:, :, seg * ATT_W:(seg + 1) * ATT_W] = (
            val.astype(BF16).reshape(dil, tm // dil, ATT_W))

    n_ret, n_rg, n_att = W_RET // SEG, RET_V // SEG, W_ATT // SEG
    n_all = (W_RET + W_GATE + W_ATT) // SEG
    heavy = [0, 1] + list(range(n_ret + n_rg, n_ret + n_rg + n_att))
    for j in heavy + [j for j in range(n_all) if j not in heavy]:
        w_chunk = w_ref[:, j * SEG:(j + 1) * SEG]
        if n_ret + n_rg <= j < n_ret + n_rg + n_att:
            jj = j - n_ret - n_rg
            seg, g = jj // N_GROUPS, jj % N_GROUPS
            dil = ATT_PATTERNS[g][1]
            acc = jnp.dot(lhs[dil], w_chunk, preferred_element_type=F32)
            if seg < 2:
                gain = (qg_ref if seg == 0 else kg_ref)[g:g + 1, :]
                if seg == 0:
                    gain = gain * ATT_QSCALE
                parts = []
                for hh in range(SEG // LANES):
                    a = _rms(acc[:, hh * LANES:(hh + 1) * LANES]) * gain
                    parts.append(_rotate(a, cos_a[dil], sin_a[dil]))
                store_att(g, seg, jnp.concatenate(parts, axis=1))
            else:
                store_att(g, seg, acc)
            continue
        if j == 0:
            acc = jnp.concatenate([jnp.dot(xb, w_chunk, preferred_element_type=F32) for xb in xn_blocks],
                                  axis=0)
        else:
            acc = jnp.dot(h, w_chunk, preferred_element_type=F32)
        if j < 2:
            cos, sin = cr_ref[...], sr_ref[...]
            for hh in range(SEG // LANES):
                r = _rotate(acc[:, hh * LANES:(hh + 1) * LANES], cos, sin)
                if j == 1:
                    r = r * (RET_DK ** -0.5)
                zr_ref[:, j * SEG + hh * LANES:j * SEG + (hh + 1) * LANES] = r.astype(BF16)
        elif j < n_ret:
            zr_ref[:, j * SEG:(j + 1) * SEG] = acc.astype(BF16)
        elif j < n_ret + n_rg:
            jj = j - n_ret
            zg_ref[:, jj * SEG:(jj + 1) * SEG] = acc.astype(BF16)
        else:
            jj = j - n_ret - n_att
            zg_ref[:, jj * SEG:(jj + 1) * SEG] = acc.astype(BF16)


def _in_proj(x2, g_mix, w_in, tabs, q_gain, k_gain, seq, weights, tm=512):
    T = x2.shape[0]
    npos = seq // tm
    tab_spec = pl.BlockSpec((tm, LANES), lambda i: (i % npos, 0))

    def slice_spec(w):
        rows = max(w.shape[0] // (T // tm), BF16_ROWS)
        reps = rows * (T // tm) // w.shape[0]
        return pl.BlockSpec((rows, w.shape[1]), lambda i: (i // reps, 0))

    w_specs = [slice_spec(w) for w in weights]
    return pl.pallas_call(
        _inproj_kernel,
        grid=(T // tm,),
        in_specs=[
            pl.BlockSpec((tm, D_MODEL), lambda i: (i, 0)),
            _resident((1, D_MODEL)),
            pl.BlockSpec(memory_space=pl.ANY),
            tab_spec, tab_spec, tab_spec, tab_spec,
            _resident(q_gain.shape), _resident(k_gain.shape),
        ] + w_specs,
        out_specs=[
            pl.BlockSpec((tm, W_RET), lambda i: (i, 0)),
            pl.BlockSpec((tm, W_GATE), lambda i: (i, 0)),
        ] + [
            pl.BlockSpec((None, dil, tm // dil, 3 * ATT_W), lambda i: (i // npos, 0, i % npos, 0))
            for _, dil in ATT_PATTERNS
        ] + w_specs,
        out_shape=[
            jax.ShapeDtypeStruct((T, W_RET), BF16),
            jax.ShapeDtypeStruct((T, W_GATE), BF16),
        ] + [
            jax.ShapeDtypeStruct((T // seq, dil, seq // dil, 3 * ATT_W), BF16)
            for _, dil in ATT_PATTERNS
        ] + [jax.ShapeDtypeStruct(w.shape, BF16) for w in weights],
        scratch_shapes=[pltpu.VMEM((D_MODEL // LANES, tm, LANES), F32),
                        pltpu.VMEM(w_in.shape, BF16),
                        pltpu.VMEM((W_IN_STAGE_SLOTS, W_IN_STAGE_ROWS, w_in.shape[1]), F32),
                        pltpu.SemaphoreType.DMA((W_IN_STAGE_SLOTS,))],
        compiler_params=pltpu.CompilerParams(
            dimension_semantics=("arbitrary",), vmem_limit_bytes=VMEM_LIMIT),
        name="in_proj",
    )(x2, g_mix, w_in, *tabs, q_gain, k_gain, *weights)


def _retention_kernel(q_ref, k_ref, v_ref, rg_ref, gn_ref, inner_ref, qd_ref, kd_ref, cd_ref,
                      y_ref, *, n_chunks):
    C = RET_CHUNK
    inner_decay, q_decay, k_decay, chunk_decay = inner_ref[...], qd_ref[...], kd_ref[...], cd_ref[...]
    gn = gn_ref[...]
    lhs, kvs = [], []
    for c in range(n_chunks):
        rows = slice(c * C, (c + 1) * C)
        q, k, v = q_ref[rows, :], k_ref[rows, :], v_ref[rows, :]
        scores = lax.dot_general(q, k, (((1,), (1,)), ((), ())), preferred_element_type=F32) * inner_decay
        qd = (q.astype(F32) * q_decay).astype(BF16)
        lhs.append(jnp.concatenate([scores.astype(BF16), qd], axis=1))
        kd = (k.astype(F32) * k_decay).astype(BF16)
        kvs.append(lax.dot_general(kd, v, (((0,), (0,)), ((), ())), preferred_element_type=F32))
    state = jnp.zeros((RET_DK, RET_DV), F32)
    for c in range(n_chunks):
        rows = slice(c * C, (c + 1) * C)
        rhs = jnp.concatenate([v_ref[rows, :], state.astype(BF16)], axis=0)
        y = jnp.dot(lhs[c], rhs, preferred_element_type=F32)
        state = state * chunk_decay + kvs[c]
        half_rg = rg_ref[rows, :].astype(F32)
        silu = half_rg * jnp.tanh(half_rg) + half_rg
        y_ref[rows, :] = (silu * (_rms(y) * gn)).astype(BF16)


def _retention(zr, zg, ret_gn, dec):
    B, S, _ = zr.shape
    ts = S
    inner_decay, q_decay, k_decay, chunk_decay = dec
    head_tab = lambda shape: pl.BlockSpec((None,) + shape, lambda b, h, s: (h, 0, 0))
    return pl.pallas_call(
        functools.partial(_retention_kernel, n_chunks=ts // RET_CHUNK),
        grid=(B, RET_HEADS, S // ts),
        in_specs=[
            pl.BlockSpec((None, ts, RET_DK), lambda b, h, s: (b, s, h)),
            pl.BlockSpec((None, ts, RET_DK), lambda b, h, s: (b, s, RET_HEADS + h)),
            pl.BlockSpec((None, ts, RET_DV), lambda b, h, s: (b, s, 2 * RET_QK // RET_DV + h)),
            pl.BlockSpec((None, ts, RET_DV), lambda b, h, s: (b, s, h)),
            pl.BlockSpec((1, RET_DV), lambda b, h, s: (0, h)),
            head_tab((RET_CHUNK, RET_CHUNK)),
            head_tab((RET_CHUNK, RET_DK)),
            head_tab((RET_CHUNK, RET_DK)),
            head_tab((RET_DK, RET_DV)),
        ],
        out_specs=pl.BlockSpec((None, ts, RET_DV), lambda b, h, s: (b, s, h)),
        out_shape=jax.ShapeDtypeStruct((B, S, RET_V), BF16),
        compiler_params=pltpu.CompilerParams(
            dimension_semantics=("parallel", "parallel", "arbitrary"), vmem_limit_bytes=VMEM_LIMIT),
        name="retention",
    )(zr, zr, zr, zg, ret_gn, inner_decay, q_decay, k_decay, chunk_decay)


def _attn_unit(q, parts):
    nt = (((1,), (1,)), ((), ()))
    s = [lax.dot_general(q, k, nt, preferred_element_type=F32) + bias for k, _, bias in parts]
    s = s[0] if len(s) == 1 else jnp.concatenate(s, axis=1)
    m = jnp.max(s, axis=-1, keepdims=True)
    e = jnp.exp2(s - m).astype(BF16)
    ol, off = None, 0
    for _, v, _ in parts:
        n = v.shape[0]
        t = jnp.dot(e[:, off:off + n], jnp.concatenate([v, jnp.ones_like(v)], axis=1),
                    preferred_element_type=F32)
        ol = t if ol is None else ol + t
        off += n
    return ol[:, :ATT_HD], m, ol[:, ATT_HD:]


def _attention_kernel(q_ref, k_ref, v_ref, *rest, n_res, n_blk, has_prev):
    if has_prev:
        kp_ref, vp_ref, o_ref, st_ref = rest
    else:
        o_ref, st_ref = rest
    W = ATT_BLK
    neg = jnp.float32(-jnp.inf)
    row2 = lax.broadcasted_iota(jnp.int32, (W, 2 * W), 0)
    col2 = lax.broadcasted_iota(jnp.int32, (W, 2 * W), 1)
    band = jnp.where((col2 >= row2) & (col2 <= row2 + W), 0.0, neg)
    first = jnp.where(col2 <= row2, 0.0, neg)
    lane_half = lax.broadcasted_iota(jnp.int32, (W, LANES), 1) // STAT_LANES
    if has_prev:
        no_prev = W * (1 - jnp.minimum(pl.program_id(1), 1))
        prev_own = jnp.where((col2 >= row2) & (col2 <= row2 + W) & (col2 >= no_prev), 0.0, neg)

    for r in range(n_res):
        for b in range(n_blk):
            rows = slice(b * W, (b + 1) * W)
            st_tile = jnp.zeros((W, LANES), F32)
            for hh in range(ATT_HEADS):
                cols = slice(hh * ATT_HD, (hh + 1) * ATT_HD)
                q = q_ref[r, rows, cols]
                if b > 0:
                    keys = slice((b - 1) * W, (b + 1) * W)
                    parts = [(k_ref[r, keys, cols], v_ref[r, keys, cols], band)]
                elif has_prev:
                    parts = [(jnp.concatenate([kp_ref[0, :, cols], k_ref[r, rows, cols]], axis=0),
                              jnp.concatenate([vp_ref[0, :, cols], v_ref[r, rows, cols]], axis=0),
                              prev_own)]
                else:
                    keys = slice(0, 2 * W)
                    parts = [(k_ref[r, keys, cols], v_ref[r, keys, cols], first)]
                o, m, l = _attn_unit(q, parts)
                o_ref[r, rows, cols] = o.astype(BF16)
                st_tile = jnp.where(lane_half == 2 * hh, m, st_tile)
                st_tile = jnp.where(lane_half == 2 * hh + 1, l, st_tile)
            st_ref[r, rows, :] = st_tile


def _attention(za, g, rows_per_step=4096):
    window, dil = ATT_PATTERNS[g]
    assert window // dil == ATT_BLK
    batch, _, L, _ = za.shape
    has_prev = L > rows_per_step
    lc = min(L, rows_per_step)
    n_res = rows_per_step // lc
    n_blk = lc // ATT_BLK
    assert n_blk >= 2
    grid = (batch, L // lc if has_prev else dil // n_res)
    idx = (lambda b, j, seg: (b, 0, j, seg)) if has_prev else (lambda b, j, seg: (b, j, 0, seg))
    assert not has_prev
    in_specs = [pl.BlockSpec((None, n_res, lc, ATT_W), functools.partial(idx, seg=seg),
                             pipeline_mode=pl.Buffered(ATT_IN_BUFFERS))
                for seg in range(3)]
    out_specs = [pl.BlockSpec((None, n_res, lc, ATT_W), functools.partial(idx, seg=0)),
                 pl.BlockSpec((None, n_res, lc, LANES), functools.partial(idx, seg=0))]
    out_shape = [jax.ShapeDtypeStruct((batch, dil, L, ATT_W), BF16),
                 jax.ShapeDtypeStruct((batch, dil, L, LANES), F32)]
    body = functools.partial(_attention_kernel, n_res=n_res, n_blk=n_blk, has_prev=False)

    def outer(q_hbm, k_hbm, v_hbm, o_hbm, st_hbm):
        pltpu.emit_pipeline(body, grid=grid, in_specs=in_specs, out_specs=out_specs)(
            q_hbm, k_hbm, v_hbm, o_hbm, st_hbm)

    any_spec = pl.BlockSpec(memory_space=pl.ANY)
    return pl.pallas_call(
        outer, in_specs=[any_spec] * 3, out_specs=[any_spec] * 2, out_shape=out_shape,
        compiler_params=pltpu.CompilerParams(vmem_limit_bytes=VMEM_LIMIT),
        name=f"attention_g{g}",
    )(za, za, za)


def _merge_mlp_kernel(x_ref, y_ref, o0_ref, o1_ref, o2_ref, s0_ref, s1_ref, s2_ref, gr_ref, ga_ref,
                      bg_ref, wr_ref, wa_ref, wo_ref, p_ref, gm_ref, wu_ref, wd_ref, gp_ref, wpg_ref,
                      wpp_ref, out_ref, *perm_refs, ff_chunk):
    tm = x_ref.shape[0]

    def token_order(ref, scratch):
        dil = ref.shape[0]
        if dil == 1:
            return ref[0].astype(F32)
        n_slab = ref.shape[2] // LANES
        pitch = _row_pitch(dil)
        for r in range(dil):
            for c in range(n_slab):
                scratch[c, pl.ds(r, tm // dil, stride=pitch), :] = (
                    ref[r, :, c * LANES:(c + 1) * LANES].astype(F32))
        if pitch == dil:
            return jnp.concatenate([scratch[c] for c in range(n_slab)], axis=1)
        return jnp.concatenate(
            [jnp.concatenate([scratch[c, l * pitch:l * pitch + dil, :] for c in range(n_slab)], axis=1)
             for l in range(tm // dil)], axis=0)

    proj = jnp.dot(p_ref[...].astype(BF16), wpp_ref[...], preferred_element_type=F32)
    ret_branch = jnp.dot(y_ref[...], wr_ref[...], preferred_element_type=F32)
    bg = bg_ref[...]
    gate_r = _sigmoid(gr_ref[...].astype(F32) + bg[:, :D_MODEL])
    gate_a = _sigmoid(ga_ref[...].astype(F32) + bg[:, D_MODEL:])
    o_tok = [token_order(o0_ref, None), token_order(o1_ref, perm_refs[0]),
             token_order(o2_ref, perm_refs[1])]
    stats = [token_order(s0_ref, None), token_order(s1_ref, perm_refs[2]),
             token_order(s2_ref, perm_refs[3])]
    sum_lane = (lax.broadcasted_iota(jnp.int32, (tm, LANES), 1) // STAT_LANES) % 2 == 1
    ms = stats
    lses = [st + pltpu.roll(jnp.log2(jnp.where(sum_lane, st, 1.0)), LANES - STAT_LANES, axis=1)
            for st in stats]
    top = jnp.maximum(jnp.maximum(lses[0], lses[1]), lses[2])
    den = jnp.exp2(lses[0] - top) + jnp.exp2(lses[1] - top) + jnp.exp2(lses[2] - top)
    wts = [jnp.exp2(m - top) / den for m in ms]
    per_head = LANES // ATT_HEADS
    parts = []
    for hh in range(ATT_HEADS):
        cols = slice(hh * ATT_HD, (hh + 1) * ATT_HD)
        acc = None
        for w, o_g in zip(wts, o_tok):
            t = w[:, hh * per_head:hh * per_head + 1] * o_g[:, cols]
            acc = t if acc is None else acc + t
        parts.append(acc)
    o = jnp.concatenate(parts, axis=1).astype(BF16)
    att_branch = jnp.dot(o, wa_ref[...], preferred_element_type=F32)
    mix = (gate_r * ret_branch + gate_a * att_branch).astype(BF16)
    x1 = x_ref[...] + jnp.dot(mix, wo_ref[...], preferred_element_type=F32)

    hn = (_rms(x1) * gm_ref[...]).astype(BF16)
    acc = x1
    for c in range(D_FF // ff_chunk):
        cols = slice(c * ff_chunk, (c + 1) * ff_chunk)
        u = jnp.dot(hn, wu_ref[:, cols], preferred_element_type=F32)
        a = jnp.square(jnp.maximum(u, 0.0)).astype(BF16)
        acc = acc + jnp.dot(a, wd_ref[cols, :], preferred_element_type=F32)
    hp = (_rms(acc) * gp_ref[...]).astype(BF16)
    gate_p = _sigmoid(jnp.dot(hp, wpg_ref[...], preferred_element_type=F32))
    out_ref[...] = acc + gate_p * proj


def _merge_mlp(x2, yg, os_, ss_, zg, b_gate, w_ret_out, w_att_out, w_o, p2, g_mlp, w_up, w_down,
               g_ple, w_pg, w_pp, tm=512, ff_chunk=1024):
    T = x2.shape[0]
    npos = os_[0].shape[1] * os_[0].shape[2] // tm
    row = lambda w: pl.BlockSpec((tm, w), lambda i: (i, 0))
    res_major = lambda a: pl.BlockSpec((None, a.shape[1], tm // a.shape[1], a.shape[3]),
                                       lambda i: (i // npos, 0, i % npos, 0))
    dilated = [a for a in os_ + ss_ if a.shape[1] > 1]
    return pl.pallas_call(
        functools.partial(_merge_mlp_kernel, ff_chunk=ff_chunk),
        grid=(T // tm,),
        in_specs=[
            row(D_MODEL), row(RET_V), *[res_major(a) for a in os_ + ss_],
            pl.BlockSpec((tm, D_MODEL), lambda i: (i, RET_V // D_MODEL)),
            pl.BlockSpec((tm, D_MODEL), lambda i: (i, RET_V // D_MODEL + 1)),
            _resident(b_gate.shape), _resident(w_ret_out.shape), _resident(w_att_out.shape),
            _resident(w_o.shape),
            row(PLE_DIM), _resident((1, D_MODEL)), _resident(w_up.shape), _resident(w_down.shape),
            _resident((1, D_MODEL)), _resident(w_pg.shape), _resident(w_pp.shape),
        ],
        out_specs=row(D_MODEL),
        out_shape=jax.ShapeDtypeStruct((T, D_MODEL), F32),
        scratch_shapes=[pltpu.VMEM((a.shape[3] // LANES, tm // a.shape[1] * _row_pitch(a.shape[1]), LANES),
                                   F32) for a in dilated],
        compiler_params=pltpu.CompilerParams(
            dimension_semantics=("parallel",), vmem_limit_bytes=VMEM_LIMIT),
        name="merge_mlp",
    )(x2, yg, *os_, *ss_, zg, zg, b_gate, w_ret_out, w_att_out, w_o, p2, g_mlp, w_up, w_down, g_ple,
      w_pg, w_pp)


def _rot_tables(inv_freq, seq):
    ang = np.arange(seq, dtype=np.float64)[:, None] * inv_freq[None, :]
    cos, sin = np.cos(ang), np.sin(ang)
    return (np.concatenate([cos, cos], axis=1).astype(np.float32),
            np.concatenate([-sin, sin], axis=1).astype(np.float32))


def _decay_tables():
    H, C = RET_HEADS, RET_CHUNK
    log_g = np.log1p(-np.exp2(-5.0 - np.arange(H, dtype=np.float64)))
    idx = np.arange(C, dtype=np.float64)
    diff = idx[:, None] - idx[None, :]
    inner = np.where(diff >= 0, np.exp(log_g[:, None, None] * np.maximum(diff, 0.0)), 0.0)
    q_decay = np.exp(log_g[:, None] * (idx[None, :] + 1.0))
    k_decay = np.exp(log_g[:, None] * (C - 1.0 - idx[None, :]))
    chunk_decay = np.exp(log_g * C)
    f32 = lambda a: np.ascontiguousarray(a, dtype=np.float32)
    return (f32(inner),
            f32(np.broadcast_to(q_decay[:, :, None], (H, C, RET_DK))),
            f32(np.broadcast_to(k_decay[:, :, None], (H, C, RET_DK))),
            f32(np.broadcast_to(chunk_decay[:, None, None], (H, RET_DK, RET_DV))))


def _layer(x, p_i, w_in, b_gate, g_mix, q_gain, k_gain, ret_gn, w_ret_out, w_att_out, w_o,
           g_mlp, w_up, w_down, g_ple, w_ple_proj, w_ple_gate):
    B, S, D = x.shape
    T = B * S
    x2 = x.reshape(T, D)
    ret_freq = 1.0 / (10000.0 ** np.linspace(0.0, 1.0, RET_DK // 2))
    rope_freq = ROPE_THETA ** (-np.arange(0, ATT_HD, 2, dtype=np.float64) / ATT_HD)
    tabs = _rot_tables(ret_freq, S) + _rot_tables(rope_freq, S)

    zr, zg, za0, za1, za2, w_ret_out, w_att_out, w_o, w_up, w_down, w_ple_gate, w_ple_proj = _in_proj(
        x2, g_mix.reshape(1, D), w_in, tabs, q_gain, k_gain, S,
        [w_ret_out, w_att_out, w_o, w_up, w_down, w_ple_gate, w_ple_proj])
    zas = (za0, za1, za2)
    yg = _retention(zr.reshape(B, S, W_RET), zg.reshape(B, S, W_GATE), ret_gn.reshape(1, RET_V),
                    _decay_tables())
    att = [_attention(zas[g], g) for g in range(N_GROUPS)]
    out = _merge_mlp(x2, yg.reshape(T, RET_V), [o for o, _ in att], [s for _, s in att], zg,
                     b_gate.reshape(1, -1), w_ret_out, w_att_out, w_o, p_i.reshape(T, PLE_DIM),
                     g_mlp.reshape(1, D), w_up, w_down, g_ple.reshape(1, D), w_ple_gate, w_ple_proj)
    return out.reshape(B, S, D)


def kernel(x, p, w_in, b_gate, g_mix, q_gain, k_gain, ret_gn, w_ret_out, w_att_out, w_o, g_mlp, w_up,
           w_down, g_ple, w_ple_proj, w_ple_gate):
    for i in range(p.shape[0]):
        x = _layer(x, p[i], w_in[i], b_gate[i], g_mix[i], q_gain[i], k_gain[i], ret_gn[i],
                   w_ret_out[i], w_att_out[i], w_o[i], g_mlp[i], w_up[i], w_down[i], g_ple[i],
                   w_ple_proj[i], w_ple_gate[i])
    return x
```

```python
import functools

import jax
import jax.numpy as jnp
import numpy as np
from jax import lax
from jax.experimental import pallas as pl
from jax.experimental.pallas import tpu as pltpu

F32 = jnp.float32
BF16 = jnp.bfloat16

D_MODEL = 1024
PLE_DIM = 256
D_FF = 4 * D_MODEL
EPS = 1e-6
RET_HEADS = 4
RET_DK = 128
RET_DV = 256
RET_CHUNK = 128
RET_QK = RET_HEADS * RET_DK
RET_V = RET_HEADS * RET_DV
ATT_PATTERNS = ((128, 1), (512, 4), (2048, 16))
N_GROUPS = len(ATT_PATTERNS)
ATT_HEADS = 4
ATT_HD = 128
ATT_W = ATT_HEADS * ATT_HD
ATT_BLK = 128
ROPE_THETA = 10000.0
LOG2E = 1.4426950408889634
ATT_QSCALE = ATT_HD ** -0.5 * LOG2E

LANES = 128
SUBLANES = 8
STAT_LANES = LANES // (2 * ATT_HEADS)
SEG = 512
W_RET = 2 * RET_QK + RET_V
W_GATE = RET_V + 2 * D_MODEL
W_ATT = 3 * N_GROUPS * ATT_W
VMEM_LIMIT = 62 * 1024 * 1024
W_IN_STAGE_ROWS = 16
W_IN_STAGE_SLOTS = 4
ATT_IN_BUFFERS = 3
BF16_ROWS = 16


def _resident(shape):
    return pl.BlockSpec(shape, lambda *_: (0,) * len(shape), pipeline_mode=pl.Buffered(1))


def _row_pitch(dil):
    return dil + SUBLANES if dil % (2 * SUBLANES) == 0 else dil


def _rms(x):
    return x * lax.rsqrt(jnp.mean(x * x, axis=-1, keepdims=True) + EPS)


def _sigmoid(x):
    return 0.5 * jnp.tanh(0.5 * x) + 0.5


def _rotate(a, cos, sin_signed):
    return a * cos + pltpu.roll(a, LANES // 2, axis=1) * sin_signed


def _load_w_in(w_hbm_ref, w_ref, stage_ref, sem):
    n_slots, rows, _ = stage_ref.shape
    n_blk = w_hbm_ref.shape[0] // rows
    rg = slice(W_RET, W_RET + RET_V)

    def copy(k):
        slot = k % n_slots
        return pltpu.make_async_copy(w_hbm_ref.at[pl.ds(k * rows, rows), :], stage_ref.at[slot],
                                     sem.at[slot])

    for k in range(n_slots - 1):
        copy(k).start(priority=k % 2)
    for k in range(n_blk):
        if k + n_slots - 1 < n_blk:
            nxt = k + n_slots - 1
            copy(nxt).start(priority=nxt % 2)
        copy(k).wait()
        slot = k % n_slots
        dst = slice(k * rows, (k + 1) * rows)
        w_ref[dst, :rg.start] = stage_ref[slot, :, :rg.start].astype(BF16)
        w_ref[dst, rg] = (stage_ref[slot, :, rg] * 0.5).astype(BF16)
        w_ref[dst, rg.stop:] = stage_ref[slot, :, rg.stop:].astype(BF16)


def _inproj_kernel(x_ref, g_ref, w_hbm_ref, cr_ref, sr_ref, ca_ref, sa_ref, qg_ref, kg_ref, *rest):
    n_w = (len(rest) - 9) // 2
    zr_ref, zg_ref, *za_refs = rest[n_w:n_w + 5]
    perm_ref, w_ref, stage_ref, sem = rest[-4:]
    for w_f32_ref, w_bf16_ref in zip(rest[:n_w], rest[n_w + 5:-4]):
        w_bf16_ref[...] = w_f32_ref[...].astype(BF16)

    @pl.when(pl.program_id(0) == 0)
    def _():
        _load_w_in(w_hbm_ref, w_ref, stage_ref, sem)

    tm = x_ref.shape[0]
    n_slab = D_MODEL // LANES

    def residue_major(load, dil):
        return jnp.concatenate([load(pl.ds(r, tm // dil, stride=dil)) for r in range(dil)], axis=0)

    xn = _rms(x_ref[...]) * g_ref[...]
    for c in range(n_slab):
        perm_ref[c] = xn[:, c * LANES:(c + 1) * LANES]
    lhs, cos_a, sin_a = {}, {}, {}
    for dil in sorted({d for _, d in ATT_PATTERNS}):
        if dil == 1:
            lhs[dil], cos_a[dil], sin_a[dil] = xn.astype(BF16), ca_ref[...], sa_ref[...]
            continue
        lhs[dil] = jnp.concatenate(
            [residue_major(lambda rows, c=c: perm_ref[c, rows, :], dil) for c in range(n_slab)],
            axis=1).astype(BF16)
        cos_a[dil] = residue_major(lambda rows: ca_ref[rows, :], dil)
        sin_a[dil] = residue_major(lambda rows: sa_ref[rows, :], dil)
    h = lhs[1]

    def store_att(g, seg, val):
        dil = ATT_PATTERNS[g][1]
        za_refs[g][:, :, seg * ATT_W:(seg + 1) * ATT_W] = (
            val.astype(BF16).reshape(dil, tm // dil, ATT_W))

    n_ret, n_rg, n_att = W_RET // SEG, RET_V // SEG, W_ATT // SEG
    n_all = (W_RET + W_GATE + W_ATT) // SEG
    heavy = [0, 1] + list(range(n_ret + n_rg, n_ret + n_rg + n_att))
    for j in heavy + [j for j in range(n_all) if j not in heavy]:
        w_chunk = w_ref[:, j * SEG:(j + 1) * SEG]
        if n_ret + n_rg <= j < n_ret + n_rg + n_att:
            jj = j - n_ret - n_rg
            seg, g = jj // N_GROUPS, jj % N_GROUPS
            dil = ATT_PATTERNS[g][1]
            acc = jnp.dot(lhs[dil], w_chunk, preferred_element_type=F32)
            if seg < 2:
                gain = (qg_ref if seg == 0 else kg_ref)[g:g + 1, :]
                if seg == 0:
                    gain = gain * ATT_QSCALE
                parts = []
                for hh in range(SEG // LANES):
                    a = _rms(acc[:, hh * LANES:(hh + 1) * LANES]) * gain
                    parts.append(_rotate(a, cos_a[dil], sin_a[dil]))
                store_att(g, seg, jnp.concatenate(parts, axis=1))
            else:
                store_att(g, seg, acc)
            continue
        acc = jnp.dot(h, w_chunk, preferred_element_type=F32)
        if j < 2:
            cos, sin = cr_ref[...], sr_ref[...]
            for hh in range(SEG // LANES):
                r = _rotate(acc[:, hh * LANES:(hh + 1) * LANES], cos, sin)
                if j == 1:
                    r = r * (RET_DK ** -0.5)
                zr_ref[:, j * SEG + hh * LANES:j * SEG + (hh + 1) * LANES] = r.astype(BF16)
        elif j < n_ret:
            zr_ref[:, j * SEG:(j + 1) * SEG] = acc.astype(BF16)
        elif j < n_ret + n_rg:
            jj = j - n_ret
            zg_ref[:, jj * SEG:(jj + 1) * SEG] = acc.astype(BF16)
        else:
            jj = j - n_ret - n_att
            zg_ref[:, jj * SEG:(jj + 1) * SEG] = acc.astype(BF16)


def _in_proj(x2, g_mix, w_in, tabs, q_gain, k_gain, seq, weights, tm=512):
    T = x2.shape[0]
    npos = seq // tm
    tab_spec = pl.BlockSpec((tm, LANES), lambda i: (i % npos, 0))

    def slice_spec(w):
        rows = max(w.shape[0] // (T // tm), BF16_ROWS)
        reps = rows * (T // tm) // w.shape[0]
        return pl.BlockSpec((rows, w.shape[1]), lambda i: (i // reps, 0))

    w_specs = [slice_spec(w) for w in weights]
    return pl.pallas_call(
        _inproj_kernel,
        grid=(T // tm,),
        in_specs=[
            pl.BlockSpec((tm, D_MODEL), lambda i: (i, 0)),
            _resident((1, D_MODEL)),
            pl.BlockSpec(memory_space=pl.ANY),
            tab_spec, tab_spec, tab_spec, tab_spec,
            _resident(q_gain.shape), _resident(k_gain.shape),
        ] + w_specs,
        out_specs=[
            pl.BlockSpec((tm, W_RET), lambda i: (i, 0)),
            pl.BlockSpec((tm, W_GATE), lambda i: (i, 0)),
        ] + [
            pl.BlockSpec((None, dil, tm // dil, 3 * ATT_W), lambda i: (i // npos, 0, i % npos, 0))
            for _, dil in ATT_PATTERNS
        ] + w_specs,
        out_shape=[
            jax.ShapeDtypeStruct((T, W_RET), BF16),
            jax.ShapeDtypeStruct((T, W_GATE), BF16),
        ] + [
            jax.ShapeDtypeStruct((T // seq, dil, seq // dil, 3 * ATT_W), BF16)
            for _, dil in ATT_PATTERNS
        ] + [jax.ShapeDtypeStruct(w.shape, BF16) for w in weights],
        scratch_shapes=[pltpu.VMEM((D_MODEL // LANES, tm, LANES), F32),
                        pltpu.VMEM(w_in.shape, BF16),
                        pltpu.VMEM((W_IN_STAGE_SLOTS, W_IN_STAGE_ROWS, w_in.shape[1]), F32),
                        pltpu.SemaphoreType.DMA((W_IN_STAGE_SLOTS,))],
        compiler_params=pltpu.CompilerParams(
            dimension_semantics=("arbitrary",), vmem_limit_bytes=VMEM_LIMIT),
        name="in_proj",
    )(x2, g_mix, w_in, *tabs, q_gain, k_gain, *weights)


def _retention_kernel(q_ref, k_ref, v_ref, rg_ref, gn_ref, inner_ref, qd_ref, kd_ref, cd_ref,
                      y_ref, *, n_chunks):
    C = RET_CHUNK
    inner_decay, q_decay, k_decay, chunk_decay = inner_ref[...], qd_ref[...], kd_ref[...], cd_ref[...]
    gn = gn_ref[...]
    lhs, kvs = [], []
    for c in range(n_chunks):
        rows = slice(c * C, (c + 1) * C)
        q, k, v = q_ref[rows, :], k_ref[rows, :], v_ref[rows, :]
        scores = lax.dot_general(q, k, (((1,), (1,)), ((), ())), preferred_element_type=F32) * inner_decay
        qd = (q.astype(F32) * q_decay).astype(BF16)
        lhs.append(jnp.concatenate([scores.astype(BF16), qd], axis=1))
        kd = (k.astype(F32) * k_decay).astype(BF16)
        kvs.append(lax.dot_general(kd, v, (((0,), (0,)), ((), ())), preferred_element_type=F32))
    state = jnp.zeros((RET_DK, RET_DV), F32)
    for c in range(n_chunks):
        rows = slice(c * C, (c + 1) * C)
        rhs = jnp.concatenate([v_ref[rows, :], state.astype(BF16)], axis=0)
        y = jnp.dot(lhs[c], rhs, preferred_element_type=F32)
        state = state * chunk_decay + kvs[c]
        half_rg = rg_ref[rows, :].astype(F32)
        silu = half_rg * jnp.tanh(half_rg) + half_rg
        y_ref[rows, :] = (silu * (_rms(y) * gn)).astype(BF16)


def _retention(zr, zg, ret_gn, dec):
    B, S, _ = zr.shape
    ts = S
    inner_decay, q_decay, k_decay, chunk_decay = dec
    head_tab = lambda shape: pl.BlockSpec((None,) + shape, lambda b, h, s: (h, 0, 0))
    return pl.pallas_call(
        functools.partial(_retention_kernel, n_chunks=ts // RET_CHUNK),
        grid=(B, RET_HEADS, S // ts),
        in_specs=[
            pl.BlockSpec((None, ts, RET_DK), lambda b, h, s: (b, s, h)),
            pl.BlockSpec((None, ts, RET_DK), lambda b, h, s: (b, s, RET_HEADS + h)),
            pl.BlockSpec((None, ts, RET_DV), lambda b, h, s: (b, s, 2 * RET_QK // RET_DV + h)),
            pl.BlockSpec((None, ts, RET_DV), lambda b, h, s: (b, s, h)),
            pl.BlockSpec((1, RET_DV), lambda b, h, s: (0, h)),
            head_tab((RET_CHUNK, RET_CHUNK)),
            head_tab((RET_CHUNK, RET_DK)),
            head_tab((RET_CHUNK, RET_DK)),
            head_tab((RET_DK, RET_DV)),
        ],
        out_specs=pl.BlockSpec((None, ts, RET_DV), lambda b, h, s: (b, s, h)),
        out_shape=jax.ShapeDtypeStruct((B, S, RET_V), BF16),
        compiler_params=pltpu.CompilerParams(
            dimension_semantics=("parallel", "parallel", "arbitrary"), vmem_limit_bytes=VMEM_LIMIT),
        name="retention",
    )(zr, zr, zr, zg, ret_gn, inner_decay, q_decay, k_decay, chunk_decay)


def _attn_unit(q, parts):
    nt = (((1,), (1,)), ((), ()))
    s = [lax.dot_general(q, k, nt, preferred_element_type=F32) + bias for k, _, bias in parts]
    s = s[0] if len(s) == 1 else jnp.concatenate(s, axis=1)
    m = jnp.max(s, axis=-1, keepdims=True)
    e = jnp.exp2(s - m).astype(BF16)
    ol, off = None, 0
    for _, v, _ in parts:
        n = v.shape[0]
        t = jnp.dot(e[:, off:off + n], jnp.concatenate([v, jnp.ones_like(v)], axis=1),
                    preferred_element_type=F32)
        ol = t if ol is None else ol + t
        off += n
    return ol[:, :ATT_HD], m, ol[:, ATT_HD:]


def _attention_kernel(q_ref, k_ref, v_ref, *rest, n_res, n_blk, has_prev):
    if has_prev:
        kp_ref, vp_ref, o_ref, st_ref = rest
    else:
        o_ref, st_ref = rest
    W = ATT_BLK
    neg = jnp.float32(-jnp.inf)
    row2 = lax.broadcasted_iota(jnp.int32, (W, 2 * W), 0)
    col2 = lax.broadcasted_iota(jnp.int32, (W, 2 * W), 1)
    band = jnp.where((col2 >= row2) & (col2 <= row2 + W), 0.0, neg)
    first = jnp.where(col2 <= row2, 0.0, neg)
    lane_half = lax.broadcasted_iota(jnp.int32, (W, LANES), 1) // STAT_LANES
    if has_prev:
        no_prev = W * (1 - jnp.minimum(pl.program_id(1), 1))
        prev_own = jnp.where((col2 >= row2) & (col2 <= row2 + W) & (col2 >= no_prev), 0.0, neg)

    for r in range(n_res):
        for b in range(n_blk):
            rows = slice(b * W, (b + 1) * W)
            st_tile = jnp.zeros((W, LANES), F32)
            for hh in range(ATT_HEADS):
                cols = slice(hh * ATT_HD, (hh + 1) * ATT_HD)
                q = q_ref[r, rows, cols]
                if b > 0:
                    keys = slice((b - 1) * W, (b + 1) * W)
                    parts = [(k_ref[r, keys, cols], v_ref[r, keys, cols], band)]
                elif has_prev:
                    parts = [(jnp.concatenate([kp_ref[0, :, cols], k_ref[r, rows, cols]], axis=0),
                              jnp.concatenate([vp_ref[0, :, cols], v_ref[r, rows, cols]], axis=0),
                              prev_own)]
                else:
                    keys = slice(0, 2 * W)
                    parts = [(k_ref[r, keys, cols], v_ref[r, keys, cols], first)]
                o, m, l = _attn_unit(q, parts)
                o_ref[r, rows, cols] = o.astype(BF16)
                st_tile = jnp.where(lane_half == 2 * hh, m, st_tile)
                st_tile = jnp.where(lane_half == 2 * hh + 1, l, st_tile)
            st_ref[r, rows, :] = st_tile


def _attention(za, g, rows_per_step=4096):
    window, dil = ATT_PATTERNS[g]
    assert window // dil == ATT_BLK
    batch, _, L, _ = za.shape
    has_prev = L > rows_per_step
    lc = min(L, rows_per_step)
    n_res = rows_per_step // lc
    n_blk = lc // ATT_BLK
    assert n_blk >= 2
    grid = (batch, L // lc if has_prev else dil // n_res)
    idx = (lambda b, j, seg: (b, 0, j, seg)) if has_prev else (lambda b, j, seg: (b, j, 0, seg))
    assert not has_prev
    in_specs = [pl.BlockSpec((None, n_res, lc, ATT_W), functools.partial(idx, seg=seg),
                             pipeline_mode=pl.Buffered(ATT_IN_BUFFERS))
                for seg in range(3)]
    out_specs = [pl.BlockSpec((None, n_res, lc, ATT_W), functools.partial(idx, seg=0)),
                 pl.BlockSpec((None, n_res, lc, LANES), functools.partial(idx, seg=0))]
    out_shape = [jax.ShapeDtypeStruct((batch, dil, L, ATT_W), BF16),
                 jax.ShapeDtypeStruct((batch, dil, L, LANES), F32)]
    body = functools.partial(_attention_kernel, n_res=n_res, n_blk=n_blk, has_prev=False)

    def outer(q_hbm, k_hbm, v_hbm, o_hbm, st_hbm):
        pltpu.emit_pipeline(body, grid=grid, in_specs=in_specs, out_specs=out_specs)(
            q_hbm, k_hbm, v_hbm, o_hbm, st_hbm)

    any_spec = pl.BlockSpec(memory_space=pl.ANY)
    return pl.pallas_call(
        outer, in_specs=[any_spec] * 3, out_specs=[any_spec] * 2, out_shape=out_shape,
        compiler_params=pltpu.CompilerParams(vmem_limit_bytes=VMEM_LIMIT),
        name=f"attention_g{g}",
    )(za, za, za)


def _merge_mlp_kernel(x_ref, y_ref, o0_ref, o1_ref, o2_ref, s0_ref, s1_ref, s2_ref, gr_ref, ga_ref,
                      bg_ref, wr_ref, wa_ref, wo_ref, p_ref, gm_ref, wu_ref, wd_ref, gp_ref, wpg_ref,
                      wpp_ref, out_ref, *perm_refs, ff_chunk):
    tm = x_ref.shape[0]

    def token_order(ref, scratch):
        dil = ref.shape[0]
        if dil == 1:
            return ref[0].astype(F32)
        n_slab = ref.shape[2] // LANES
        pitch = _row_pitch(dil)
        for r in range(dil):
            for c in range(n_slab):
                scratch[c, pl.ds(r, tm // dil, stride=pitch), :] = (
                    ref[r, :, c * LANES:(c + 1) * LANES].astype(F32))
        if pitch == dil:
            return jnp.concatenate([scratch[c] for c in range(n_slab)], axis=1)
        return jnp.concatenate(
            [jnp.concatenate([scratch[c, l * pitch:l * pitch + dil, :] for c in range(n_slab)], axis=1)
             for l in range(tm // dil)], axis=0)

    proj = jnp.dot(p_ref[...].astype(BF16), wpp_ref[...], preferred_element_type=F32)
    ret_branch = jnp.dot(y_ref[...], wr_ref[...], preferred_element_type=F32)
    bg = bg_ref[...]
    gate_r = _sigmoid(gr_ref[...].astype(F32) + bg[:, :D_MODEL])
    gate_a = _sigmoid(ga_ref[...].astype(F32) + bg[:, D_MODEL:])
    o_tok = [token_order(o0_ref, None), token_order(o1_ref, perm_refs[0]),
             token_order(o2_ref, perm_refs[1])]
    stats = [token_order(s0_ref, None), token_order(s1_ref, perm_refs[2]),
             token_order(s2_ref, perm_refs[3])]
    sum_lane = (lax.broadcasted_iota(jnp.int32, (tm, LANES), 1) // STAT_LANES) % 2 == 1
    ms = stats
    lses = [st + pltpu.roll(jnp.log2(jnp.where(sum_lane, st, 1.0)), LANES - STAT_LANES, axis=1)
            for st in stats]
    top = jnp.maximum(jnp.maximum(lses[0], lses[1]), lses[2])
    den = jnp.exp2(lses[0] - top) + jnp.exp2(lses[1] - top) + jnp.exp2(lses[2] - top)
    wts = [jnp.exp2(m - top) / den for m in ms]
    per_head = LANES // ATT_HEADS
    parts = []
    for hh in range(ATT_HEADS):
        cols = slice(hh * ATT_HD, (hh + 1) * ATT_HD)
        acc = None
        for w, o_g in zip(wts, o_tok):
            t = w[:, hh * per_head:hh * per_head + 1] * o_g[:, cols]
            acc = t if acc is None else acc + t
        parts.append(acc)
    o = jnp.concatenate(parts, axis=1).astype(BF16)
    att_branch = jnp.dot(o, wa_ref[...], preferred_element_type=F32)
    mix = (gate_r * ret_branch + gate_a * att_branch).astype(BF16)
    x1 = x_ref[...] + jnp.dot(mix, wo_ref[...], preferred_element_type=F32)

    hn = (_rms(x1) * gm_ref[...]).astype(BF16)
    acc = x1
    for c in range(D_FF // ff_chunk):
        cols = slice(c * ff_chunk, (c + 1) * ff_chunk)
        u = jnp.dot(hn, wu_ref[:, cols], preferred_element_type=F32)
        a = jnp.square(jnp.maximum(u, 0.0)).astype(BF16)
        acc = acc + jnp.dot(a, wd_ref[cols, :], preferred_element_type=F32)
    hp = (_rms(acc) * gp_ref[...]).astype(BF16)
    gate_p = _sigmoid(jnp.dot(hp, wpg_ref[...], preferred_element_type=F32))
    out_ref[...] = acc + gate_p * proj


def _merge_mlp(x2, yg, os_, ss_, zg, b_gate, w_ret_out, w_att_out, w_o, p2, g_mlp, w_up, w_down,
               g_ple, w_pg, w_pp, tm=512, ff_chunk=1024):
    T = x2.shape[0]
    npos = os_[0].shape[1] * os_[0].shape[2] // tm
    row = lambda w: pl.BlockSpec((tm, w), lambda i: (i, 0))
    res_major = lambda a: pl.BlockSpec((None, a.shape[1], tm // a.shape[1], a.shape[3]),
                                       lambda i: (i // npos, 0, i % npos, 0))
    dilated = [a for a in os_ + ss_ if a.shape[1] > 1]
    return pl.pallas_call(
        functools.partial(_merge_mlp_kernel, ff_chunk=ff_chunk),
        grid=(T // tm,),
        in_specs=[
            row(D_MODEL), row(RET_V), *[res_major(a) for a in os_ + ss_],
            pl.BlockSpec((tm, D_MODEL), lambda i: (i, RET_V // D_MODEL)),
            pl.BlockSpec((tm, D_MODEL), lambda i: (i, RET_V // D_MODEL + 1)),
            _resident(b_gate.shape), _resident(w_ret_out.shape), _resident(w_att_out.shape),
            _resident(w_o.shape),
            row(PLE_DIM), _resident((1, D_MODEL)), _resident(w_up.shape), _resident(w_down.shape),
            _resident((1, D_MODEL)), _resident(w_pg.shape), _resident(w_pp.shape),
        ],
        out_specs=row(D_MODEL),
        out_shape=jax.ShapeDtypeStruct((T, D_MODEL), F32),
        scratch_shapes=[pltpu.VMEM((a.shape[3] // LANES, tm // a.shape[1] * _row_pitch(a.shape[1]), LANES),
                                   F32) for a in dilated],
        compiler_params=pltpu.CompilerParams(
            dimension_semantics=("parallel",), vmem_limit_bytes=VMEM_LIMIT),
        name="merge_mlp",
    )(x2, yg, *os_, *ss_, zg, zg, b_gate, w_ret_out, w_att_out, w_o, p2, g_mlp, w_up, w_down, g_ple,
      w_pg, w_pp)


def _rot_tables(inv_freq, seq):
    ang = np.arange(seq, dtype=np.float64)[:, None] * inv_freq[None, :]
    cos, sin = np.cos(ang), np.sin(ang)
    return (np.concatenate([cos, cos], axis=1).astype(np.float32),
            np.concatenate([-sin, sin], axis=1).astype(np.float32))


def _decay_tables():
    H, C = RET_HEADS, RET_CHUNK
    log_g = np.log1p(-np.exp2(-5.0 - np.arange(H, dtype=np.float64)))
    idx = np.arange(C, dtype=np.float64)
    diff = idx[:, None] - idx[None, :]
    inner = np.where(diff >= 0, np.exp(log_g[:, None, None] * np.maximum(diff, 0.0)), 0.0)
    q_decay = np.exp(log_g[:, None] * (idx[None, :] + 1.0))
    k_decay = np.exp(log_g[:, None] * (C - 1.0 - idx[None, :]))
    chunk_decay = np.exp(log_g * C)
    f32 = lambda a: np.ascontiguousarray(a, dtype=np.float32)
    return (f32(inner),
            f32(np.broadcast_to(q_decay[:, :, None], (H, C, RET_DK))),
            f32(np.broadcast_to(k_decay[:, :, None], (H, C, RET_DK))),
            f32(np.broadcast_to(chunk_decay[:, None, None], (H, RET_DK, RET_DV))))


def _layer(x, p_i, w_in, b_gate, g_mix, q_gain, k_gain, ret_gn, w_ret_out, w_att_out, w_o,
           g_mlp, w_up, w_down, g_ple, w_ple_proj, w_ple_gate):
    B, S, D = x.shape
    T = B * S
    x2 = x.reshape(T, D)
    ret_freq = 1.0 / (10000.0 ** np.linspace(0.0, 1.0, RET_DK // 2))
    rope_freq = ROPE_THETA ** (-np.arange(0, ATT_HD, 2, dtype=np.float64) / ATT_HD)
    tabs = _rot_tables(ret_freq, S) + _rot_tables(rope_freq, S)

    zr, zg, za0, za1, za2, w_ret_out, w_att_out, w_o, w_up, w_down, w_ple_gate, w_ple_proj = _in_proj(
        x2, g_mix.reshape(1, D), w_in, tabs, q_gain, k_gain, S,
        [w_ret_out, w_att_out, w_o, w_up, w_down, w_ple_gate, w_ple_proj])
    zas = (za0, za1, za2)
    yg = _retention(zr.reshape(B, S, W_RET), zg.reshape(B, S, W_GATE), ret_gn.reshape(1, RET_V),
                    _decay_tables())
    att = [_attention(zas[g], g) for g in range(N_GROUPS)]
    out = _merge_mlp(x2, yg.reshape(T, RET_V), [o for o, _ in att], [s for _, s in att], zg,
                     b_gate.reshape(1, -1), w_ret_out, w_att_out, w_o, p_i.reshape(T, PLE_DIM),
                     g_mlp.reshape(1, D), w_up, w_down, g_ple.reshape(1, D), w_ple_gate, w_ple_proj)
    return out.reshape(B, S, D)


def kernel(x, p, w_in, b_gate, g_mix, q_gain, k_gain, ret_gn, w_ret_out, w_att_out, w_o, g_mlp, w_up,
           w_down, g_ple, w_ple_proj, w_ple_gate):
    for i in range(p.shape[0]):
        x = _layer(x, p[i], w_in[i], b_gate[i], g_mix[i], q_gain[i], k_gain[i], ret_gn[i],
                   w_ret_out[i], w_att_out[i], w_o[i], g_mlp[i], w_up[i], w_down[i], g_ple[i],
                   w_ple_proj[i], w_ple_gate[i])
    return x
```

```python
import functools

import jax
import jax.numpy as jnp
import numpy as np
from jax import lax
from jax.experimental import pallas as pl
from jax.experimental.pallas import tpu as pltpu

F32 = jnp.float32
BF16 = jnp.bfloat16

D_MODEL = 1024
PLE_DIM = 256
D_FF = 4 * D_MODEL
EPS = 1e-6
RET_HEADS = 4
RET_DK = 128
RET_DV = 256
RET_CHUNK = 128
RET_QK = RET_HEADS * RET_DK
RET_V = RET_HEADS * RET_DV
ATT_PATTERNS = ((128, 1), (512, 4), (2048, 16))
N_GROUPS = len(ATT_PATTERNS)
ATT_HEADS = 4
ATT_HD = 128
ATT_W = ATT_HEADS * ATT_HD
ATT_BLK = 128
ROPE_THETA = 10000.0
LOG2E = 1.4426950408889634
ATT_QSCALE = ATT_HD ** -0.5 * LOG2E

LANES = 128
SUBLANES = 8
STAT_LANES = LANES // (2 * ATT_HEADS)
SEG = 512
W_RET = 2 * RET_QK + RET_V
W_GATE = RET_V + 2 * D_MODEL
W_ATT = 3 * N_GROUPS * ATT_W
VMEM_LIMIT = 62 * 1024 * 1024
W_IN_STAGE_ROWS = 16
W_IN_STAGE_SLOTS = 4
ATT_IN_BUFFERS = 3
BF16_ROWS = 16


def _resident(shape):
    return pl.BlockSpec(shape, lambda *_: (0,) * len(shape), pipeline_mode=pl.Buffered(1))


def _row_pitch(dil):
    return dil + SUBLANES if dil % (2 * SUBLANES) == 0 else dil


def _rms(x):
    return x * lax.rsqrt(jnp.mean(x * x, axis=-1, keepdims=True) + EPS)


def _sigmoid(x):
    return 0.5 * jnp.tanh(0.5 * x) + 0.5


def _rotate(a, cos, sin_signed):
    return a * cos + pltpu.roll(a, LANES // 2, axis=1) * sin_signed


def _load_w_in(w_hbm_ref, w_ref, stage_ref, sem):
    n_slots, rows, _ = stage_ref.shape
    n_blk = w_hbm_ref.shape[0] // rows
    rg = slice(W_RET, W_RET + RET_V)

    def copy(k):
        slot = k % n_slots
        return pltpu.make_async_copy(w_hbm_ref.at[pl.ds(k * rows, rows), :], stage_ref.at[slot],
                                     sem.at[slot])

    for k in range(n_slots - 1):
        copy(k).start()
    for k in range(n_blk):
        if k + n_slots - 1 < n_blk:
            copy(k + n_slots - 1).start()
        copy(k).wait()
        slot = k % n_slots
        dst = slice(k * rows, (k + 1) * rows)
        w_ref[dst, :rg.start] = stage_ref[slot, :, :rg.start].astype(BF16)
        w_ref[dst, rg] = (stage_ref[slot, :, rg] * 0.5).astype(BF16)
        w_ref[dst, rg.stop:] = stage_ref[slot, :, rg.stop:].astype(BF16)


def _inproj_kernel(x_ref, g_ref, w_hbm_ref, cr_ref, sr_ref, ca_ref, sa_ref, qg_ref, kg_ref, *rest):
    n_w = (len(rest) - 9) // 2
    zr_ref, zg_ref, *za_refs = rest[n_w:n_w + 5]
    perm_ref, w_ref, stage_ref, sem = rest[-4:]
    for w_f32_ref, w_bf16_ref in zip(rest[:n_w], rest[n_w + 5:-4]):
        w_bf16_ref[...] = w_f32_ref[...].astype(BF16)

    @pl.when(pl.program_id(0) == 0)
    def _():
        _load_w_in(w_hbm_ref, w_ref, stage_ref, sem)

    tm = x_ref.shape[0]
    n_slab = D_MODEL // LANES

    def residue_major(load, dil):
        return jnp.concatenate([load(pl.ds(r, tm // dil, stride=dil)) for r in range(dil)], axis=0)

    xn = _rms(x_ref[...]) * g_ref[...]
    for c in range(n_slab):
        perm_ref[c] = xn[:, c * LANES:(c + 1) * LANES]
    lhs, cos_a, sin_a = {}, {}, {}
    for dil in sorted({d for _, d in ATT_PATTERNS}):
        if dil == 1:
            lhs[dil], cos_a[dil], sin_a[dil] = xn.astype(BF16), ca_ref[...], sa_ref[...]
            continue
        lhs[dil] = jnp.concatenate(
            [residue_major(lambda rows, c=c: perm_ref[c, rows, :], dil) for c in range(n_slab)],
            axis=1).astype(BF16)
        cos_a[dil] = residue_major(lambda rows: ca_ref[rows, :], dil)
        sin_a[dil] = residue_major(lambda rows: sa_ref[rows, :], dil)
    h = lhs[1]

    def store_att(g, seg, val):
        dil = ATT_PATTERNS[g][1]
        za_refs[g][:, :, seg * ATT_W:(seg + 1) * ATT_W] = (
            val.astype(BF16).reshape(dil, tm // dil, ATT_W))

    n_ret, n_rg, n_att = W_RET // SEG, RET_V // SEG, W_ATT // SEG
    n_all = (W_RET + W_GATE + W_ATT) // SEG
    heavy = [0, 1] + list(range(n_ret + n_rg, n_ret + n_rg + n_att))
    for j in heavy + [j for j in range(n_all) if j not in heavy]:
        w_chunk = w_ref[:, j * SEG:(j + 1) * SEG]
        if n_ret + n_rg <= j < n_ret + n_rg + n_att:
            jj = j - n_ret - n_rg
            seg, g = jj // N_GROUPS, jj % N_GROUPS
            dil = ATT_PATTERNS[g][1]
            acc = jnp.dot(lhs[dil], w_chunk, preferred_element_type=F32)
            if seg < 2:
                gain = (qg_ref if seg == 0 else kg_ref)[g:g + 1, :]
                if seg == 0:
                    gain = gain * ATT_QSCALE
                parts = []
                for hh in range(SEG // LANES):
                    a = _rms(acc[:, hh * LANES:(hh + 1) * LANES]) * gain
                    parts.append(_rotate(a, cos_a[dil], sin_a[dil]))
                store_att(g, seg, jnp.concatenate(parts, axis=1))
            else:
                store_att(g, seg, acc)
            continue
        acc = jnp.dot(h, w_chunk, preferred_element_type=F32)
        if j < 2:
            cos, sin = cr_ref[...], sr_ref[...]
            for hh in range(SEG // LANES):
                r = _rotate(acc[:, hh * LANES:(hh + 1) * LANES], cos, sin)
                if j == 1:
                    r = r * (RET_DK ** -0.5)
                zr_ref[:, j * SEG + hh * LANES:j * SEG + (hh + 1) * LANES] = r.astype(BF16)
        elif j < n_ret:
            zr_ref[:, j * SEG:(j + 1) * SEG] = acc.astype(BF16)
        elif j < n_ret + n_rg:
            jj = j - n_ret
            zg_ref[:, jj * SEG:(jj + 1) * SEG] = acc.astype(BF16)
        else:
            jj = j - n_ret - n_att
            zg_ref[:, jj * SEG:(jj + 1) * SEG] = acc.astype(BF16)


def _in_proj(x2, g_mix, w_in, tabs, q_gain, k_gain, seq, weights, tm=512):
    T = x2.shape[0]
    npos = seq // tm
    tab_spec = pl.BlockSpec((tm, LANES), lambda i: (i % npos, 0))

    def slice_spec(w):
        rows = max(w.shape[0] // (T // tm), BF16_ROWS)
        reps = rows * (T // tm) // w.shape[0]
        return pl.BlockSpec((rows, w.shape[1]), lambda i: (i // reps, 0))

    w_specs = [slice_spec(w) for w in weights]
    return pl.pallas_call(
        _inproj_kernel,
        grid=(T // tm,),
        in_specs=[
            pl.BlockSpec((tm, D_MODEL), lambda i: (i, 0)),
            _resident((1, D_MODEL)),
            pl.BlockSpec(memory_space=pl.ANY),
            tab_spec, tab_spec, tab_spec, tab_spec,
            _resident(q_gain.shape), _resident(k_gain.shape),
        ] + w_specs,
        out_specs=[
            pl.BlockSpec((tm, W_RET), lambda i: (i, 0)),
            pl.BlockSpec((tm, W_GATE), lambda i: (i, 0)),
        ] + [
            pl.BlockSpec((None, dil, tm // dil, 3 * ATT_W), lambda i: (i // npos, 0, i % npos, 0))
            for _, dil in ATT_PATTERNS
        ] + w_specs,
        out_shape=[
            jax.ShapeDtypeStruct((T, W_RET), BF16),
            jax.ShapeDtypeStruct((T, W_GATE), BF16),
        ] + [
            jax.ShapeDtypeStruct((T // seq, dil, seq // dil, 3 * ATT_W), BF16)
            for _, dil in ATT_PATTERNS
        ] + [jax.ShapeDtypeStruct(w.shape, BF16) for w in weights],
        scratch_shapes=[pltpu.VMEM((D_MODEL // LANES, tm, LANES), F32),
                        pltpu.VMEM(w_in.shape, BF16),
                        pltpu.VMEM((W_IN_STAGE_SLOTS, W_IN_STAGE_ROWS, w_in.shape[1]), F32),
                        pltpu.SemaphoreType.DMA((W_IN_STAGE_SLOTS,))],
        compiler_params=pltpu.CompilerParams(
            dimension_semantics=("arbitrary",), vmem_limit_bytes=VMEM_LIMIT),
        name="in_proj",
    )(x2, g_mix, w_in, *tabs, q_gain, k_gain, *weights)


def _retention_kernel(q_ref, k_ref, v_ref, rg_ref, gn_ref, inner_ref, qd_ref, kd_ref, cd_ref,
                      y_ref, *, n_chunks):
    C = RET_CHUNK
    inner_decay, q_decay, k_decay, chunk_decay = inner_ref[...], qd_ref[...], kd_ref[...], cd_ref[...]
    gn = gn_ref[...]
    lhs, kvs = [], []
    for c in range(n_chunks):
        rows = slice(c * C, (c + 1) * C)
        q, k, v = q_ref[rows, :], k_ref[rows, :], v_ref[rows, :]
        scores = lax.dot_general(q, k, (((1,), (1,)), ((), ())), preferred_element_type=F32) * inner_decay
        qd = (q.astype(F32) * q_decay).astype(BF16)
        lhs.append(jnp.concatenate([scores.astype(BF16), qd], axis=1))
        kd = (k.astype(F32) * k_decay).astype(BF16)
        kvs.append(lax.dot_general(kd, v, (((0,), (0,)), ((), ())), preferred_element_type=F32))
    state = jnp.zeros((RET_DK, RET_DV), F32)
    for c in range(n_chunks):
        rows = slice(c * C, (c + 1) * C)
        rhs = jnp.concatenate([v_ref[rows, :], state.astype(BF16)], axis=0)
        y = jnp.dot(lhs[c], rhs, preferred_element_type=F32)
        state = state * chunk_decay + kvs[c]
        half_rg = rg_ref[rows, :].astype(F32)
        silu = half_rg * jnp.tanh(half_rg) + half_rg
        y_ref[rows, :] = (silu * (_rms(y) * gn)).astype(BF16)


def _retention(zr, zg, ret_gn, dec):
    B, S, _ = zr.shape
    ts = S
    inner_decay, q_decay, k_decay, chunk_decay = dec
    head_tab = lambda shape: pl.BlockSpec((None,) + shape, lambda b, h, s: (h, 0, 0))
    return pl.pallas_call(
        functools.partial(_retention_kernel, n_chunks=ts // RET_CHUNK),
        grid=(B, RET_HEADS, S // ts),
        in_specs=[
            pl.BlockSpec((None, ts, RET_DK), lambda b, h, s: (b, s, h)),
            pl.BlockSpec((None, ts, RET_DK), lambda b, h, s: (b, s, RET_HEADS + h)),
            pl.BlockSpec((None, ts, RET_DV), lambda b, h, s: (b, s, 2 * RET_QK // RET_DV + h)),
            pl.BlockSpec((None, ts, RET_DV), lambda b, h, s: (b, s, h)),
            pl.BlockSpec((1, RET_DV), lambda b, h, s: (0, h)),
            head_tab((RET_CHUNK, RET_CHUNK)),
            head_tab((RET_CHUNK, RET_DK)),
            head_tab((RET_CHUNK, RET_DK)),
            head_tab((RET_DK, RET_DV)),
        ],
        out_specs=pl.BlockSpec((None, ts, RET_DV), lambda b, h, s: (b, s, h)),
        out_shape=jax.ShapeDtypeStruct((B, S, RET_V), BF16),
        compiler_params=pltpu.CompilerParams(
            dimension_semantics=("parallel", "parallel", "arbitrary"), vmem_limit_bytes=VMEM_LIMIT),
        name="retention",
    )(zr, zr, zr, zg, ret_gn, inner_decay, q_decay, k_decay, chunk_decay)


def _attn_unit(q, parts):
    nt = (((1,), (1,)), ((), ()))
    s = [lax.dot_general(q, k, nt, preferred_element_type=F32) + bias for k, _, bias in parts]
    s = s[0] if len(s) == 1 else jnp.concatenate(s, axis=1)
    m = jnp.max(s, axis=-1, keepdims=True)
    e = jnp.exp2(s - m).astype(BF16)
    ol, off = None, 0
    for _, v, _ in parts:
        n = v.shape[0]
        t = jnp.dot(e[:, off:off + n], jnp.concatenate([v, jnp.ones_like(v)], axis=1),
                    preferred_element_type=F32)
        ol = t if ol is None else ol + t
        off += n
    return ol[:, :ATT_HD], m, ol[:, ATT_HD:]


def _attention_kernel(q_ref, k_ref, v_ref, *rest, n_res, n_blk, has_prev):
    if has_prev:
        kp_ref, vp_ref, o_ref, st_ref = rest
    else:
        o_ref, st_ref = rest
    W = ATT_BLK
    neg = jnp.float32(-jnp.inf)
    row2 = lax.broadcasted_iota(jnp.int32, (W, 2 * W), 0)
    col2 = lax.broadcasted_iota(jnp.int32, (W, 2 * W), 1)
    band = jnp.where((col2 >= row2) & (col2 <= row2 + W), 0.0, neg)
    first = jnp.where(col2 <= row2, 0.0, neg)
    lane_half = lax.broadcasted_iota(jnp.int32, (W, LANES), 1) // STAT_LANES
    if has_prev:
        no_prev = W * (1 - jnp.minimum(pl.program_id(1), 1))
        prev_own = jnp.where((col2 >= row2) & (col2 <= row2 + W) & (col2 >= no_prev), 0.0, neg)

    for r in range(n_res):
        for b in range(n_blk):
            rows = slice(b * W, (b + 1) * W)
            st_tile = jnp.zeros((W, LANES), F32)
            for hh in range(ATT_HEADS):
                cols = slice(hh * ATT_HD, (hh + 1) * ATT_HD)
                q = q_ref[r, rows, cols]
                if b > 0:
                    keys = slice((b - 1) * W, (b + 1) * W)
                    parts = [(k_ref[r, keys, cols], v_ref[r, keys, cols], band)]
                elif has_prev:
                    parts = [(jnp.concatenate([kp_ref[0, :, cols], k_ref[r, rows, cols]], axis=0),
                              jnp.concatenate([vp_ref[0, :, cols], v_ref[r, rows, cols]], axis=0),
                              prev_own)]
                else:
                    keys = slice(0, 2 * W)
                    parts = [(k_ref[r, keys, cols], v_ref[r, keys, cols], first)]
                o, m, l = _attn_unit(q, parts)
                o_ref[r, rows, cols] = o.astype(BF16)
                st_tile = jnp.where(lane_half == 2 * hh, m, st_tile)
                st_tile = jnp.where(lane_half == 2 * hh + 1, l, st_tile)
            st_ref[r, rows, :] = st_tile


def _attention(za, g, rows_per_step=4096):
    window, dil = ATT_PATTERNS[g]
    assert window // dil == ATT_BLK
    batch, _, L, _ = za.shape
    has_prev = L > rows_per_step
    lc = min(L, rows_per_step)
    n_res = rows_per_step // lc
    n_blk = lc // ATT_BLK
    assert n_blk >= 2
    grid = (batch, L // lc if has_prev else dil // n_res)
    idx = (lambda b, j, seg: (b, 0, j, seg)) if has_prev else (lambda b, j, seg: (b, j, 0, seg))
    assert not has_prev
    in_specs = [pl.BlockSpec((None, n_res, lc, 3 * ATT_W), functools.partial(idx, seg=0),
                             pipeline_mode=pl.Buffered(ATT_IN_BUFFERS))]
    out_specs = [pl.BlockSpec((None, n_res, lc, ATT_W), functools.partial(idx, seg=0)),
                 pl.BlockSpec((None, n_res, lc, LANES), functools.partial(idx, seg=0))]
    out_shape = [jax.ShapeDtypeStruct((batch, dil, L, ATT_W), BF16),
                 jax.ShapeDtypeStruct((batch, dil, L, LANES), F32)]

    def body(qkv_ref, o_ref, st_ref):
        q_ref, k_ref, v_ref = (qkv_ref.at[:, :, seg * ATT_W:(seg + 1) * ATT_W] for seg in range(3))
        _attention_kernel(q_ref, k_ref, v_ref, o_ref, st_ref, n_res=n_res, n_blk=n_blk, has_prev=False)

    def outer(qkv_hbm, o_hbm, st_hbm):
        pltpu.emit_pipeline(body, grid=grid, in_specs=in_specs, out_specs=out_specs)(
            qkv_hbm, o_hbm, st_hbm)

    any_spec = pl.BlockSpec(memory_space=pl.ANY)
    return pl.pallas_call(
        outer, in_specs=[any_spec], out_specs=[any_spec] * 2, out_shape=out_shape,
        compiler_params=pltpu.CompilerParams(vmem_limit_bytes=VMEM_LIMIT),
        name=f"attention_g{g}",
    )(za)


def _merge_mlp_kernel(x_ref, y_ref, o0_ref, o1_ref, o2_ref, s0_ref, s1_ref, s2_ref, gr_ref, ga_ref,
                      bg_ref, wr_ref, wa_ref, wo_ref, p_ref, gm_ref, wu_ref, wd_ref, gp_ref, wpg_ref,
                      wpp_ref, out_ref, *perm_refs, ff_chunk):
    tm = x_ref.shape[0]

    def token_order(ref, scratch):
        dil = ref.shape[0]
        if dil == 1:
            return ref[0].astype(F32)
        n_slab = ref.shape[2] // LANES
        pitch = _row_pitch(dil)
        for r in range(dil):
            for c in range(n_slab):
                scratch[c, pl.ds(r, tm // dil, stride=pitch), :] = (
                    ref[r, :, c * LANES:(c + 1) * LANES].astype(F32))
        if pitch == dil:
            return jnp.concatenate([scratch[c] for c in range(n_slab)], axis=1)
        return jnp.concatenate(
            [jnp.concatenate([scratch[c, l * pitch:l * pitch + dil, :] for c in range(n_slab)], axis=1)
             for l in range(tm // dil)], axis=0)

    proj = jnp.dot(p_ref[...].astype(BF16), wpp_ref[...], preferred_element_type=F32)
    ret_branch = jnp.dot(y_ref[...], wr_ref[...], preferred_element_type=F32)
    bg = bg_ref[...]
    gate_r = _sigmoid(gr_ref[...].astype(F32) + bg[:, :D_MODEL])
    gate_a = _sigmoid(ga_ref[...].astype(F32) + bg[:, D_MODEL:])
    o_tok = [token_order(o0_ref, None), token_order(o1_ref, perm_refs[0]),
             token_order(o2_ref, perm_refs[1])]
    stats = [token_order(s0_ref, None), token_order(s1_ref, perm_refs[2]),
             token_order(s2_ref, perm_refs[3])]
    sum_lane = (lax.broadcasted_iota(jnp.int32, (tm, LANES), 1) // STAT_LANES) % 2 == 1
    ms = stats
    lses = [st + pltpu.roll(jnp.log2(jnp.where(sum_lane, st, 1.0)), LANES - STAT_LANES, axis=1)
            for st in stats]
    top = jnp.maximum(jnp.maximum(lses[0], lses[1]), lses[2])
    den = jnp.exp2(lses[0] - top) + jnp.exp2(lses[1] - top) + jnp.exp2(lses[2] - top)
    wts = [jnp.exp2(m - top) / den for m in ms]
    per_head = LANES // ATT_HEADS
    parts = []
    for hh in range(ATT_HEADS):
        cols = slice(hh * ATT_HD, (hh + 1) * ATT_HD)
        acc = None
        for w, o_g in zip(wts, o_tok):
            t = w[:, hh * per_head:hh * per_head + 1] * o_g[:, cols]
            acc = t if acc is None else acc + t
        parts.append(acc)
    o = jnp.concatenate(parts, axis=1).astype(BF16)
    att_branch = jnp.dot(o, wa_ref[...], preferred_element_type=F32)
    mix = (gate_r * ret_branch + gate_a * att_branch).astype(BF16)
    x1 = x_ref[...] + jnp.dot(mix, wo_ref[...], preferred_element_type=F32)

    hn = (_rms(x1) * gm_ref[...]).astype(BF16)
    acc = x1
    for c in range(D_FF // ff_chunk):
        cols = slice(c * ff_chunk, (c + 1) * ff_chunk)
        u = jnp.dot(hn, wu_ref[:, cols], preferred_element_type=F32)
        a = jnp.square(jnp.maximum(u, 0.0)).astype(BF16)
        acc = acc + jnp.dot(a, wd_ref[cols, :], preferred_element_type=F32)
    hp = (_rms(acc) * gp_ref[...]).astype(BF16)
    gate_p = _sigmoid(jnp.dot(hp, wpg_ref[...], preferred_element_type=F32))
    out_ref[...] = acc + gate_p * proj


def _merge_mlp(x2, yg, os_, ss_, zg, b_gate, w_ret_out, w_att_out, w_o, p2, g_mlp, w_up, w_down,
               g_ple, w_pg, w_pp, tm=512, ff_chunk=1024):
    T = x2.shape[0]
    npos = os_[0].shape[1] * os_[0].shape[2] // tm
    row = lambda w: pl.BlockSpec((tm, w), lambda i: (i, 0))
    res_major = lambda a: pl.BlockSpec((None, a.shape[1], tm // a.shape[1], a.shape[3]),
                                       lambda i: (i // npos, 0, i % npos, 0))
    dilated = [a for a in os_ + ss_ if a.shape[1] > 1]
    return pl.pallas_call(
        functools.partial(_merge_mlp_kernel, ff_chunk=ff_chunk),
        grid=(T // tm,),
        in_specs=[
            row(D_MODEL), row(RET_V), *[res_major(a) for a in os_ + ss_],
            pl.BlockSpec((tm, D_MODEL), lambda i: (i, RET_V // D_MODEL)),
            pl.BlockSpec((tm, D_MODEL), lambda i: (i, RET_V // D_MODEL + 1)),
            _resident(b_gate.shape), _resident(w_ret_out.shape), _resident(w_att_out.shape),
            _resident(w_o.shape),
            row(PLE_DIM), _resident((1, D_MODEL)), _resident(w_up.shape), _resident(w_down.shape),
            _resident((1, D_MODEL)), _resident(w_pg.shape), _resident(w_pp.shape),
        ],
        out_specs=row(D_MODEL),
        out_shape=jax.ShapeDtypeStruct((T, D_MODEL), F32),
        scratch_shapes=[pltpu.VMEM((a.shape[3] // LANES, tm // a.shape[1] * _row_pitch(a.shape[1]), LANES),
                                   F32) for a in dilated],
        compiler_params=pltpu.CompilerParams(
            dimension_semantics=("parallel",), vmem_limit_bytes=VMEM_LIMIT),
        name="merge_mlp",
    )(x2, yg, *os_, *ss_, zg, zg, b_gate, w_ret_out, w_att_out, w_o, p2, g_mlp, w_up, w_down, g_ple,
      w_pg, w_pp)


def _rot_tables(inv_freq, seq):
    ang = np.arange(seq, dtype=np.float64)[:, None] * inv_freq[None, :]
    cos, sin = np.cos(ang), np.sin(ang)
    return (np.concatenate([cos, cos], axis=1).astype(np.float32),
            np.concatenate([-sin, sin], axis=1).astype(np.float32))


def _decay_tables():
    H, C = RET_HEADS, RET_CHUNK
    log_g = np.log1p(-np.exp2(-5.0 - np.arange(H, dtype=np.float64)))
    idx = np.arange(C, dtype=np.float64)
    diff = idx[:, None] - idx[None, :]
    inner = np.where(diff >= 0, np.exp(log_g[:, None, None] * np.maximum(diff, 0.0)), 0.0)
    q_decay = np.exp(log_g[:, None] * (idx[None, :] + 1.0))
    k_decay = np.exp(log_g[:, None] * (C - 1.0 - idx[None, :]))
    chunk_decay = np.exp(log_g * C)
    f32 = lambda a: np.ascontiguousarray(a, dtype=np.float32)
    return (f32(inner),
            f32(np.broadcast_to(q_decay[:, :, None], (H, C, RET_DK))),
            f32(np.broadcast_to(k_decay[:, :, None], (H, C, RET_DK))),
            f32(np.broadcast_to(chunk_decay[:, None, None], (H, RET_DK, RET_DV))))


def _layer(x, p_i, w_in, b_gate, g_mix, q_gain, k_gain, ret_gn, w_ret_out, w_att_out, w_o,
           g_mlp, w_up, w_down, g_ple, w_ple_proj, w_ple_gate):
    B, S, D = x.shape
    T = B * S
    x2 = x.reshape(T, D)
    ret_freq = 1.0 / (10000.0 ** np.linspace(0.0, 1.0, RET_DK // 2))
    rope_freq = ROPE_THETA ** (-np.arange(0, ATT_HD, 2, dtype=np.float64) / ATT_HD)
    tabs = _rot_tables(ret_freq, S) + _rot_tables(rope_freq, S)

    zr, zg, za0, za1, za2, w_ret_out, w_att_out, w_o, w_up, w_down, w_ple_gate, w_ple_proj = _in_proj(
        x2, g_mix.reshape(1, D), w_in, tabs, q_gain, k_gain, S,
        [w_ret_out, w_att_out, w_o, w_up, w_down, w_ple_gate, w_ple_proj])
    zas = (za0, za1, za2)
    yg = _retention(zr.reshape(B, S, W_RET), zg.reshape(B, S, W_GATE), ret_gn.reshape(1, RET_V),
                    _decay_tables())
    att = [_attention(zas[g], g) for g in range(N_GROUPS)]
    out = _merge_mlp(x2, yg.reshape(T, RET_V), [o for o, _ in att], [s for _, s in att], zg,
                     b_gate.reshape(1, -1), w_ret_out, w_att_out, w_o, p_i.reshape(T, PLE_DIM),
                     g_mlp.reshape(1, D), w_up, w_down, g_ple.reshape(1, D), w_ple_gate, w_ple_proj)
    return out.reshape(B, S, D)


def kernel(x, p, w_in, b_gate, g_mix, q_gain, k_gain, ret_gn, w_ret_out, w_att_out, w_o, g_mlp, w_up,
           w_down, g_ple, w_ple_proj, w_ple_gate):
    for i in range(p.shape[0]):
        x = _layer(x, p[i], w_in[i], b_gate[i], g_mix[i], q_gain[i], k_gain[i], ret_gn[i],
                   w_ret_out[i], w_att_out[i], w_o[i], g_mlp[i], w_up[i], w_down[i], g_ple[i],
                   w_ple_proj[i], w_ple_gate[i])
    return x
```
